```python
import math
import jax, jax.numpy as jnp
from jax import lax
import numpy as np

D_MODEL = 4096
BATCH = 2
SEQ = 4096
DEPTH = 2

HEAD_DIM = 128
EPS = 1e-6
A_BRANCHES = ((128, 1), (512, 4), (2048, 16))
N_BRANCH = len(A_BRANCHES)
A_HEADS = D_MODEL // (2 * HEAD_DIM)
A_WIDTH = A_HEADS * HEAD_DIM
BAND_BLOCK = 128
POOL_WINDOWS = (2, 4, 8, 16)
B_WIDTH = D_MODEL - A_WIDTH
B_GROUP = B_WIDTH // len(POOL_WINDOWS)
EVEN_IN = 2 * N_BRANCH * A_WIDTH + A_WIDTH + B_WIDTH
C_HEADS = 8
C_KEY_WIDTH = D_MODEL // 2
C_VAL_WIDTH = D_MODEL
C_DK = C_KEY_WIDTH // C_HEADS
C_DV = C_VAL_WIDTH // C_HEADS
C_GATE_RANK = 16
C_GATE_TAU = 16.0
C_CHUNK = 64
ODD_IN = 2 * C_KEY_WIDTH + 2 * C_VAL_WIDTH + C_GATE_RANK
MEM_LEN = 256
X_HEADS = 4
X_WIDTH = X_HEADS * HEAD_DIM
D_FF = ((8 * D_MODEL // 3 + 255) // 256) * 256
CONV_WIDTH = 3
N_EVEN = (DEPTH + 1) // 2
N_ODD = DEPTH // 2

kernel_name = 'hybrid_dilated_pool_gla_memx_convffn'


def rmsnorm(x, g):
    xf = x.astype(jnp.float32)
    y = xf * lax.rsqrt(jnp.mean(xf * xf, axis=-1, keepdims=True) + EPS)
    return (y * g.astype(jnp.float32)).astype(x.dtype)


def alibi_slopes():
    n = N_BRANCH * A_HEADS
    s = np.power(np.float32(2.0), -8.0 * np.arange(1, n + 1, dtype=np.float32) / np.float32(n)).astype(np.float32)
    return jnp.asarray(s.reshape(N_BRANCH, A_HEADS))


def dilated_band_attention(q, k, v, window, dilation, slopes):
    bsz, seq, heads, dh = q.shape
    w_sub = window // dilation
    sub_len = seq // dilation
    n_blk = -(-sub_len // BAND_BLOCK)
    pad_len = n_blk * BAND_BLOCK

    def to_sub(t):
        t = t.reshape(bsz, sub_len, dilation, heads, dh).transpose(0, 2, 3, 1, 4)
        t = jnp.pad(t, ((0, 0), (0, 0), (0, 0), (0, pad_len - sub_len), (0, 0)))
        return t.reshape(bsz, dilation, heads, n_blk, BAND_BLOCK, dh)

    def band(t):
        prev = jnp.concatenate([jnp.zeros_like(t[:, :, :, :1]), t[:, :, :, :-1]], axis=3)
        return jnp.concatenate([prev, t], axis=4)

    qb = to_sub(q)
    kb = band(to_sub(k))
    vb = band(to_sub(v))
    s = jnp.einsum('brhnqe,brhnke->brhnqk', qb, kb, preferred_element_type=jnp.float32) * (dh ** -0.5)
    q_idx = jnp.arange(BAND_BLOCK)[:, None] + BAND_BLOCK
    k_idx = jnp.arange(2 * BAND_BLOCK)[None, :]
    rel = q_idx - k_idx
    key_sub = (jnp.arange(n_blk) * BAND_BLOCK - BAND_BLOCK)[:, None, None] + k_idx[None]
    valid = (rel >= 0) & (rel <= w_sub) & (key_sub >= 0)
    dist = (rel * dilation).astype(jnp.float32)
    s = s - slopes.astype(jnp.float32)[:, None, None, None] * dist
    s = jnp.where(valid, s, -jnp.inf)
    m = jnp.max(s, axis=-1, keepdims=True)
    p = jnp.exp(s - m)
    den = jnp.sum(p, axis=-1, keepdims=True)
    o = jnp.einsum('brhnqk,brhnke->brhnqe', p, vb.astype(jnp.float32)) / den
    lse = (m + jnp.log(den))[..., 0]
    o = o.reshape(bsz, dilation, heads, pad_len, dh)[:, :, :, :sub_len]
    o = o.transpose(0, 3, 1, 2, 4).reshape(bsz, seq, heads, dh)
    lse = lse.reshape(bsz, dilation, heads, pad_len)[..., :sub_len]
    lse = lse.transpose(0, 3, 1, 2).reshape(bsz, seq, heads)
    return o, lse


def multiscale_pool(u, pool_w, pool_scale):
    bsz, seq, _ = u.shape
    uf = u.astype(jnp.float32)
    count = jnp.arange(1, seq + 1, dtype=jnp.float32)[:, None]
    groups = []
    for g, win in enumerate(POOL_WINDOWS):
        seg = uf[..., g * B_GROUP:(g + 1) * B_GROUP]
        cs = jnp.cumsum(seg, axis=1)
        lag = jnp.pad(cs[:, :seq - win], ((0, 0), (win, 0), (0, 0)))
        groups.append((cs - lag) / jnp.minimum(count, win) - seg)
    pooled = jnp.stack(groups, axis=2).astype(u.dtype)
    y = jnp.einsum('bsgc,gcd->bsgd', pooled, pool_w).reshape(bsz, seq, B_WIDTH)
    return y * pool_scale


def dilated_pool_mixer(h, w_in, q_gain, k_gain, pool_w, pool_scale, w_out):
    bsz, seq, _ = h.shape
    qk = N_BRANCH * A_WIDTH
    z = h @ w_in
    q = rmsnorm(z[..., :qk].reshape(bsz, seq, N_BRANCH, A_HEADS, HEAD_DIM), q_gain)
    k = rmsnorm(z[..., qk:2 * qk].reshape(bsz, seq, N_BRANCH, A_HEADS, HEAD_DIM), k_gain)
    v = z[..., 2 * qk:2 * qk + A_WIDTH].reshape(bsz, seq, A_HEADS, HEAD_DIM)
    u = z[..., 2 * qk + A_WIDTH:]
    slopes = alibi_slopes()
    outs, lses = [], []
    for g, (window, dilation) in enumerate(A_BRANCHES):
        o, l = dilated_band_attention(q[:, :, g], k[:, :, g], v, window, dilation, slopes[g])
        outs.append(o)
        lses.append(l)
    wts = jax.nn.softmax(jnp.stack(lses), axis=0)
    a_out = jnp.sum(wts[..., None] * jnp.stack(outs), axis=0).reshape(bsz, seq, A_WIDTH).astype(h.dtype)
    b_out = multiscale_pool(u, pool_w, pool_scale)
    return jnp.concatenate([a_out, b_out], axis=-1) @ w_out


def gla_chunked(q, k, v, log_a):
    bsz, seq, heads, dk = q.shape
    dv = v.shape[-1]
    n_chunk = seq // C_CHUNK

    def chunks(t):
        return t.astype(jnp.float32).reshape(bsz, n_chunk, C_CHUNK, heads, t.shape[-1]).transpose(1, 0, 3, 2, 4)

    qc = chunks(q) * (dk ** -0.5)
    kc = chunks(k)
    vc = chunks(v)
    bc = jnp.cumsum(chunks(log_a), axis=3)
    b_last = bc[:, :, :, -1:]
    q_in = qc * jnp.exp(bc)
    k_in = kc * jnp.exp(-bc)
    k_st = kc * jnp.exp(b_last - bc)
    causal = jnp.tril(jnp.ones((C_CHUNK, C_CHUNK), dtype=bool))
    att = jnp.where(causal, jnp.einsum('nbhid,nbhjd->nbhij', q_in, k_in), 0.0)
    o_intra = jnp.einsum('nbhij,nbhje->nbhie', att, vc)
    decay = jnp.exp(b_last)

    def step(state, xs):
        q_n, k_n, v_n, dec_n = xs
        o_n = jnp.einsum('bhid,bhde->bhie', q_n, state)
        state = dec_n[:, :, 0, :, None] * state + jnp.einsum('bhjd,bhje->bhde', k_n, v_n)
        return state, o_n

    state0 = jnp.zeros((bsz, heads, dk, dv), jnp.float32)
    _, o_inter = lax.scan(step, state0, (q_in, k_st, vc, decay))
    o = o_inter + o_intra
    return o.transpose(1, 0, 3, 2, 4).reshape(bsz, seq, heads, dv)


def gla_mixer(h, w_in, w_a2, b_a, o_gain, w_out):
    bsz, seq, _ = h.shape
    kw, vw = C_KEY_WIDTH, C_VAL_WIDTH
    z = h @ w_in
    q = z[..., :kw].reshape(bsz, seq, C_HEADS, C_DK)
    k = z[..., kw:2 * kw].reshape(bsz, seq, C_HEADS, C_DK)
    v = z[..., 2 * kw:2 * kw + vw].reshape(bsz, seq, C_HEADS, C_DV)
    gate = z[..., 2 * kw + vw:2 * kw + 2 * vw].reshape(bsz, seq, C_HEADS, C_DV)
    r = z[..., 2 * kw + 2 * vw:]
    log_a = jax.nn.log_sigmoid((r @ w_a2 + b_a).astype(jnp.float32)) / C_GATE_TAU
    log_a = log_a.reshape(bsz, seq, C_HEADS, C_DK)
    o = gla_chunked(q, k, v, log_a)
    o = rmsnorm(o, o_gain) * jax.nn.silu(gate.astype(jnp.float32))
    return o.reshape(bsz, seq, vw).astype(h.dtype) @ w_out


def memory_cross_attention(h, mem_n, wq, wkv, q_gain, k_gain, wo):
    bsz, seq, _ = h.shape
    mlen = mem_n.shape[1]
    q = rmsnorm((h @ wq).reshape(bsz, seq, X_HEADS, HEAD_DIM), q_gain)
    kv = mem_n @ wkv
    k = rmsnorm(kv[..., :X_WIDTH].reshape(bsz, mlen, X_HEADS, HEAD_DIM), k_gain)
    v = kv[..., X_WIDTH:].reshape(bsz, mlen, X_HEADS, HEAD_DIM)
    s = jnp.einsum('bshe,bmhe->bhsm', q, k, preferred_element_type=jnp.float32) * (HEAD_DIM ** -0.5)
    p = jax.nn.softmax(s, axis=-1)
    o = jnp.einsum('bhsm,bmhe->bshe', p.astype(v.dtype), v).reshape(bsz, seq, X_WIDTH)
    return o @ wo


def conv_ffn(h, w_up, conv_w, conv_b, w_down):
    seq = h.shape[1]
    u = h @ w_up
    up = jnp.pad(u, ((0, 0), (CONV_WIDTH - 1, 0), (0, 0)))
    c = conv_b
    for i in range(CONV_WIDTH):
        c = c + conv_w[i] * up[:, i:i + seq]
    gate, val = c[..., :D_FF], c[..., D_FF:]
    return (jax.nn.silu(gate) * val) @ w_down


def setup_inputs(seed: int = 0) -> dict:
    key = jax.random.key(seed)
    ks = iter(jax.random.split(key, 40))

    def nrm(shape, scale):
        return jax.random.normal(next(ks), shape, jnp.float32) * scale

    def gain(shape):
        return 1.0 + 0.02 * jax.random.normal(next(ks), shape, jnp.float32)

    d = D_MODEL
    return {
        'x': nrm((BATCH, SEQ, d), 1.0),
        'mem': nrm((BATCH, MEM_LEN, d), 1.0),
        'mix_norm': gain((DEPTH, d)),
        'ab_w_in': nrm((N_EVEN, d, EVEN_IN), d ** -0.5),
        'ab_q_norm': gain((N_EVEN, HEAD_DIM)),
        'ab_k_norm': gain((N_EVEN, HEAD_DIM)),
        'ab_pool_w': nrm((N_EVEN, len(POOL_WINDOWS), B_GROUP, B_GROUP), B_GROUP ** -0.5),
        'ab_pool_scale': gain((N_EVEN, B_WIDTH)),
        'ab_w_out': nrm((N_EVEN, A_WIDTH + B_WIDTH, d), (A_WIDTH + B_WIDTH) ** -0.5),
        'c_w_in': nrm((N_ODD, d, ODD_IN), d ** -0.5),
        'c_w_a2': nrm((N_ODD, C_GATE_RANK, C_KEY_WIDTH), C_GATE_RANK ** -0.5),
        'c_b_a': nrm((N_ODD, C_KEY_WIDTH), 0.1),
        'c_o_norm': gain((N_ODD, C_DV)),
        'c_w_out': nrm((N_ODD, C_VAL_WIDTH, d), C_VAL_WIDTH ** -0.5),
        'x_norm': gain((DEPTH, d)),
        'x_mem_norm': gain((DEPTH, d)),
        'x_wq': nrm((DEPTH, d, X_WIDTH), d ** -0.5),
        'x_wkv': nrm((DEPTH, d, 2 * X_WIDTH), d ** -0.5),
        'x_q_norm': gain((DEPTH, HEAD_DIM)),
        'x_k_norm': gain((DEPTH, HEAD_DIM)),
        'x_wo': nrm((DEPTH, X_WIDTH, d), X_WIDTH ** -0.5),
        'f_norm': gain((DEPTH, d)),
        'f_w_up': nrm((DEPTH, d, 2 * D_FF), d ** -0.5),
        'f_conv_w': nrm((DEPTH, CONV_WIDTH, 2 * D_FF), CONV_WIDTH ** -0.5),
        'f_conv_b': nrm((DEPTH, 2 * D_FF), 0.02),
        'f_w_down': nrm((DEPTH, D_FF, d), D_FF ** -0.5),
    }


def reference(x, mem, mix_norm, ab_w_in, ab_q_norm, ab_k_norm, ab_pool_w, ab_pool_scale, ab_w_out,
              c_w_in, c_w_a2, c_b_a, c_o_norm, c_w_out,
              x_norm, x_mem_norm, x_wq, x_wkv, x_q_norm, x_k_norm, x_wo,
              f_norm, f_w_up, f_conv_w, f_conv_b, f_w_down):
    h = x
    for layer in range(DEPTH):
        j = layer // 2
        hn = rmsnorm(h, mix_norm[layer])
        if layer % 2 == 0:
            mix = dilated_pool_mixer(hn, ab_w_in[j], ab_q_norm[j], ab_k_norm[j], ab_pool_w[j],
                                     ab_pool_scale[j], ab_w_out[j])
        else:
            mix = gla_mixer(hn, c_w_in[j], c_w_a2[j], c_b_a[j], c_o_norm[j], c_w_out[j])
        h = h + mix
        h = h + memory_cross_attention(rmsnorm(h, x_norm[layer]), rmsnorm(mem, x_mem_norm[layer]),
                                       x_wq[layer], x_wkv[layer], x_q_norm[layer], x_k_norm[layer], x_wo[layer])
        h = h + conv_ffn(rmsnorm(h, f_norm[layer]), f_w_up[layer], f_conv_w[layer], f_conv_b[layer], f_w_down[layer])
    return h
```

```python
import functools

import numpy as np
import jax
import jax.numpy as jnp
from jax import lax
from jax.experimental import pallas as pl
from jax.experimental.pallas import tpu as pltpu

F32 = jnp.float32
BF16 = jnp.bfloat16

LANES = 128
SUBLANES = 8
VMEM_LIMIT_BYTES = 56 * 2 ** 20

EPS = 1e-6
HEAD_DIM = 128
A_BRANCHES = ((128, 1), (512, 4), (2048, 16))
N_BRANCH = len(A_BRANCHES)
BAND_BLOCK = 128
POOL_WINDOWS = (2, 4, 8, 16)
POOL_HALO = 16
C_HEADS = 8
C_GATE_RANK = 16
C_GATE_TAU = 16.0
C_CHUNK = 64
X_HEADS = 4
CONV_WIDTH = 3
CONV_HALO = SUBLANES


def _params(*semantics):
    return pltpu.CompilerParams(dimension_semantics=semantics, vmem_limit_bytes=VMEM_LIMIT_BYTES)


def _rmsnorm_kernel(x_ref, g_ref, o_ref):
    x = x_ref[...].astype(F32)
    ms = jnp.mean(x * x, axis=-1, keepdims=True)
    o_ref[...] = (x * lax.rsqrt(ms + EPS) * g_ref[...]).astype(o_ref.dtype)


def _rmsnorm(x, gain, tm=256):
    t, d = x.shape
    return pl.pallas_call(
        _rmsnorm_kernel,
        grid=(t // tm,),
        in_specs=[pl.BlockSpec((tm, d), lambda i: (i, 0)),
                  pl.BlockSpec((1, d), lambda i: (0, 0))],
        out_specs=pl.BlockSpec((tm, d), lambda i: (i, 0)),
        out_shape=jax.ShapeDtypeStruct((t, d), BF16),
        compiler_params=_params("parallel"),
        name="rmsnorm",
    )(x, gain.reshape(1, d))


def _headnorm_store(acc, g_ref, o_ref):
    for c in range(acc.shape[1] // HEAD_DIM):
        sl = slice(c * HEAD_DIM, (c + 1) * HEAD_DIM)
        blk = acc[:, sl]
        ms = jnp.mean(blk * blk, axis=-1, keepdims=True)
        o_ref[:, sl] = (blk * lax.rsqrt(ms + EPS) * g_ref[:, sl]).astype(o_ref.dtype)


def _mm_kernel(*refs, n_a, mode):
    a_refs = refs[:n_a]
    w_ref = refs[n_a]
    extra = refs[n_a + 1:-1]
    o_ref = refs[-1]
    acc = None
    k0 = 0
    for a_ref in a_refs:
        kk = a_ref.shape[1]
        part = jnp.dot(a_ref[...], w_ref[k0:k0 + kk, :].astype(BF16), preferred_element_type=F32)
        acc = part if acc is None else acc + part
        k0 += kk
    if mode == "headnorm":
        _headnorm_store(acc, extra[0], o_ref)
    elif mode == "residual":
        o_ref[...] = (extra[0][...] + acc).astype(o_ref.dtype)
    else:
        o_ref[...] = acc.astype(o_ref.dtype)


def _mm(a_list, w, layer, col_off, ncols, out_dtype, *, mode="plain", extra=None, tm=1024, tn=512, name="mm"):
    t = a_list[0].shape[0]
    k_total = sum(a.shape[1] for a in a_list)
    assert w.shape[1] == k_total and col_off % tn == 0 and ncols % tn == 0 and t % tm == 0
    off = col_off // tn
    in_specs = [pl.BlockSpec((tm, a.shape[1]), lambda i, j: (i, 0)) for a in a_list]
    in_specs.append(pl.BlockSpec((None, k_total, tn), lambda i, j: (layer, 0, j + off)))
    args = list(a_list) + [w]
    if mode == "headnorm":
        in_specs.append(pl.BlockSpec((1, tn), lambda i, j: (0, j)))
        args.append(extra)
    elif mode == "residual":
        in_specs.append(pl.BlockSpec((tm, tn), lambda i, j: (i, j)))
        args.append(extra)
    return pl.pallas_call(
        functools.partial(_mm_kernel, n_a=len(a_list), mode=mode),
        grid=(t // tm, ncols // tn),
        in_specs=in_specs,
        out_specs=pl.BlockSpec((tm, tn), lambda i, j: (i, j)),
        out_shape=jax.ShapeDtypeStruct((t, ncols), out_dtype),
        compiler_params=_params("parallel", "arbitrary"),
        name=name,
    )(*args)


def _mmk_kernel(a_ref, w_ref, r_ref, o_ref, acc_ref, *, nk):
    k = pl.program_id(2)

    @pl.when(k == 0)
    def _():
        acc_ref[...] = jnp.zeros_like(acc_ref)

    acc_ref[...] += jnp.dot(a_ref[...], w_ref[...].astype(BF16), preferred_element_type=F32)

    @pl.when(k == nk - 1)
    def _():
        o_ref[...] = r_ref[...] + acc_ref[...]


def _mm_ktiled_residual(a, w, layer, res, *, tm=1024, tn=2048, tk=256, name="mmk"):
    t, k_total = a.shape
    n = w.shape[2]
    nk = k_total // tk
    assert k_total % tk == 0 and n % tn == 0 and t % tm == 0
    return pl.pallas_call(
        functools.partial(_mmk_kernel, nk=nk),
        grid=(t // tm, n // tn, nk),
        in_specs=[pl.BlockSpec((tm, tk), lambda i, j, k: (i, k)),
                  pl.BlockSpec((None, tk, tn), lambda i, j, k: (layer, k, j)),
                  pl.BlockSpec((tm, tn), lambda i, j, k: (i, j))],
        out_specs=pl.BlockSpec((tm, tn), lambda i, j, k: (i, j)),
        out_shape=jax.ShapeDtypeStruct((t, n), F32),
        scratch_shapes=[pltpu.VMEM((tm, tn), F32)],
        compiler_params=_params("parallel", "parallel", "arbitrary"),
        name=name,
    )(a, w, res)


def _alibi_slopes():
    n = N_BRANCH * (2048 // HEAD_DIM)
    s = np.power(np.float32(2.0), -8.0 * np.arange(1, n + 1, dtype=np.float32) / np.float32(n)).astype(np.float32)
    return s.reshape(N_BRANCH, -1)


def _band_attn_kernel(q_ref, kp_ref, kc_ref, vp_ref, vc_ref, o_ref, lse_ref, *, dilation, slopes):
    blk = BAND_BLOCK
    has_prev = pl.program_id(2) > 0
    qi = lax.broadcasted_iota(jnp.int32, (blk, blk), 0)
    ki = lax.broadcasted_iota(jnp.int32, (blk, blk), 1)
    valid_p = jnp.logical_and(ki >= qi, has_prev)
    valid_c = ki <= qi
    dist_p = ((qi + blk - ki) * dilation).astype(F32)
    dist_c = ((qi - ki) * dilation).astype(F32)
    lane = lax.broadcasted_iota(jnp.int32, (blk, LANES), 1)
    lse_tile = jnp.zeros((blk, LANES), F32)
    scale = HEAD_DIM ** -0.5
    contract_last = (((1,), (1,)), ((), ()))
    for h, slope in enumerate(slopes):
        sl = slice(h * HEAD_DIM, (h + 1) * HEAD_DIM)
        q = q_ref[0, :, sl]
        s_p = lax.dot_general(q, kp_ref[0, :, sl], contract_last, preferred_element_type=F32) * scale
        s_c = lax.dot_general(q, kc_ref[0, :, sl], contract_last, preferred_element_type=F32) * scale
        s_p = jnp.where(valid_p, s_p - slope * dist_p, -jnp.inf)
        s_c = jnp.where(valid_c, s_c - slope * dist_c, -jnp.inf)
        m = jnp.maximum(jnp.max(s_p, axis=-1, keepdims=True), jnp.max(s_c, axis=-1, keepdims=True))
        p_p = jnp.exp(s_p - m)
        p_c = jnp.exp(s_c - m)
        den = jnp.sum(p_p, axis=-1, keepdims=True) + jnp.sum(p_c, axis=-1, keepdims=True)
        o = (jnp.dot(p_p.astype(BF16), vp_ref[0, :, sl], preferred_element_type=F32)
             + jnp.dot(p_c.astype(BF16), vc_ref[0, :, sl], preferred_element_type=F32))
        o_ref[0, :, sl] = o / den
        lse_tile = jnp.where(lane == h, m + jnp.log(den), lse_tile)
    lse_ref[0] = lse_tile


def _band_attention(qk, v, bsz, seq, branch, slopes):
    _, dilation = A_BRANCHES[branch]
    width = v.shape[1]
    sub_len = seq // dilation
    n_blk = sub_len // BAND_BLOCK
    qk_cols = qk.shape[1] // width
    qk3 = qk.reshape(bsz, sub_len, dilation * qk.shape[1])
    v3 = v.reshape(bsz, sub_len, dilation * width)
    blk = (1, BAND_BLOCK, width)

    def prev(n):
        return jnp.maximum(n - 1, 0)

    o, lse = pl.pallas_call(
        functools.partial(_band_attn_kernel, dilation=dilation, slopes=tuple(float(s) for s in slopes)),
        grid=(bsz, dilation, n_blk),
        in_specs=[pl.BlockSpec(blk, lambda b, r, n: (b, n, r * qk_cols + branch)),
                  pl.BlockSpec(blk, lambda b, r, n: (b, prev(n), r * qk_cols + N_BRANCH + branch)),
                  pl.BlockSpec(blk, lambda b, r, n: (b, n, r * qk_cols + N_BRANCH + branch)),
                  pl.BlockSpec(blk, lambda b, r, n: (b, prev(n), r)),
                  pl.BlockSpec(blk, lambda b, r, n: (b, n, r))],
        out_specs=[pl.BlockSpec(blk, lambda b, r, n: (b, n, r)),
                   pl.BlockSpec((1, BAND_BLOCK, LANES), lambda b, r, n: (b, n, r))],
        out_shape=[jax.ShapeDtypeStruct((bsz, sub_len, dilation * width), F32),
                   jax.ShapeDtypeStruct((bsz, sub_len, dilation * LANES), F32)],
        compiler_params=_params("parallel", "parallel", "arbitrary"),
        name=f"band_attn_d{dilation}",
    )(qk3, qk3, qk3, v3, v3)
    return o.reshape(bsz * seq, width), lse.reshape(bsz * seq, LANES)


def _combine_kernel(o0_ref, o1_ref, o2_ref, l0_ref, l1_ref, l2_ref, out_ref):
    l0, l1, l2 = l0_ref[...], l1_ref[...], l2_ref[...]
    m = jnp.maximum(jnp.maximum(l0, l1), l2)
    e0, e1, e2 = jnp.exp(l0 - m), jnp.exp(l1 - m), jnp.exp(l2 - m)
    tot = e0 + e1 + e2
    w0, w1, w2 = e0 / tot, e1 / tot, e2 / tot
    for h in range(out_ref.shape[1] // HEAD_DIM):
        sl = slice(h * HEAD_DIM, (h + 1) * HEAD_DIM)
        acc = w0[:, h:h + 1] * o0_ref[:, sl] + w1[:, h:h + 1] * o1_ref[:, sl] + w2[:, h:h + 1] * o2_ref[:, sl]
        out_ref[:, sl] = acc.astype(out_ref.dtype)


def _combine_branches(outs, lses, tm=512):
    t, width = outs[0].shape
    o_spec = pl.BlockSpec((tm, width), lambda i: (i, 0))
    l_spec = pl.BlockSpec((tm, LANES), lambda i: (i, 0))
    return pl.pallas_call(
        _combine_kernel,
        grid=(t // tm,),
        in_specs=[o_spec] * 3 + [l_spec] * 3,
        out_specs=o_spec,
        out_shape=jax.ShapeDtypeStruct((t, width), BF16),
        compiler_params=_params("parallel"),
        name="combine_branches",
    )(*outs, *lses)


def _pool_kernel(u_ref, uh_ref, w_ref, sc_ref, o_ref, wb_ref, *, blocks_per_seq):
    i = pl.program_id(0)

    @pl.when(i == 0)
    def _():
        wb_ref[...] = w_ref[...].astype(BF16)

    tm = u_ref.shape[0]
    group = w_ref.shape[1]
    first = (i % blocks_per_seq) == 0
    row = lax.broadcasted_iota(jnp.int32, (tm, 1), 0)
    pos = (i % blocks_per_seq) * tm + row
    for g, win in enumerate(POOL_WINDOWS):
        sl = slice(g * group, (g + 1) * group)
        u = u_ref[:, sl]
        halo = jnp.where(first, 0.0, uh_ref[:, sl])
        s = jnp.concatenate([halo, u], axis=0)
        step = 1
        while step < win:
            s = s + pltpu.roll(s, step, axis=0)
            step *= 2
        count = jnp.minimum(pos + 1, win).astype(F32)
        pooled = s[POOL_HALO:] / count - u
        y = jnp.dot(pooled.astype(BF16), wb_ref[g], preferred_element_type=F32)
        o_ref[:, sl] = (y * sc_ref[:, sl]).astype(o_ref.dtype)


def _pool_mixer(u, pool_w, pool_scale, seq, tm=512):
    t, width = u.shape
    n_group, group, _ = pool_w.shape
    blocks_per_seq = seq // tm
    halo_blocks = tm // POOL_HALO
    return pl.pallas_call(
        functools.partial(_pool_kernel, blocks_per_seq=blocks_per_seq),
        grid=(t // tm,),
        in_specs=[pl.BlockSpec((tm, width), lambda i: (i, 0)),
                  pl.BlockSpec((POOL_HALO, width), lambda i: (jnp.maximum(i * halo_blocks - 1, 0), 0)),
                  pl.BlockSpec((n_group, group, group), lambda i: (0, 0, 0)),
                  pl.BlockSpec((1, width), lambda i: (0, 0))],
        out_specs=pl.BlockSpec((tm, width), lambda i: (i, 0)),
        out_shape=jax.ShapeDtypeStruct((t, width), BF16),
        scratch_shapes=[pltpu.VMEM((n_group, group, group), BF16)],
        compiler_params=_params("arbitrary"),
        name="pool_mixer",
    )(u, u, pool_w, pool_scale.reshape(1, width))


def _split3(x):
    hi = x.astype(BF16)
    r1 = x - hi.astype(F32)
    mid = r1.astype(BF16)
    lo = (r1 - mid.astype(F32)).astype(BF16)
    return hi, mid, lo


def _gate_kernel(a_ref, wr_ref, wa2_ref, ba_ref, o_ref):
    r = jnp.dot(a_ref[...], wr_ref[...].astype(BF16), preferred_element_type=F32)
    r_hi, r_mid, r_lo = _split3(r)
    w_hi, w_mid, w_lo = _split3(wa2_ref[...])
    g = (jnp.dot(r_hi, w_hi, preferred_element_type=F32)
         + (jnp.dot(r_hi, w_mid, preferred_element_type=F32) + jnp.dot(r_mid, w_hi, preferred_element_type=F32))
         + (jnp.dot(r_hi, w_lo, preferred_element_type=F32) + jnp.dot(r_mid, w_mid, preferred_element_type=F32)
            + jnp.dot(r_lo, w_hi, preferred_element_type=F32)))
    g = g + ba_ref[...]
    log_sig = jnp.minimum(g, 0.0) - jnp.log1p(jnp.exp(-jnp.abs(g)))
    o_ref[...] = log_sig / C_GATE_TAU


def _gla_log_decay(hn, w_r, w_a2, b_a, tm=512):
    t, d = hn.shape
    kw = w_a2.shape[1]
    return pl.pallas_call(
        _gate_kernel,
        grid=(t // tm,),
        in_specs=[pl.BlockSpec((tm, d), lambda i: (i, 0)),
                  pl.BlockSpec((d, LANES), lambda i: (0, 0)),
                  pl.BlockSpec((LANES, kw), lambda i: (0, 0)),
                  pl.BlockSpec((1, kw), lambda i: (0, 0))],
        out_specs=pl.BlockSpec((tm, kw), lambda i: (i, 0)),
        out_shape=jax.ShapeDtypeStruct((t, kw), F32),
        compiler_params=_params("parallel"),
        name="gla_log_decay",
    )(hn, w_r, w_a2, b_a.reshape(1, kw))


def _gla_kernel(q_ref, k_ref, v_ref, la_ref, gate_ref, og_ref, o_ref, state_ref, *, chunks_per_step):
    @pl.when(pl.program_id(2) == 0)
    def _():
        state_ref[...] = jnp.zeros_like(state_ref)

    c = C_CHUNK
    dk = q_ref.shape[1]
    ri = lax.broadcasted_iota(jnp.int32, (c, c), 0)
    ci = lax.broadcasted_iota(jnp.int32, (c, c), 1)
    causal = ci <= ri
    tri = causal.astype(BF16)
    contract_last = (((1,), (1,)), ((), ()))
    contract_first = (((0,), (0,)), ((), ()))
    for n in range(chunks_per_step):
        rows = slice(n * c, (n + 1) * c)
        la = la_ref[rows, :]
        la_hi, la_mid, la_lo = _split3(la)
        bc = (jnp.dot(tri, la_hi, preferred_element_type=F32)
              + jnp.dot(tri, la_mid, preferred_element_type=F32)
              + jnp.dot(tri, la_lo, preferred_element_type=F32))
        b_last = bc[c - 1:c, :]
        q_in = q_ref[rows, :] * (dk ** -0.5) * jnp.exp(bc)
        k = k_ref[rows, :]
        k_in = k * jnp.exp(-bc)
        k_st = k * jnp.exp(b_last - bc)
        v = v_ref[rows, :]
        q_b = q_in.astype(BF16)
        att = lax.dot_general(q_b, k_in.astype(BF16), contract_last, preferred_element_type=F32)
        att = jnp.where(causal, att, 0.0)
        o = jnp.dot(att.astype(BF16), v, preferred_element_type=F32)
        state = state_ref[...]
        o = o + lax.dot_general(q_b, state.astype(BF16), contract_last, preferred_element_type=F32)
        upd = lax.dot_general(v, k_st.astype(BF16), contract_first, preferred_element_type=F32)
        state_ref[...] = state * jnp.exp(b_last) + upd
        ms = jnp.mean(o * o, axis=-1, keepdims=True)
        gate = gate_ref[rows, :]
        y = o * lax.rsqrt(ms + EPS) * og_ref[...] * (gate * jax.nn.sigmoid(gate))
        o_ref[rows, :] = y.astype(o_ref.dtype)


def _gla(qk, v, gate, log_a, o_gain, bsz, seq, rows=256):
    t = bsz * seq
    dk = log_a.shape[1] // C_HEADS
    dv = v.shape[1] // C_HEADS
    steps = seq // rows

    def row_block(b, n):
        return b * steps + n

    return pl.pallas_call(
        functools.partial(_gla_kernel, chunks_per_step=rows // C_CHUNK),
        grid=(bsz, C_HEADS, steps),
        in_specs=[pl.BlockSpec((rows, dk), lambda b, h, n: (row_block(b, n), h)),
                  pl.BlockSpec((rows, dk), lambda b, h, n: (row_block(b, n), C_HEADS + h)),
                  pl.BlockSpec((rows, dv), lambda b, h, n: (row_block(b, n), h)),
                  pl.BlockSpec((rows, dk), lambda b, h, n: (row_block(b, n), h)),
                  pl.BlockSpec((rows, dv), lambda b, h, n: (row_block(b, n), h)),
                  pl.BlockSpec((1, dv), lambda b, h, n: (0, 0))],
        out_specs=pl.BlockSpec((rows, dv), lambda b, h, n: (row_block(b, n), h)),
        out_shape=jax.ShapeDtypeStruct((t, v.shape[1]), BF16),
        scratch_shapes=[pltpu.VMEM((dv, dk), F32)],
        compiler_params=_params("parallel", "parallel", "arbitrary"),
        name="gla",
    )(qk, qk, v, log_a, gate, o_gain.reshape(1, dv))


def _xattn_kernel(h_ref, g_ref, wq_ref, qg_ref, k_ref, v_ref, wo_ref, o_ref):
    x = h_ref[...]
    ms = jnp.mean(x * x, axis=-1, keepdims=True)
    hn = (x * lax.rsqrt(ms + EPS) * g_ref[...]).astype(BF16)
    q = jnp.dot(hn, wq_ref[...], preferred_element_type=F32)
    scale = HEAD_DIM ** -0.5
    contract_last = (((1,), (1,)), ((), ()))
    heads = []
    for hd in range(X_HEADS):
        sl = slice(hd * HEAD_DIM, (hd + 1) * HEAD_DIM)
        qh = q[:, sl]
        qms = jnp.mean(qh * qh, axis=-1, keepdims=True)
        qh = (qh * lax.rsqrt(qms + EPS) * qg_ref[...]).astype(BF16)
        s = lax.dot_general(qh, k_ref[0, :, sl], contract_last, preferred_element_type=F32) * scale
        m = jnp.max(s, axis=-1, keepdims=True)
        p = jnp.exp(s - m)
        p = p / jnp.sum(p, axis=-1, keepdims=True)
        heads.append(jnp.dot(p.astype(BF16), v_ref[0, :, sl], preferred_element_type=F32))
    o = jnp.concatenate(heads, axis=-1).astype(BF16)
    o_ref[...] = x + jnp.dot(o, wo_ref[...], preferred_element_type=F32)


def _cross_attention(h, norm_gain, wq, q_gain, k, v, wo, seq, tm=256):
    t, d = h.shape
    xw = wq.shape[1]
    mlen = k.shape[1]
    tiles_per_seq = seq // tm
    return pl.pallas_call(
        _xattn_kernel,
        grid=(t // tm,),
        in_specs=[pl.BlockSpec((tm, d), lambda i: (i, 0)),
                  pl.BlockSpec((1, d), lambda i: (0, 0)),
                  pl.BlockSpec((d, xw), lambda i: (0, 0)),
                  pl.BlockSpec((1, HEAD_DIM), lambda i: (0, 0)),
                  pl.BlockSpec((1, mlen, xw), lambda i: (i // tiles_per_seq, 0, 0)),
                  pl.BlockSpec((1, mlen, xw), lambda i: (i // tiles_per_seq, 0, 0)),
                  pl.BlockSpec((xw, d), lambda i: (0, 0))],
        out_specs=pl.BlockSpec((tm, d), lambda i: (i, 0)),
        out_shape=jax.ShapeDtypeStruct((t, d), F32),
        compiler_params=_params("parallel"),
        name="cross_attention",
    )(h, norm_gain.reshape(1, d), wq, q_gain.reshape(1, HEAD_DIM), k, v, wo)


def _ffn_up_kernel(a_ref, ah_ref, wg_ref, wv_ref, cwg_ref, cwv_ref, cbg_ref, cbv_ref, o_ref, *, blocks_per_seq):
    tm = a_ref.shape[0]
    first = (pl.program_id(0) % blocks_per_seq) == 0
    row = lax.broadcasted_iota(jnp.int32, (tm, 1), 0)
    a = a_ref[...]
    ah = ah_ref[...]

    def conv_half(w_ref, cw_ref, cb_ref):
        w = w_ref[...].astype(BF16)
        u = jnp.dot(a, w, preferred_element_type=F32)
        uh = jnp.where(first, 0.0, jnp.dot(ah, w, preferred_element_type=F32))
        u1 = jnp.where(row == 0, uh[CONV_HALO - 1:CONV_HALO], pltpu.roll(u, 1, axis=0))
        u2 = jnp.where(row == 0, uh[CONV_HALO - 2:CONV_HALO - 1],
                       jnp.where(row == 1, uh[CONV_HALO - 1:CONV_HALO], pltpu.roll(u, 2, axis=0)))
        cw = cw_ref[...]
        return cb_ref[...] + cw[0:1] * u2 + cw[1:2] * u1 + cw[2:3] * u

    cg = conv_half(wg_ref, cwg_ref, cbg_ref)
    cv = conv_half(wv_ref, cwv_ref, cbv_ref)
    o_ref[...] = (cg * jax.nn.sigmoid(cg) * cv).astype(o_ref.dtype)


def _ffn_up(hn, w_up, conv_w, conv_b, layer, seq, tm=1024, tn=256):
    t, d = hn.shape
    d_ff = w_up.shape[2] // 2
    n_tiles = d_ff // tn
    blocks_per_seq = seq // tm
    halo_blocks = tm // CONV_HALO
    cw = conv_w[layer]
    cb = conv_b[layer].reshape(1, 2 * d_ff)
    return pl.pallas_call(
        functools.partial(_ffn_up_kernel, blocks_per_seq=blocks_per_seq),
        grid=(t // tm, n_tiles),
        in_specs=[pl.BlockSpec((tm, d), lambda i, j: (i, 0)),
                  pl.BlockSpec((CONV_HALO, d), lambda i, j: (jnp.maximum(i * halo_blocks - 1, 0), 0)),
                  pl.BlockSpec((None, d, tn), lambda i, j: (layer, 0, j)),
                  pl.BlockSpec((None, d, tn), lambda i, j: (layer, 0, n_tiles + j)),
                  pl.BlockSpec((CONV_WIDTH, tn), lambda i, j: (0, j)),
                  pl.BlockSpec((CONV_WIDTH, tn), lambda i, j: (0, n_tiles + j)),
                  pl.BlockSpec((1, tn), lambda i, j: (0, j)),
                  pl.BlockSpec((1, tn), lambda i, j: (0, n_tiles + j))],
        out_specs=pl.BlockSpec((tm, tn), lambda i, j: (i, j)),
        out_shape=jax.ShapeDtypeStruct((t, d_ff), BF16),
        compiler_params=_params("parallel", "arbitrary"),
        name="ffn_up",
    )(hn, hn, w_up, w_up, cw, cw, cb, cb)


def _dilated_pool_layer(h, hn, w_in, q_gain, k_gain, pool_w, pool_scale, w_out, j, bsz, seq):
    d = h.shape[1]
    a_width = d // 2
    qk_width = 2 * N_BRANCH * a_width
    n_heads = N_BRANCH * a_width // HEAD_DIM
    gains = jnp.concatenate([jnp.tile(q_gain, n_heads), jnp.tile(k_gain, n_heads)]).reshape(1, qk_width)
    qk = _mm([hn], w_in, j, 0, qk_width, BF16, mode="headnorm", extra=gains, name="ab_in_qk")
    v = _mm([hn], w_in, j, qk_width, a_width, BF16, name="ab_in_v")
    u = _mm([hn], w_in, j, qk_width + a_width, d - a_width, F32, name="ab_in_u")
    slopes = _alibi_slopes()
    outs, lses = [], []
    for g in range(N_BRANCH):
        o, lse = _band_attention(qk, v, bsz, seq, g, slopes[g])
        outs.append(o)
        lses.append(lse)
    a_out = _combine_branches(outs, lses)
    b_out = _pool_mixer(u, pool_w, pool_scale, seq)
    return _mm([a_out, b_out], w_out, j, 0, d, F32, mode="residual", extra=h, name="ab_out")


def _gla_layer(h, hn, w_in, w_a2, b_a, o_gain, w_out, j, bsz, seq):
    d = h.shape[1]
    kw = w_a2.shape[2]
    vw = w_out.shape[1]
    qk = _mm([hn], w_in, j, 0, 2 * kw, F32, name="c_in_qk")
    v = _mm([hn], w_in, j, 2 * kw, vw, BF16, name="c_in_v")
    gate = _mm([hn], w_in, j, 2 * kw + vw, vw, F32, name="c_in_gate")
    rank = w_a2.shape[1]
    w_r = jnp.pad(w_in[j, :, 2 * kw + 2 * vw:], ((0, 0), (0, LANES - rank)))
    w_a2p = jnp.pad(w_a2[j], ((0, LANES - rank), (0, 0)))
    log_a = _gla_log_decay(hn, w_r, w_a2p, b_a[j])
    o = _gla(qk, v, gate, log_a, o_gain[j], bsz, seq)
    return _mm([o], w_out, j, 0, d, F32, mode="residual", extra=h, name="c_out")


def _memory_kv(mem, gain, wkv, k_gain, layer, bsz):
    xw = wkv.shape[2] // 2
    mem_n = _rmsnorm(mem, gain)
    rows = mem.shape[0]
    gains = jnp.tile(k_gain, xw // HEAD_DIM).reshape(1, xw)
    k = _mm([mem_n], wkv, layer, 0, xw, BF16, mode="headnorm", extra=gains, tm=rows, tn=256, name="mem_k")
    v = _mm([mem_n], wkv, layer, xw, xw, BF16, tm=rows, tn=256, name="mem_v")
    return k.reshape(bsz, rows // bsz, xw), v.reshape(bsz, rows // bsz, xw)


def kernel(x, mem, mix_norm, ab_w_in, ab_q_norm, ab_k_norm, ab_pool_w, ab_pool_scale, ab_w_out, c_w_in, c_w_a2, c_b_a, c_o_norm, c_w_out, x_norm, x_mem_norm, x_wq, x_wkv, x_q_norm, x_k_norm, x_wo, f_norm, f_w_up, f_conv_w, f_conv_b, f_w_down):
    bsz, seq, d = x.shape
    depth = mix_norm.shape[0]
    h = x.reshape(bsz * seq, d)
    mem2 = mem.reshape(bsz * mem.shape[1], d)
    for layer in range(depth):
        j = layer // 2
        hn = _rmsnorm(h, mix_norm[layer])
        if layer % 2 == 0:
            h = _dilated_pool_layer(h, hn, ab_w_in, ab_q_norm[j], ab_k_norm[j], ab_pool_w[j], ab_pool_scale[j],
                                    ab_w_out, j, bsz, seq)
        else:
            h = _gla_layer(h, hn, c_w_in, c_w_a2, c_b_a, c_o_norm, c_w_out, j, bsz, seq)
        k, v = _memory_kv(mem2, x_mem_norm[layer], x_wkv, x_k_norm[layer], layer, bsz)
        h = _cross_attention(h, x_norm[layer], x_wq[layer].astype(BF16), x_q_norm[layer], k, v,
                             x_wo[layer].astype(BF16), seq)
        hn = _rmsnorm(h, f_norm[layer])
        act = _ffn_up(hn, f_w_up, f_conv_w, f_conv_b, layer, seq)
        h = _mm_ktiled_residual(act, f_w_down, layer, h)
    return h.reshape(bsz, seq, d)
```

```python
import functools

import numpy as np
import jax
import jax.numpy as jnp
from jax import lax
from jax.experimental import pallas as pl
from jax.experimental.pallas import tpu as pltpu

F32 = jnp.float32
BF16 = jnp.bfloat16

LANES = 128
SUBLANES = 8
VMEM_LIMIT_BYTES = 56 * 2 ** 20

EPS = 1e-6
HEAD_DIM = 128
A_BRANCHES = ((128, 1), (512, 4), (2048, 16))
N_BRANCH = len(A_BRANCHES)
BAND_BLOCK = 128
POOL_WINDOWS = (2, 4, 8, 16)
POOL_HALO = 16
C_HEADS = 8
C_GATE_RANK = 16
C_GATE_TAU = 16.0
C_CHUNK = 64
X_HEADS = 4
CONV_WIDTH = 3
CONV_HALO = SUBLANES


def _params(*semantics):
    return pltpu.CompilerParams(dimension_semantics=semantics, vmem_limit_bytes=VMEM_LIMIT_BYTES)


def _lane_groups(width):
    return [slice(c * LANES, (c + 1) * LANES) for c in range(width // LANES)]


def _rmsnorm_kernel(x_ref, g_ref, o_ref):
    x = x_ref[...].astype(F32)
    ms = jnp.mean(x * x, axis=-1, keepdims=True)
    o_ref[...] = (x * lax.rsqrt(ms + EPS) * g_ref[...]).astype(o_ref.dtype)


def _rmsnorm(x, gain, tm=256):
    t, d = x.shape
    return pl.pallas_call(
        _rmsnorm_kernel,
        grid=(t // tm,),
        in_specs=[pl.BlockSpec((tm, d), lambda i: (i, 0)),
                  pl.BlockSpec((1, d), lambda i: (0, 0))],
        out_specs=pl.BlockSpec((tm, d), lambda i: (i, 0)),
        out_shape=jax.ShapeDtypeStruct((t, d), BF16),
        compiler_params=_params("parallel"),
        name="rmsnorm",
    )(x, gain.reshape(1, d))


def _mm_kernel(*refs, n_a, mode, dils):
    a_refs = refs[:n_a]
    w_ref = refs[n_a]
    n_extra = 1 if mode in ("headnorm", "residual") else 0
    extra = refs[n_a + 1:n_a + 1 + n_extra]
    o_refs = refs[n_a + 1 + n_extra:n_a + 1 + n_extra + len(dils)]
    scratch = refs[n_a + 1 + n_extra + len(dils):]
    acc = None
    k0 = 0
    for a_ref in a_refs:
        kk = a_ref.shape[1]
        part = jnp.dot(a_ref[...], w_ref[k0:k0 + kk, :].astype(BF16), preferred_element_type=F32)
        acc = part if acc is None else acc + part
        k0 += kk
    tm, tn = acc.shape
    for c, sl in enumerate(_lane_groups(tn)):
        blk = acc[:, sl]
        if mode == "headnorm":
            ms = jnp.mean(blk * blk, axis=-1, keepdims=True)
            blk = blk * lax.rsqrt(ms + EPS) * extra[0][:, sl]
        elif mode == "residual":
            blk = extra[0][:, sl] + blk
        if scratch:
            scratch[0][c] = blk
        for o_ref, dil in zip(o_refs, dils):
            if dil == 1:
                o_ref[:, sl] = blk.astype(o_ref.dtype)
    for o_ref, dil in zip(o_refs, dils):
        if dil > 1:
            for r in range(dil):
                for c, sl in enumerate(_lane_groups(tn)):
                    o_ref[r, :, sl] = scratch[0][c, pl.ds(r, tm // dil, stride=dil), :].astype(o_ref.dtype)


def _mm(a_list, w, layer, wcol, ncols, out_dtype, *, mode="plain", extra=None, dils=(1,), seq=None,
        tm=1024, tn=512, name="mm"):
    t = a_list[0].shape[0]
    k_total = sum(a.shape[1] for a in a_list)
    assert w.shape[-2] == k_total and ncols % tn == 0 and t % tm == 0
    in_specs = [pl.BlockSpec((tm, a.shape[1]), lambda i, j: (i, 0)) for a in a_list]
    if w.ndim == 3:
        in_specs.append(pl.BlockSpec((None, k_total, tn), lambda i, j: (layer, 0, wcol(j))))
    else:
        in_specs.append(pl.BlockSpec((k_total, tn), lambda i, j: (0, wcol(j))))
    args = list(a_list) + [w]
    if mode == "headnorm":
        in_specs.append(pl.BlockSpec((1, tn), lambda i, j: (0, j)))
        args.append(extra)
    elif mode == "residual":
        in_specs.append(pl.BlockSpec((tm, tn), lambda i, j: (i, j)))
        args.append(extra)
    out_specs, out_shapes = [], []
    for dil in dils:
        if dil == 1:
            out_specs.append(pl.BlockSpec((tm, tn), lambda i, j: (i, j)))
            out_shapes.append(jax.ShapeDtypeStruct((t, ncols), out_dtype))
        else:
            tiles = seq // tm
            assert seq % tm == 0 and tm % dil == 0
            out_specs.append(pl.BlockSpec((None, dil, tm // dil, tn), lambda i, j: (i // tiles, 0, i % tiles, j)))
            out_shapes.append(jax.ShapeDtypeStruct((t // seq, dil, seq // dil, ncols), out_dtype))
    scratch = [pltpu.VMEM((tn // LANES, tm, LANES), F32)] if any(dil > 1 for dil in dils) else []
    outs = pl.pallas_call(
        functools.partial(_mm_kernel, n_a=len(a_list), mode=mode, dils=tuple(dils)),
        grid=(t // tm, ncols // tn),
        in_specs=in_specs,
        out_specs=out_specs,
        out_shape=out_shapes,
        scratch_shapes=scratch,
        compiler_params=_params("parallel", "arbitrary"),
        name=name,
    )(*args)
    return outs[0] if len(dils) == 1 else outs


def _alibi_slopes():
    n = N_BRANCH * (2048 // HEAD_DIM)
    s = np.power(np.float32(2.0), -8.0 * np.arange(1, n + 1, dtype=np.float32) / np.float32(n)).astype(np.float32)
    return s.reshape(N_BRANCH, -1)


def _band_attn_kernel(q_ref, kp_ref, kc_ref, vp_ref, vc_ref, o_ref, lse_ref, *scratch, dilation, slopes):
    blk = BAND_BLOCK
    hg = pl.program_id(2)
    has_prev = pl.program_id(1) > 0
    n_heads = len(slopes[0])
    qi = lax.broadcasted_iota(jnp.int32, (blk, 2 * blk), 0)
    ki = lax.broadcasted_iota(jnp.int32, (blk, 2 * blk), 1)
    rel = qi + blk - ki
    valid = jnp.logical_and(jnp.logical_and(rel >= 0, rel <= blk), jnp.logical_or(ki >= blk, has_prev))
    dist = (rel * dilation).astype(F32)
    lane = lax.broadcasted_iota(jnp.int32, (blk, LANES), 1)
    scale = HEAD_DIM ** -0.5
    contract_last = (((1,), (1,)), ((), ()))
    for r in range(dilation):
        lse_tile = jnp.zeros((blk, LANES), F32)
        for h in range(n_heads):
            sl = slice(h * HEAD_DIM, (h + 1) * HEAD_DIM)
            k = jnp.concatenate([kp_ref[r, :, sl], kc_ref[r, :, sl]], axis=0)
            v = jnp.concatenate([vp_ref[r, :, sl], vc_ref[r, :, sl]], axis=0)
            s = lax.dot_general(q_ref[r, :, sl], k, contract_last, preferred_element_type=F32) * scale
            slope = slopes[0][h]
            for g in range(1, len(slopes)):
                slope = jnp.where(hg == g, slopes[g][h], slope)
            s = jnp.where(valid, s - slope * dist, -jnp.inf)
            m = jnp.max(s, axis=-1, keepdims=True)
            p = jnp.exp(s - m)
            den = jnp.sum(p, axis=-1, keepdims=True)
            o = jnp.dot(p.astype(BF16), v, preferred_element_type=F32) / den
            lse_h = m + jnp.log(den)
            for g in range(len(slopes)):
                lse_tile = jnp.where(jnp.logical_and(lane == g * n_heads + h, hg == g), lse_h, lse_tile)
            if dilation == 1:
                o_ref[:, sl] = o
            else:
                scratch[0][h, pl.ds(r, blk, stride=dilation), :] = o
        if dilation == 1:
            lse_slab = lse_tile
        else:
            scratch[1][pl.ds(r, blk, stride=dilation), :] = lse_tile
    if dilation > 1:
        for h in range(n_heads):
            o_ref[:, h * HEAD_DIM:(h + 1) * HEAD_DIM] = scratch[0][h]
        lse_slab = scratch[1][...]

    @pl.when(hg == 0)
    def _():
        lse_ref[...] = lse_slab

    @pl.when(hg > 0)
    def _():
        lse_ref[...] += lse_slab


def _band_attention(qk, v, bsz, seq, branch, slopes):
    _, dilation = A_BRANCHES[branch]
    width = v.shape[-1]
    n_blk = seq // dilation // BAND_BLOCK
    hw = max(width * 4 // dilation, 512) if dilation > 1 else width
    hw = min(hw, width)
    n_hg = width // hw
    heads = hw // HEAD_DIM
    slope_tab = tuple(tuple(float(s) for s in slopes[g * heads:(g + 1) * heads]) for g in range(n_hg))
    rows = BAND_BLOCK * dilation
    blk = (None, dilation, BAND_BLOCK, hw)

    def prev(n):
        return jnp.maximum(n - 1, 0)

    scratch = []
    if dilation > 1:
        scratch = [pltpu.VMEM((heads, rows, HEAD_DIM), F32), pltpu.VMEM((rows, LANES), F32)]
    return pl.pallas_call(
        functools.partial(_band_attn_kernel, dilation=dilation, slopes=slope_tab),
        grid=(bsz, n_blk, n_hg),
        in_specs=[pl.BlockSpec(blk, lambda b, n, g: (b, 0, n, g)),
                  pl.BlockSpec(blk, lambda b, n, g: (b, 0, prev(n), n_hg + g)),
                  pl.BlockSpec(blk, lambda b, n, g: (b, 0, n, n_hg + g)),
                  pl.BlockSpec(blk, lambda b, n, g: (b, 0, prev(n), g)),
                  pl.BlockSpec(blk, lambda b, n, g: (b, 0, n, g))],
        out_specs=[pl.BlockSpec((rows, hw), lambda b, n, g: (b * n_blk + n, g)),
                   pl.BlockSpec((rows, LANES), lambda b, n, g: (b * n_blk + n, 0))],
        out_shape=[jax.ShapeDtypeStruct((bsz * seq, width), F32),
                   jax.ShapeDtypeStruct((bsz * seq, LANES), F32)],
        scratch_shapes=scratch,
        compiler_params=_params("parallel", "arbitrary", "arbitrary"),
        name=f"band_attn_d{dilation}",
    )(qk, qk, qk, v, v)


def _combine_kernel(o0_ref, o1_ref, o2_ref, l0_ref, l1_ref, l2_ref, out_ref):
    l0, l1, l2 = l0_ref[...], l1_ref[...], l2_ref[...]
    m = jnp.maximum(jnp.maximum(l0, l1), l2)
    e0, e1, e2 = jnp.exp(l0 - m), jnp.exp(l1 - m), jnp.exp(l2 - m)
    tot = e0 + e1 + e2
    w0, w1, w2 = e0 / tot, e1 / tot, e2 / tot
    for h in range(out_ref.shape[1] // HEAD_DIM):
        sl = slice(h * HEAD_DIM, (h + 1) * HEAD_DIM)
        acc = w0[:, h:h + 1] * o0_ref[:, sl] + w1[:, h:h + 1] * o1_ref[:, sl] + w2[:, h:h + 1] * o2_ref[:, sl]
        out_ref[:, sl] = acc.astype(out_ref.dtype)


def _combine_branches(outs, lses, tm=512):
    t, width = outs[0].shape
    o_spec = pl.BlockSpec((tm, width), lambda i: (i, 0))
    l_spec = pl.BlockSpec((tm, LANES), lambda i: (i, 0))
    return pl.pallas_call(
        _combine_kernel,
        grid=(t // tm,),
        in_specs=[o_spec] * 3 + [l_spec] * 3,
        out_specs=o_spec,
        out_shape=jax.ShapeDtypeStruct((t, width), BF16),
        compiler_params=_params("parallel"),
        name="combine_branches",
    )(*outs, *lses)


def _pool_kernel(u_ref, uh_ref, w_ref, sc_ref, o_ref, wb_ref, *, blocks_per_seq):
    i = pl.program_id(0)

    @pl.when(i == 0)
    def _():
        wb_ref[...] = w_ref[...].astype(BF16)

    tm = u_ref.shape[0]
    group = w_ref.shape[1]
    first = (i % blocks_per_seq) == 0
    row = lax.broadcasted_iota(jnp.int32, (tm, 1), 0)
    pos = (i % blocks_per_seq) * tm + row
    for g, win in enumerate(POOL_WINDOWS):
        sl = slice(g * group, (g + 1) * group)
        u = u_ref[:, sl]
        halo = jnp.where(first, 0.0, uh_ref[:, sl])
        s = jnp.concatenate([halo, u], axis=0)
        step = 1
        while step < win:
            s = s + pltpu.roll(s, step, axis=0)
            step *= 2
        count = jnp.minimum(pos + 1, win).astype(F32)
        pooled = s[POOL_HALO:] / count - u
        y = jnp.dot(pooled.astype(BF16), wb_ref[g], preferred_element_type=F32)
        o_ref[:, sl] = (y * sc_ref[:, sl]).astype(o_ref.dtype)


def _pool_mixer(u, pool_w, pool_scale, seq, tm=512):
    t, width = u.shape
    n_group, group, _ = pool_w.shape
    blocks_per_seq = seq // tm
    halo_blocks = tm // POOL_HALO
    return pl.pallas_call(
        functools.partial(_pool_kernel, blocks_per_seq=blocks_per_seq),
        grid=(t // tm,),
        in_specs=[pl.BlockSpec((tm, width), lambda i: (i, 0)),
                  pl.BlockSpec((POOL_HALO, width), lambda i: (jnp.maximum(i * halo_blocks - 1, 0), 0)),
                  pl.BlockSpec((n_group, group, group), lambda i: (0, 0, 0)),
                  pl.BlockSpec((1, width), lambda i: (0, 0))],
        out_specs=pl.BlockSpec((tm, width), lambda i: (i, 0)),
        out_shape=jax.ShapeDtypeStruct((t, width), BF16),
        scratch_shapes=[pltpu.VMEM((n_group, group, group), BF16)],
        compiler_params=_params("arbitrary"),
        name="pool_mixer",
    )(u, u, pool_w, pool_scale.reshape(1, width))


def _split3(x):
    hi = x.astype(BF16)
    r1 = x - hi.astype(F32)
    mid = r1.astype(BF16)
    lo = (r1 - mid.astype(F32)).astype(BF16)
    return hi, mid, lo


def _gate_kernel(a_ref, wr_ref, wa2_ref, ba_ref, o_ref):
    r = jnp.dot(a_ref[...], wr_ref[...].astype(BF16), preferred_element_type=F32)
    r_hi, r_mid, r_lo = _split3(r)
    w_hi, w_mid, w_lo = _split3(wa2_ref[...])
    g = (jnp.dot(r_hi, w_hi, preferred_element_type=F32)
         + (jnp.dot(r_hi, w_mid, preferred_element_type=F32) + jnp.dot(r_mid, w_hi, preferred_element_type=F32))
         + (jnp.dot(r_hi, w_lo, preferred_element_type=F32) + jnp.dot(r_mid, w_mid, preferred_element_type=F32)
            + jnp.dot(r_lo, w_hi, preferred_element_type=F32)))
    g = g + ba_ref[...]
    log_sig = jnp.minimum(g, 0.0) - jnp.log1p(jnp.exp(-jnp.abs(g)))
    o_ref[...] = log_sig / C_GATE_TAU


def _gla_log_decay(hn, w_r, w_a2, b_a, tm=512):
    t, d = hn.shape
    kw = w_a2.shape[1]
    return pl.pallas_call(
        _gate_kernel,
        grid=(t // tm,),
        in_specs=[pl.BlockSpec((tm, d), lambda i: (i, 0)),
                  pl.BlockSpec((d, LANES), lambda i: (0, 0)),
                  pl.BlockSpec((LANES, kw), lambda i: (0, 0)),
                  pl.BlockSpec((1, kw), lambda i: (0, 0))],
        out_specs=pl.BlockSpec((tm, kw), lambda i: (i, 0)),
        out_shape=jax.ShapeDtypeStruct((t, kw), F32),
        compiler_params=_params("parallel"),
        name="gla_log_decay",
    )(hn, w_r, w_a2, b_a.reshape(1, kw))


def _gla_kernel(q_ref, k_ref, v_ref, la_ref, gate_ref, og_ref, o_ref, state_ref, *, chunks_per_step):
    @pl.when(pl.program_id(2) == 0)
    def _():
        state_ref[...] = jnp.zeros_like(state_ref)

    c = C_CHUNK
    dk = q_ref.shape[1]
    ri = lax.broadcasted_iota(jnp.int32, (c, c), 0)
    ci = lax.broadcasted_iota(jnp.int32, (c, c), 1)
    causal = ci <= ri
    tri = causal.astype(BF16)
    contract_last = (((1,), (1,)), ((), ()))
    contract_first = (((0,), (0,)), ((), ()))
    for n in range(chunks_per_step):
        rows = slice(n * c, (n + 1) * c)
        la = la_ref[rows, :]
        la_hi, la_mid, la_lo = _split3(la)
        bc = (jnp.dot(tri, la_hi, preferred_element_type=F32)
              + jnp.dot(tri, la_mid, preferred_element_type=F32)
              + jnp.dot(tri, la_lo, preferred_element_type=F32))
        b_last = bc[c - 1:c, :]
        q_in = q_ref[rows, :] * (dk ** -0.5) * jnp.exp(bc)
        k = k_ref[rows, :]
        k_in = k * jnp.exp(-bc)
        k_st = k * jnp.exp(b_last - bc)
        v = v_ref[rows, :]
        q_b = q_in.astype(BF16)
        att = lax.dot_general(q_b, k_in.astype(BF16), contract_last, preferred_element_type=F32)
        att = jnp.where(causal, att, 0.0)
        o = jnp.dot(att.astype(BF16), v, preferred_element_type=F32)
        state = state_ref[...]
        o = o + lax.dot_general(q_b, state.astype(BF16), contract_last, preferred_element_type=F32)
        upd = lax.dot_general(v, k_st.astype(BF16), contract_first, preferred_element_type=F32)
        state_ref[...] = state * jnp.exp(b_last) + upd
        ms = jnp.mean(o * o, axis=-1, keepdims=True)
        gate = gate_ref[rows, :]
        y = o * lax.rsqrt(ms + EPS) * og_ref[...] * (gate * jax.nn.sigmoid(gate))
        o_ref[rows, :] = y.astype(o_ref.dtype)


def _gla(qk, v, gate, log_a, o_gain, bsz, seq, rows=256):
    t = bsz * seq
    dk = log_a.shape[1] // C_HEADS
    dv = v.shape[1] // C_HEADS
    steps = seq // rows

    def row_block(b, n):
        return b * steps + n

    return pl.pallas_call(
        functools.partial(_gla_kernel, chunks_per_step=rows // C_CHUNK),
        grid=(bsz, C_HEADS, steps),
        in_specs=[pl.BlockSpec((rows, dk), lambda b, h, n: (row_block(b, n), h)),
                  pl.BlockSpec((rows, dk), lambda b, h, n: (row_block(b, n), C_HEADS + h)),
                  pl.BlockSpec((rows, dv), lambda b, h, n: (row_block(b, n), h)),
                  pl.BlockSpec((rows, dk), lambda b, h, n: (row_block(b, n), h)),
                  pl.BlockSpec((rows, dv), lambda b, h, n: (row_block(b, n), h)),
                  pl.BlockSpec((1, dv), lambda b, h, n: (0, 0))],
        out_specs=pl.BlockSpec((rows, dv), lambda b, h, n: (row_block(b, n), h)),
        out_shape=jax.ShapeDtypeStruct((t, v.shape[1]), BF16),
        scratch_shapes=[pltpu.VMEM((dv, dk), F32)],
        compiler_params=_params("parallel", "parallel", "arbitrary"),
        name="gla",
    )(qk, qk, v, log_a, gate, o_gain.reshape(1, dv))


def _xattn_kernel(h_ref, g_ref, wq_ref, qg_ref, k_ref, v_ref, wo_ref, o_ref):
    x = h_ref[...]
    ms = jnp.mean(x * x, axis=-1, keepdims=True)
    hn = (x * lax.rsqrt(ms + EPS) * g_ref[...]).astype(BF16)
    q = jnp.dot(hn, wq_ref[...], preferred_element_type=F32)
    scale = HEAD_DIM ** -0.5
    contract_last = (((1,), (1,)), ((), ()))
    heads = []
    for hd in range(X_HEADS):
        sl = slice(hd * HEAD_DIM, (hd + 1) * HEAD_DIM)
        qh = q[:, sl]
        qms = jnp.mean(qh * qh, axis=-1, keepdims=True)
        qh = (qh * lax.rsqrt(qms + EPS) * qg_ref[...]).astype(BF16)
        s = lax.dot_general(qh, k_ref[0, :, sl], contract_last, preferred_element_type=F32) * scale
        m = jnp.max(s, axis=-1, keepdims=True)
        p = jnp.exp(s - m)
        p = p / jnp.sum(p, axis=-1, keepdims=True)
        heads.append(jnp.dot(p.astype(BF16), v_ref[0, :, sl], preferred_element_type=F32))
    o = jnp.concatenate(heads, axis=-1).astype(BF16)
    o_ref[...] = x + jnp.dot(o, wo_ref[...], preferred_element_type=F32)


def _cross_attention(h, norm_gain, wq, q_gain, k, v, wo, seq, tm=256):
    t, d = h.shape
    xw = wq.shape[1]
    mlen = k.shape[1]
    tiles_per_seq = seq // tm
    return pl.pallas_call(
        _xattn_kernel,
        grid=(t // tm,),
        in_specs=[pl.BlockSpec((tm, d), lambda i: (i, 0)),
                  pl.BlockSpec((1, d), lambda i: (0, 0)),
                  pl.BlockSpec((d, xw), lambda i: (0, 0)),
                  pl.BlockSpec((1, HEAD_DIM), lambda i: (0, 0)),
                  pl.BlockSpec((1, mlen, xw), lambda i: (i // tiles_per_seq, 0, 0)),
                  pl.BlockSpec((1, mlen, xw), lambda i: (i // tiles_per_seq, 0, 0)),
                  pl.BlockSpec((xw, d), lambda i: (0, 0))],
        out_specs=pl.BlockSpec((tm, d), lambda i: (i, 0)),
        out_shape=jax.ShapeDtypeStruct((t, d), F32),
        compiler_params=_params("parallel"),
        name="cross_attention",
    )(h, norm_gain.reshape(1, d), wq, q_gain.reshape(1, HEAD_DIM), k, v, wo)


def _ffn_up_kernel(a_ref, ah_ref, wg_ref, wv_ref, cwg_ref, cwv_ref, cbg_ref, cbv_ref, o_ref, *, blocks_per_seq):
    tm = a_ref.shape[0]
    first = (pl.program_id(0) % blocks_per_seq) == 0
    row = lax.broadcasted_iota(jnp.int32, (tm, 1), 0)
    a = a_ref[...]
    ah = ah_ref[...]

    def conv_half(w_ref, cw_ref, cb_ref):
        w = w_ref[...].astype(BF16)
        u = jnp.dot(a, w, preferred_element_type=F32)
        uh = jnp.where(first, 0.0, jnp.dot(ah, w, preferred_element_type=F32))
        u1 = jnp.where(row == 0, uh[CONV_HALO - 1:CONV_HALO], pltpu.roll(u, 1, axis=0))
        u2 = jnp.where(row == 0, uh[CONV_HALO - 2:CONV_HALO - 1],
                       jnp.where(row == 1, uh[CONV_HALO - 1:CONV_HALO], pltpu.roll(u, 2, axis=0)))
        cw = cw_ref[...]
        return cb_ref[...] + cw[0:1] * u2 + cw[1:2] * u1 + cw[2:3] * u

    cg = conv_half(wg_ref, cwg_ref, cbg_ref)
    cv = conv_half(wv_ref, cwv_ref, cbv_ref)
    o_ref[...] = (cg * jax.nn.sigmoid(cg) * cv).astype(o_ref.dtype)


def _ffn_up(hn, w_up, conv_w, conv_b, layer, seq, tm=1024, tn=256):
    t, d = hn.shape
    d_ff = w_up.shape[2] // 2
    n_tiles = d_ff // tn
    blocks_per_seq = seq // tm
    halo_blocks = tm // CONV_HALO
    cw = conv_w[layer]
    cb = conv_b[layer].reshape(1, 2 * d_ff)
    return pl.pallas_call(
        functools.partial(_ffn_up_kernel, blocks_per_seq=blocks_per_seq),
        grid=(t // tm, n_tiles),
        in_specs=[pl.BlockSpec((tm, d), lambda i, j: (i, 0)),
                  pl.BlockSpec((CONV_HALO, d), lambda i, j: (jnp.maximum(i * halo_blocks - 1, 0), 0)),
                  pl.BlockSpec((None, d, tn), lambda i, j: (layer, 0, j)),
                  pl.BlockSpec((None, d, tn), lambda i, j: (layer, 0, n_tiles + j)),
                  pl.BlockSpec((CONV_WIDTH, tn), lambda i, j: (0, j)),
                  pl.BlockSpec((CONV_WIDTH, tn), lambda i, j: (0, n_tiles + j)),
                  pl.BlockSpec((1, tn), lambda i, j: (0, j)),
                  pl.BlockSpec((1, tn), lambda i, j: (0, n_tiles + j))],
        out_specs=pl.BlockSpec((tm, tn), lambda i, j: (i, j)),
        out_shape=jax.ShapeDtypeStruct((t, d_ff), BF16),
        compiler_params=_params("parallel", "arbitrary"),
        name="ffn_up",
    )(hn, hn, w_up, w_up, cw, cw, cb, cb)


def _dilated_pool_layer(h, hn, w_in, q_gain, k_gain, pool_w, pool_scale, w_out, j, bsz, seq):
    d = h.shape[1]
    a_width = d // 2
    heads = a_width // HEAD_DIM
    tn = 512
    q_blocks = a_width // tn
    k_base = N_BRANCH * q_blocks
    v_base = 2 * k_base
    gains = jnp.concatenate([jnp.tile(q_gain, heads), jnp.tile(k_gain, heads)]).reshape(1, 2 * a_width)
    dils = tuple(dil for _, dil in A_BRANCHES)
    v_list = _mm([hn], w_in, j, lambda c: v_base + c, a_width, BF16, dils=dils, seq=seq, tn=tn, name="ab_in_v")
    u = _mm([hn], w_in, j, lambda c: v_base + q_blocks + c, d - a_width, F32, tn=tn, name="ab_in_u")
    slopes = _alibi_slopes()
    outs, lses = [], []
    for g, dil in enumerate(dils):
        def wcol(c, g=g):
            return jnp.where(c < q_blocks, g * q_blocks + c, k_base + g * q_blocks + c - q_blocks)
        qk = _mm([hn], w_in, j, wcol, 2 * a_width, BF16, mode="headnorm", extra=gains, dils=(dil,), seq=seq,
                 tn=tn, name=f"ab_in_qk{g}")
        if dil == 1:
            qk = qk.reshape(bsz, 1, seq, 2 * a_width)
            v_g = v_list[g].reshape(bsz, 1, seq, a_width)
        else:
            v_g = v_list[g]
        o, lse = _band_attention(qk, v_g, bsz, seq, g, slopes[g])
        outs.append(o)
        lses.append(lse)
    a_out = _combine_branches(outs, lses)
    b_out = _pool_mixer(u, pool_w, pool_scale, seq)
    return _mm([a_out, b_out], w_out, j, lambda c: c, d, F32, mode="residual", extra=h, name="ab_out")


def _gla_layer(h, hn, w_in, w_a2, b_a, o_gain, w_out, j, bsz, seq):
    d = h.shape[1]
    kw = w_a2.shape[2]
    vw = w_out.shape[1]
    tn = 512
    qk = _mm([hn], w_in, j, lambda c: c, 2 * kw, F32, tn=tn, name="c_in_qk")
    v = _mm([hn], w_in, j, lambda c: 2 * kw // tn + c, vw, BF16, tn=tn, name="c_in_v")
    gate = _mm([hn], w_in, j, lambda c: (2 * kw + vw) // tn + c, vw, F32, tn=tn, name="c_in_gate")
    rank = w_a2.shape[1]
    w_r = jnp.pad(w_in[j, :, 2 * kw + 2 * vw:], ((0, 0), (0, LANES - rank)))
    w_a2p = jnp.pad(w_a2[j], ((0, LANES - rank), (0, 0)))
    log_a = _gla_log_decay(hn, w_r, w_a2p, b_a[j])
    o = _gla(qk, v, gate, log_a, o_gain[j], bsz, seq)
    return _mm([o], w_out, j, lambda c: c, d, F32, mode="residual", extra=h, name="c_out")


def _memory_kv(mem, gain, wkv, k_gain, layer, bsz):
    xw = wkv.shape[2] // 2
    tn = 256
    mem_n = _rmsnorm(mem, gain)
    rows = mem.shape[0]
    gains = jnp.tile(k_gain, xw // HEAD_DIM).reshape(1, xw)
    k = _mm([mem_n], wkv, layer, lambda c: c, xw, BF16, mode="headnorm", extra=gains, tm=rows, tn=tn, name="mem_k")
    v = _mm([mem_n], wkv, layer, lambda c: xw // tn + c, xw, BF16, tm=rows, tn=tn, name="mem_v")
    return k.reshape(bsz, rows // bsz, xw), v.reshape(bsz, rows // bsz, xw)


def kernel(x, mem, mix_norm, ab_w_in, ab_q_norm, ab_k_norm, ab_pool_w, ab_pool_scale, ab_w_out, c_w_in, c_w_a2, c_b_a, c_o_norm, c_w_out, x_norm, x_mem_norm, x_wq, x_wkv, x_q_norm, x_k_norm, x_wo, f_norm, f_w_up, f_conv_w, f_conv_b, f_w_down):
    bsz, seq, d = x.shape
    depth = mix_norm.shape[0]
    h = x.reshape(bsz * seq, d)
    mem2 = mem.reshape(bsz * mem.shape[1], d)
    for layer in range(depth):
        j = layer // 2
        hn = _rmsnorm(h, mix_norm[layer])
        if layer % 2 == 0:
            h = _dilated_pool_layer(h, hn, ab_w_in, ab_q_norm[j], ab_k_norm[j], ab_pool_w[j], ab_pool_scale[j],
                                    ab_w_out, j, bsz, seq)
        else:
            h = _gla_layer(h, hn, c_w_in, c_w_a2, c_b_a, c_o_norm, c_w_out, j, bsz, seq)
        k, v = _memory_kv(mem2, x_mem_norm[layer], x_wkv, x_k_norm[layer], layer, bsz)
        h = _cross_attention(h, x_norm[layer], x_wq[layer].astype(BF16), x_q_norm[layer], k, v,
                             x_wo[layer].astype(BF16), seq)
        hn = _rmsnorm(h, f_norm[layer])
        act = _ffn_up(hn, f_w_up, f_conv_w, f_conv_b, layer, seq)
        h = _mm([act], f_w_down[layer].astype(BF16), None, lambda c: c, d, F32, mode="residual", extra=h,
                tm=512, name="ffn_down")
    return h.reshape(bsz, seq, d)
```

```python
import functools

import numpy as np
import jax
import jax.numpy as jnp
from jax import lax
from jax.experimental import pallas as pl
from jax.experimental.pallas import tpu as pltpu

F32 = jnp.float32
BF16 = jnp.bfloat16

LANES = 128
SUBLANES = 8
VMEM_LIMIT_BYTES = 56 * 2 ** 20

EPS = 1e-6
HEAD_DIM = 128
A_BRANCHES = ((128, 1), (512, 4), (2048, 16))
N_BRANCH = len(A_BRANCHES)
BAND_BLOCK = 128
POOL_WINDOWS = (2, 4, 8, 16)
POOL_HALO = 16
C_HEADS = 8
C_GATE_RANK = 16
C_GATE_TAU = 16.0
C_CHUNK = 64
X_HEADS = 4
CONV_WIDTH = 3
CONV_HALO = SUBLANES


def _params(*semantics):
    return pltpu.CompilerParams(dimension_semantics=semantics, vmem_limit_bytes=VMEM_LIMIT_BYTES)


def _lane_groups(width):
    return [slice(c * LANES, (c + 1) * LANES) for c in range(width // LANES)]


def _rmsnorm_kernel(x_ref, g_ref, o_ref):
    x = x_ref[...].astype(F32)
    ms = jnp.mean(x * x, axis=-1, keepdims=True)
    o_ref[...] = (x * lax.rsqrt(ms + EPS) * g_ref[...]).astype(o_ref.dtype)


def _rmsnorm(x, gain, tm=256):
    t, d = x.shape
    return pl.pallas_call(
        _rmsnorm_kernel,
        grid=(t // tm,),
        in_specs=[pl.BlockSpec((tm, d), lambda i: (i, 0)),
                  pl.BlockSpec((1, d), lambda i: (0, 0))],
        out_specs=pl.BlockSpec((tm, d), lambda i: (i, 0)),
        out_shape=jax.ShapeDtypeStruct((t, d), BF16),
        compiler_params=_params("parallel"),
        name="rmsnorm",
    )(x, gain.reshape(1, d))


def _mm_kernel(*refs, n_a, mode, dils, w_t):
    a_refs = refs[:n_a]
    w_ref = refs[n_a]
    n_extra = 1 if mode in ("headnorm", "residual") else 0
    extra = refs[n_a + 1:n_a + 1 + n_extra]
    o_refs = refs[n_a + 1 + n_extra:n_a + 1 + n_extra + len(dils)]
    scratch = refs[n_a + 1 + n_extra + len(dils):]
    acc = None
    k0 = 0
    for a_ref in a_refs:
        kk = a_ref.shape[1]
        if w_t:
            part = lax.dot_general(a_ref[...], w_ref[:, k0:k0 + kk].astype(BF16), (((1,), (1,)), ((), ())),
                                   preferred_element_type=F32)
        else:
            part = jnp.dot(a_ref[...], w_ref[k0:k0 + kk, :].astype(BF16), preferred_element_type=F32)
        acc = part if acc is None else acc + part
        k0 += kk
    tm, tn = acc.shape
    for c, sl in enumerate(_lane_groups(tn)):
        blk = acc[:, sl]
        if mode == "headnorm":
            ms = jnp.mean(blk * blk, axis=-1, keepdims=True)
            blk = blk * lax.rsqrt(ms + EPS) * extra[0][:, sl]
        elif mode == "residual":
            blk = extra[0][:, sl] + blk
        if scratch:
            scratch[0][c] = blk
        for o_ref, dil in zip(o_refs, dils):
            if dil == 1:
                o_ref[:, sl] = blk.astype(o_ref.dtype)
    for o_ref, dil in zip(o_refs, dils):
        if dil > 1:
            for r in range(dil):
                for c, sl in enumerate(_lane_groups(tn)):
                    o_ref[r, :, sl] = scratch[0][c, pl.ds(r, tm // dil, stride=dil), :].astype(o_ref.dtype)


def _mm(a_list, w, layer, wcol, ncols, out_dtype, *, mode="plain", extra=None, dils=(1,), seq=None,
        w_t=False, tm=1024, tn=512, name="mm"):
    t = a_list[0].shape[0]
    k_total = sum(a.shape[1] for a in a_list)
    assert w.shape[-1 if w_t else -2] == k_total and ncols % tn == 0 and t % tm == 0
    in_specs = [pl.BlockSpec((tm, a.shape[1]), lambda i, j: (i, 0)) for a in a_list]
    if w_t:
        in_specs.append(pl.BlockSpec((None, tn, k_total), lambda i, j: (layer, wcol(j), 0)))
    elif w.ndim == 3:
        in_specs.append(pl.BlockSpec((None, k_total, tn), lambda i, j: (layer, 0, wcol(j))))
    else:
        in_specs.append(pl.BlockSpec((k_total, tn), lambda i, j: (0, wcol(j))))
    args = list(a_list) + [w]
    if mode == "headnorm":
        in_specs.append(pl.BlockSpec((1, tn), lambda i, j: (0, j)))
        args.append(extra)
    elif mode == "residual":
        in_specs.append(pl.BlockSpec((tm, tn), lambda i, j: (i, j)))
        args.append(extra)
    out_specs, out_shapes = [], []
    for dil in dils:
        if dil == 1:
            out_specs.append(pl.BlockSpec((tm, tn), lambda i, j: (i, j)))
            out_shapes.append(jax.ShapeDtypeStruct((t, ncols), out_dtype))
        else:
            tiles = seq // tm
            assert seq % tm == 0 and tm % dil == 0
            out_specs.append(pl.BlockSpec((None, dil, tm // dil, tn), lambda i, j: (i // tiles, 0, i % tiles, j)))
            out_shapes.append(jax.ShapeDtypeStruct((t // seq, dil, seq // dil, ncols), out_dtype))
    scratch = [pltpu.VMEM((tn // LANES, tm, LANES), F32)] if any(dil > 1 for dil in dils) else []
    outs = pl.pallas_call(
        functools.partial(_mm_kernel, n_a=len(a_list), mode=mode, dils=tuple(dils), w_t=w_t),
        grid=(t // tm, ncols // tn),
        in_specs=in_specs,
        out_specs=out_specs,
        out_shape=out_shapes,
        scratch_shapes=scratch,
        compiler_params=_params("parallel", "arbitrary"),
        name=name,
    )(*args)
    return outs[0] if len(dils) == 1 else outs


def _alibi_slopes():
    n = N_BRANCH * (2048 // HEAD_DIM)
    s = np.power(np.float32(2.0), -8.0 * np.arange(1, n + 1, dtype=np.float32) / np.float32(n)).astype(np.float32)
    return s.reshape(N_BRANCH, -1)


def _band_attn_kernel(q_ref, kp_ref, kc_ref, vp_ref, vc_ref, o_ref, lse_ref, *scratch, dilation, slopes):
    blk = BAND_BLOCK
    hg = pl.program_id(2)
    has_prev = pl.program_id(1) > 0
    n_heads = len(slopes[0])
    qi = lax.broadcasted_iota(jnp.int32, (blk, 2 * blk), 0)
    ki = lax.broadcasted_iota(jnp.int32, (blk, 2 * blk), 1)
    rel = qi + blk - ki
    valid = jnp.logical_and(jnp.logical_and(rel >= 0, rel <= blk), jnp.logical_or(ki >= blk, has_prev))
    dist = (rel * dilation).astype(F32)
    lane = lax.broadcasted_iota(jnp.int32, (blk, LANES), 1)
    scale = HEAD_DIM ** -0.5
    contract_last = (((1,), (1,)), ((), ()))
    for r in range(dilation):
        lse_tile = jnp.zeros((blk, LANES), F32)
        for h in range(n_heads):
            sl = slice(h * HEAD_DIM, (h + 1) * HEAD_DIM)
            k = jnp.concatenate([kp_ref[r, :, sl], kc_ref[r, :, sl]], axis=0)
            v = jnp.concatenate([vp_ref[r, :, sl], vc_ref[r, :, sl]], axis=0)
            s = lax.dot_general(q_ref[r, :, sl], k, contract_last, preferred_element_type=F32) * scale
            slope = slopes[0][h]
            for g in range(1, len(slopes)):
                slope = jnp.where(hg == g, slopes[g][h], slope)
            s = jnp.where(valid, s - slope * dist, -jnp.inf)
            m = jnp.max(s, axis=-1, keepdims=True)
            p = jnp.exp(s - m)
            den = jnp.sum(p, axis=-1, keepdims=True)
            o = jnp.dot(p.astype(BF16), v, preferred_element_type=F32) / den
            lse_h = m + jnp.log(den)
            for g in range(len(slopes)):
                lse_tile = jnp.where(jnp.logical_and(lane == g * n_heads + h, hg == g), lse_h, lse_tile)
            if dilation == 1:
                o_ref[:, sl] = o
            else:
                scratch[0][h, pl.ds(r, blk, stride=dilation), :] = o
        if dilation == 1:
            lse_slab = lse_tile
        else:
            scratch[1][pl.ds(r, blk, stride=dilation), :] = lse_tile
    if dilation > 1:
        for h in range(n_heads):
            o_ref[:, h * HEAD_DIM:(h + 1) * HEAD_DIM] = scratch[0][h]
        lse_slab = scratch[1][...]

    @pl.when(hg == 0)
    def _():
        lse_ref[...] = lse_slab

    @pl.when(hg > 0)
    def _():
        lse_ref[...] += lse_slab


def _band_attention(qk, v, bsz, seq, branch, slopes):
    _, dilation = A_BRANCHES[branch]
    width = v.shape[-1]
    n_blk = seq // dilation // BAND_BLOCK
    hw = max(width * 4 // dilation, 512) if dilation > 1 else width
    hw = min(hw, width)
    n_hg = width // hw
    heads = hw // HEAD_DIM
    slope_tab = tuple(tuple(float(s) for s in slopes[g * heads:(g + 1) * heads]) for g in range(n_hg))
    rows = BAND_BLOCK * dilation
    blk = (None, dilation, BAND_BLOCK, hw)

    def prev(n):
        return jnp.maximum(n - 1, 0)

    scratch = []
    if dilation > 1:
        scratch = [pltpu.VMEM((heads, rows, HEAD_DIM), F32), pltpu.VMEM((rows, LANES), F32)]
    return pl.pallas_call(
        functools.partial(_band_attn_kernel, dilation=dilation, slopes=slope_tab),
        grid=(bsz, n_blk, n_hg),
        in_specs=[pl.BlockSpec(blk, lambda b, n, g: (b, 0, n, g)),
                  pl.BlockSpec(blk, lambda b, n, g: (b, 0, prev(n), n_hg + g)),
                  pl.BlockSpec(blk, lambda b, n, g: (b, 0, n, n_hg + g)),
                  pl.BlockSpec(blk, lambda b, n, g: (b, 0, prev(n), g)),
                  pl.BlockSpec(blk, lambda b, n, g: (b, 0, n, g))],
        out_specs=[pl.BlockSpec((rows, hw), lambda b, n, g: (b * n_blk + n, g)),
                   pl.BlockSpec((rows, LANES), lambda b, n, g: (b * n_blk + n, 0))],
        out_shape=[jax.ShapeDtypeStruct((bsz * seq, width), F32),
                   jax.ShapeDtypeStruct((bsz * seq, LANES), F32)],
        scratch_shapes=scratch,
        compiler_params=_params("parallel", "arbitrary", "arbitrary"),
        name=f"band_attn_d{dilation}",
    )(qk, qk, qk, v, v)


def _combine_kernel(o0_ref, o1_ref, o2_ref, l0_ref, l1_ref, l2_ref, out_ref):
    l0, l1, l2 = l0_ref[...], l1_ref[...], l2_ref[...]
    m = jnp.maximum(jnp.maximum(l0, l1), l2)
    e0, e1, e2 = jnp.exp(l0 - m), jnp.exp(l1 - m), jnp.exp(l2 - m)
    tot = e0 + e1 + e2
    w0, w1, w2 = e0 / tot, e1 / tot, e2 / tot
    for h in range(out_ref.shape[1] // HEAD_DIM):
        sl = slice(h * HEAD_DIM, (h + 1) * HEAD_DIM)
        acc = w0[:, h:h + 1] * o0_ref[:, sl] + w1[:, h:h + 1] * o1_ref[:, sl] + w2[:, h:h + 1] * o2_ref[:, sl]
        out_ref[:, sl] = acc.astype(out_ref.dtype)


def _combine_branches(outs, lses, tm=512):
    t, width = outs[0].shape
    o_spec = pl.BlockSpec((tm, width), lambda i: (i, 0))
    l_spec = pl.BlockSpec((tm, LANES), lambda i: (i, 0))
    return pl.pallas_call(
        _combine_kernel,
        grid=(t // tm,),
        in_specs=[o_spec] * 3 + [l_spec] * 3,
        out_specs=o_spec,
        out_shape=jax.ShapeDtypeStruct((t, width), BF16),
        compiler_params=_params("parallel"),
        name="combine_branches",
    )(*outs, *lses)


def _pool_kernel(u_ref, uh_ref, w_ref, sc_ref, o_ref, wb_ref, *, blocks_per_seq):
    i = pl.program_id(0)

    @pl.when(i == 0)
    def _():
        wb_ref[...] = w_ref[...].astype(BF16)

    tm = u_ref.shape[0]
    group = w_ref.shape[1]
    first = (i % blocks_per_seq) == 0
    row = lax.broadcasted_iota(jnp.int32, (tm, 1), 0)
    pos = (i % blocks_per_seq) * tm + row
    for g, win in enumerate(POOL_WINDOWS):
        sl = slice(g * group, (g + 1) * group)
        u = u_ref[:, sl]
        halo = jnp.where(first, 0.0, uh_ref[:, sl])
        s = jnp.concatenate([halo, u], axis=0)
        step = 1
        while step < win:
            s = s + pltpu.roll(s, step, axis=0)
            step *= 2
        count = jnp.minimum(pos + 1, win).astype(F32)
        pooled = s[POOL_HALO:] / count - u
        y = jnp.dot(pooled.astype(BF16), wb_ref[g], preferred_element_type=F32)
        o_ref[:, sl] = (y * sc_ref[:, sl]).astype(o_ref.dtype)


def _pool_mixer(u, pool_w, pool_scale, seq, tm=512):
    t, width = u.shape
    n_group, group, _ = pool_w.shape
    blocks_per_seq = seq // tm
    halo_blocks = tm // POOL_HALO
    return pl.pallas_call(
        functools.partial(_pool_kernel, blocks_per_seq=blocks_per_seq),
        grid=(t // tm,),
        in_specs=[pl.BlockSpec((tm, width), lambda i: (i, 0)),
                  pl.BlockSpec((POOL_HALO, width), lambda i: (jnp.maximum(i * halo_blocks - 1, 0), 0)),
                  pl.BlockSpec((n_group, group, group), lambda i: (0, 0, 0)),
                  pl.BlockSpec((1, width), lambda i: (0, 0))],
        out_specs=pl.BlockSpec((tm, width), lambda i: (i, 0)),
        out_shape=jax.ShapeDtypeStruct((t, width), BF16),
        scratch_shapes=[pltpu.VMEM((n_group, group, group), BF16)],
        compiler_params=_params("arbitrary"),
        name="pool_mixer",
    )(u, u, pool_w, pool_scale.reshape(1, width))


def _split3(x):
    hi = x.astype(BF16)
    r1 = x - hi.astype(F32)
    mid = r1.astype(BF16)
    lo = (r1 - mid.astype(F32)).astype(BF16)
    return hi, mid, lo


def _gate_kernel(a_ref, wr_ref, wa2_ref, ba_ref, o_ref):
    r = lax.dot_general(a_ref[...], wr_ref[...].astype(BF16), (((1,), (1,)), ((), ())),
                        preferred_element_type=F32)
    r_hi, r_mid, r_lo = _split3(r)
    w_hi, w_mid, w_lo = _split3(wa2_ref[...])
    g = (jnp.dot(r_hi, w_hi, preferred_element_type=F32)
         + (jnp.dot(r_hi, w_mid, preferred_element_type=F32) + jnp.dot(r_mid, w_hi, preferred_element_type=F32))
         + (jnp.dot(r_hi, w_lo, preferred_element_type=F32) + jnp.dot(r_mid, w_mid, preferred_element_type=F32)
            + jnp.dot(r_lo, w_hi, preferred_element_type=F32)))
    g = g + ba_ref[...]
    log_sig = jnp.minimum(g, 0.0) - jnp.log1p(jnp.exp(-jnp.abs(g)))
    o_ref[...] = log_sig / C_GATE_TAU


def _gla_log_decay(hn, w_r, w_a2, b_a, tm=512):
    t, d = hn.shape
    kw = w_a2.shape[1]
    return pl.pallas_call(
        _gate_kernel,
        grid=(t // tm,),
        in_specs=[pl.BlockSpec((tm, d), lambda i: (i, 0)),
                  pl.BlockSpec((LANES, d), lambda i: (0, 0)),
                  pl.BlockSpec((LANES, kw), lambda i: (0, 0)),
                  pl.BlockSpec((1, kw), lambda i: (0, 0))],
        out_specs=pl.BlockSpec((tm, kw), lambda i: (i, 0)),
        out_shape=jax.ShapeDtypeStruct((t, kw), F32),
        compiler_params=_params("parallel"),
        name="gla_log_decay",
    )(hn, w_r, w_a2, b_a.reshape(1, kw))


def _gla_kernel(q_ref, k_ref, v_ref, la_ref, gate_ref, og_ref, o_ref, state_ref, *, chunks_per_step):
    @pl.when(pl.program_id(2) == 0)
    def _():
        state_ref[...] = jnp.zeros_like(state_ref)

    c = C_CHUNK
    dk = q_ref.shape[1]
    ri = lax.broadcasted_iota(jnp.int32, (c, c), 0)
    ci = lax.broadcasted_iota(jnp.int32, (c, c), 1)
    causal = ci <= ri
    tri = causal.astype(BF16)
    contract_last = (((1,), (1,)), ((), ()))
    contract_first = (((0,), (0,)), ((), ()))
    for n in range(chunks_per_step):
        rows = slice(n * c, (n + 1) * c)
        la = la_ref[rows, :]
        la_hi, la_mid, la_lo = _split3(la)
        bc = (jnp.dot(tri, la_hi, preferred_element_type=F32)
              + jnp.dot(tri, la_mid, preferred_element_type=F32)
              + jnp.dot(tri, la_lo, preferred_element_type=F32))
        b_last = bc[c - 1:c, :]
        q_in = q_ref[rows, :] * (dk ** -0.5) * jnp.exp(bc)
        k = k_ref[rows, :]
        k_in = k * jnp.exp(-bc)
        k_st = k * jnp.exp(b_last - bc)
        v = v_ref[rows, :]
        q_b = q_in.astype(BF16)
        att = lax.dot_general(q_b, k_in.astype(BF16), contract_last, preferred_element_type=F32)
        att = jnp.where(causal, att, 0.0)
        o = jnp.dot(att.astype(BF16), v, preferred_element_type=F32)
        state = state_ref[...]
        o = o + lax.dot_general(q_b, state.astype(BF16), contract_last, preferred_element_type=F32)
        upd = lax.dot_general(v, k_st.astype(BF16), contract_first, preferred_element_type=F32)
        state_ref[...] = state * jnp.exp(b_last) + upd
        ms = jnp.mean(o * o, axis=-1, keepdims=True)
        gate = gate_ref[rows, :]
        y = o * lax.rsqrt(ms + EPS) * og_ref[...] * (gate * jax.nn.sigmoid(gate))
        o_ref[rows, :] = y.astype(o_ref.dtype)


def _gla(qk, v, gate, log_a, o_gain, bsz, seq, rows=256):
    t = bsz * seq
    dk = log_a.shape[1] // C_HEADS
    dv = v.shape[1] // C_HEADS
    steps = seq // rows

    def row_block(b, n):
        return b * steps + n

    return pl.pallas_call(
        functools.partial(_gla_kernel, chunks_per_step=rows // C_CHUNK),
        grid=(bsz, C_HEADS, steps),
        in_specs=[pl.BlockSpec((rows, dk), lambda b, h, n: (row_block(b, n), h)),
                  pl.BlockSpec((rows, dk), lambda b, h, n: (row_block(b, n), C_HEADS + h)),
                  pl.BlockSpec((rows, dv), lambda b, h, n: (row_block(b, n), h)),
                  pl.BlockSpec((rows, dk), lambda b, h, n: (row_block(b, n), h)),
                  pl.BlockSpec((rows, dv), lambda b, h, n: (row_block(b, n), h)),
                  pl.BlockSpec((1, dv), lambda b, h, n: (0, 0))],
        out_specs=pl.BlockSpec((rows, dv), lambda b, h, n: (row_block(b, n), h)),
        out_shape=jax.ShapeDtypeStruct((t, v.shape[1]), BF16),
        scratch_shapes=[pltpu.VMEM((dv, dk), F32)],
        compiler_params=_params("parallel", "parallel", "arbitrary"),
        name="gla",
    )(qk, qk, v, log_a, gate, o_gain.reshape(1, dv))


def _xattn_kernel(h_ref, g_ref, wq_ref, qg_ref, k_ref, v_ref, wo_ref, fg_ref, o_ref, on_ref):
    x = h_ref[...]
    ms = jnp.mean(x * x, axis=-1, keepdims=True)
    hn = (x * lax.rsqrt(ms + EPS) * g_ref[...]).astype(BF16)
    q = jnp.dot(hn, wq_ref[...], preferred_element_type=F32)
    scale = HEAD_DIM ** -0.5
    contract_last = (((1,), (1,)), ((), ()))
    heads = []
    for hd in range(X_HEADS):
        sl = slice(hd * HEAD_DIM, (hd + 1) * HEAD_DIM)
        qh = q[:, sl]
        qms = jnp.mean(qh * qh, axis=-1, keepdims=True)
        qh = (qh * lax.rsqrt(qms + EPS) * qg_ref[...]).astype(BF16)
        s = lax.dot_general(qh, k_ref[0, :, sl], contract_last, preferred_element_type=F32) * scale
        m = jnp.max(s, axis=-1, keepdims=True)
        p = jnp.exp(s - m)
        p = p / jnp.sum(p, axis=-1, keepdims=True)
        heads.append(jnp.dot(p.astype(BF16), v_ref[0, :, sl], preferred_element_type=F32))
    o = jnp.concatenate(heads, axis=-1).astype(BF16)
    y = x + jnp.dot(o, wo_ref[...], preferred_element_type=F32)
    o_ref[...] = y
    yms = jnp.mean(y * y, axis=-1, keepdims=True)
    on_ref[...] = (y * lax.rsqrt(yms + EPS) * fg_ref[...]).astype(on_ref.dtype)


def _cross_attention(h, norm_gain, wq, q_gain, k, v, wo, next_gain, seq, tm=256):
    t, d = h.shape
    xw = wq.shape[1]
    mlen = k.shape[1]
    tiles_per_seq = seq // tm
    return pl.pallas_call(
        _xattn_kernel,
        grid=(t // tm,),
        in_specs=[pl.BlockSpec((tm, d), lambda i: (i, 0)),
                  pl.BlockSpec((1, d), lambda i: (0, 0)),
                  pl.BlockSpec((d, xw), lambda i: (0, 0)),
                  pl.BlockSpec((1, HEAD_DIM), lambda i: (0, 0)),
                  pl.BlockSpec((1, mlen, xw), lambda i: (i // tiles_per_seq, 0, 0)),
                  pl.BlockSpec((1, mlen, xw), lambda i: (i // tiles_per_seq, 0, 0)),
                  pl.BlockSpec((xw, d), lambda i: (0, 0)),
                  pl.BlockSpec((1, d), lambda i: (0, 0))],
        out_specs=[pl.BlockSpec((tm, d), lambda i: (i, 0)),
                   pl.BlockSpec((tm, d), lambda i: (i, 0))],
        out_shape=[jax.ShapeDtypeStruct((t, d), F32),
                   jax.ShapeDtypeStruct((t, d), BF16)],
        compiler_params=_params("parallel"),
        name="cross_attention",
    )(h, norm_gain.reshape(1, d), wq, q_gain.reshape(1, HEAD_DIM), k, v, wo, next_gain.reshape(1, d))


def _ffn_up_kernel(a_ref, ah_ref, wg_ref, wv_ref, cwg_ref, cwv_ref, cbg_ref, cbv_ref, wd_ref, o_ref, wdo_ref, *,
                   blocks_per_seq):
    wdo_ref[...] = wd_ref[...].astype(wdo_ref.dtype)
    tm = a_ref.shape[0]
    first = (pl.program_id(0) % blocks_per_seq) == 0
    row = lax.broadcasted_iota(jnp.int32, (tm, 1), 0)
    a = a_ref[...]
    ah = ah_ref[...]

    def conv_half(w_ref, cw_ref, cb_ref):
        w = w_ref[...].astype(BF16)
        u = jnp.dot(a, w, preferred_element_type=F32)
        uh = jnp.where(first, 0.0, jnp.dot(ah, w, preferred_element_type=F32))
        u1 = jnp.where(row == 0, uh[CONV_HALO - 1:CONV_HALO], pltpu.roll(u, 1, axis=0))
        u2 = jnp.where(row == 0, uh[CONV_HALO - 2:CONV_HALO - 1],
                       jnp.where(row == 1, uh[CONV_HALO - 1:CONV_HALO], pltpu.roll(u, 2, axis=0)))
        cw = cw_ref[...]
        return cb_ref[...] + cw[0:1] * u2 + cw[1:2] * u1 + cw[2:3] * u

    cg = conv_half(wg_ref, cwg_ref, cbg_ref)
    cv = conv_half(wv_ref, cwv_ref, cbv_ref)
    o_ref[...] = (cg * jax.nn.sigmoid(cg) * cv).astype(o_ref.dtype)


def _ffn_up(hn, w_up, conv_w, conv_b, w_down, layer, seq, tm=1024, tn=256):
    t, d = hn.shape
    d_ff = w_up.shape[2] // 2
    n_tiles = d_ff // tn
    n_steps = (t // tm) * n_tiles
    slab = d_ff // n_steps
    assert d_ff % n_steps == 0 and slab % (2 * SUBLANES) == 0
    blocks_per_seq = seq // tm
    halo_blocks = tm // CONV_HALO
    cw = conv_w[layer]
    cb = conv_b[layer].reshape(1, 2 * d_ff)
    return pl.pallas_call(
        functools.partial(_ffn_up_kernel, blocks_per_seq=blocks_per_seq),
        grid=(t // tm, n_tiles),
        in_specs=[pl.BlockSpec((tm, d), lambda i, j: (i, 0)),
                  pl.BlockSpec((CONV_HALO, d), lambda i, j: (jnp.maximum(i * halo_blocks - 1, 0), 0)),
                  pl.BlockSpec((None, d, tn), lambda i, j: (layer, 0, j)),
                  pl.BlockSpec((None, d, tn), lambda i, j: (layer, 0, n_tiles + j)),
                  pl.BlockSpec((CONV_WIDTH, tn), lambda i, j: (0, j)),
                  pl.BlockSpec((CONV_WIDTH, tn), lambda i, j: (0, n_tiles + j)),
                  pl.BlockSpec((1, tn), lambda i, j: (0, j)),
                  pl.BlockSpec((1, tn), lambda i, j: (0, n_tiles + j)),
                  pl.BlockSpec((None, slab, d), lambda i, j: (layer, i * n_tiles + j, 0))],
        out_specs=[pl.BlockSpec((tm, tn), lambda i, j: (i, j)),
                   pl.BlockSpec((slab, d), lambda i, j: (i * n_tiles + j, 0))],
        out_shape=[jax.ShapeDtypeStruct((t, d_ff), BF16),
                   jax.ShapeDtypeStruct((d_ff, d), BF16)],
        compiler_params=_params("parallel", "arbitrary"),
        name="ffn_up",
    )(hn, hn, w_up, w_up, cw, cw, cb, cb, w_down)


def _dilated_pool_layer(h, hn, w_in, q_gain, k_gain, pool_w, pool_scale, w_out, j, bsz, seq):
    d = h.shape[1]
    a_width = d // 2
    heads = a_width // HEAD_DIM
    tn = 512
    q_blocks = a_width // tn
    k_base = N_BRANCH * q_blocks
    v_base = 2 * k_base
    gains = jnp.concatenate([jnp.tile(q_gain, heads), jnp.tile(k_gain, heads)]).reshape(1, 2 * a_width)
    dils = tuple(dil for _, dil in A_BRANCHES)
    v_list = _mm([hn], w_in, j, lambda c: v_base + c, a_width, BF16, dils=dils, seq=seq, tn=tn, name="ab_in_v")
    u = _mm([hn], w_in, j, lambda c: v_base + q_blocks + c, d - a_width, F32, tn=tn, name="ab_in_u")
    slopes = _alibi_slopes()
    outs, lses = [], []
    for g, dil in enumerate(dils):
        def wcol(c, g=g):
            return jnp.where(c < q_blocks, g * q_blocks + c, k_base + g * q_blocks + c - q_blocks)
        qk = _mm([hn], w_in, j, wcol, 2 * a_width, BF16, mode="headnorm", extra=gains, dils=(dil,), seq=seq,
                 tn=tn, name=f"ab_in_qk{g}")
        if dil == 1:
            qk = qk.reshape(bsz, 1, seq, 2 * a_width)
            v_g = v_list[g].reshape(bsz, 1, seq, a_width)
        else:
            v_g = v_list[g]
        o, lse = _band_attention(qk, v_g, bsz, seq, g, slopes[g])
        outs.append(o)
        lses.append(lse)
    a_out = _combine_branches(outs, lses)
    b_out = _pool_mixer(u, pool_w, pool_scale, seq)
    return _mm([a_out, b_out], w_out, j, lambda c: c, d, F32, mode="residual", extra=h, name="ab_out")


def _gla_layer(h, hn, w_in, w_a2, b_a, o_gain, w_out, j, bsz, seq):
    d = h.shape[1]
    kw = w_a2.shape[2]
    vw = w_out.shape[1]
    tn = 512
    w_in_t = jnp.swapaxes(w_in, 1, 2)
    qk = _mm([hn], w_in_t, j, lambda c: c, 2 * kw, F32, w_t=True, tn=tn, name="c_in_qk")
    v = _mm([hn], w_in_t, j, lambda c: 2 * kw // tn + c, vw, BF16, w_t=True, tn=tn, name="c_in_v")
    gate = _mm([hn], w_in_t, j, lambda c: (2 * kw + vw) // tn + c, vw, F32, w_t=True, tn=tn, name="c_in_gate")
    rank = w_a2.shape[1]
    w_r = jnp.pad(w_in_t[j, 2 * kw + 2 * vw:, :], ((0, LANES - rank), (0, 0)))
    w_a2p = jnp.pad(w_a2[j], ((0, LANES - rank), (0, 0)))
    log_a = _gla_log_decay(hn, w_r, w_a2p, b_a[j])
    o = _gla(qk, v, gate, log_a, o_gain[j], bsz, seq)
    return _mm([o], w_out, j, lambda c: c, d, F32, mode="residual", extra=h, name="c_out")


def _memory_kv(mem, gain, wkv, k_gain, layer, bsz):
    xw = wkv.shape[2] // 2
    tn = 256
    mem_n = _rmsnorm(mem, gain)
    rows = mem.shape[0]
    gains = jnp.tile(k_gain, xw // HEAD_DIM).reshape(1, xw)
    k = _mm([mem_n], wkv, layer, lambda c: c, xw, BF16, mode="headnorm", extra=gains, tm=rows, tn=tn, name="mem_k")
    v = _mm([mem_n], wkv, layer, lambda c: xw // tn + c, xw, BF16, tm=rows, tn=tn, name="mem_v")
    return k.reshape(bsz, rows // bsz, xw), v.reshape(bsz, rows // bsz, xw)


def kernel(x, mem, mix_norm, ab_w_in, ab_q_norm, ab_k_norm, ab_pool_w, ab_pool_scale, ab_w_out, c_w_in, c_w_a2, c_b_a, c_o_norm, c_w_out, x_norm, x_mem_norm, x_wq, x_wkv, x_q_norm, x_k_norm, x_wo, f_norm, f_w_up, f_conv_w, f_conv_b, f_w_down):
    bsz, seq, d = x.shape
    depth = mix_norm.shape[0]
    h = x.reshape(bsz * seq, d)
    mem2 = mem.reshape(bsz * mem.shape[1], d)
    for layer in range(depth):
        j = layer // 2
        hn = _rmsnorm(h, mix_norm[layer])
        if layer % 2 == 0:
            h = _dilated_pool_layer(h, hn, ab_w_in, ab_q_norm[j], ab_k_norm[j], ab_pool_w[j], ab_pool_scale[j],
                                    ab_w_out, j, bsz, seq)
        else:
            h = _gla_layer(h, hn, c_w_in, c_w_a2, c_b_a, c_o_norm, c_w_out, j, bsz, seq)
        k, v = _memory_kv(mem2, x_mem_norm[layer], x_wkv, x_k_norm[layer], layer, bsz)
        h, hn = _cross_attention(h, x_norm[layer], x_wq[layer].astype(BF16), x_q_norm[layer], k, v,
                                 x_wo[layer].astype(BF16), f_norm[layer], seq)
        act, w_down = _ffn_up(hn, f_w_up, f_conv_w, f_conv_b, f_w_down, layer, seq)
        h = _mm([act], w_down, None, lambda c: c, d, F32, mode="residual", extra=h, tm=512, name="ffn_down")
    return h.reshape(bsz, seq, d)
```

```python
import functools

import numpy as np
import jax
import jax.numpy as jnp
from jax import lax
from jax.experimental import pallas as pl
from jax.experimental.pallas import tpu as pltpu

F32 = jnp.float32
BF16 = jnp.bfloat16

LANES = 128
SUBLANES = 8
VMEM_LIMIT_BYTES = 56 * 2 ** 20

EPS = 1e-6
HEAD_DIM = 128
A_BRANCHES = ((128, 1), (512, 4), (2048, 16))
N_BRANCH = len(A_BRANCHES)
BAND_BLOCK = 128
POOL_WINDOWS = (2, 4, 8, 16)
POOL_HALO = 16
C_HEADS = 8
C_GATE_RANK = 16
C_GATE_TAU = 16.0
C_CHUNK = 64
X_HEADS = 4
CONV_WIDTH = 3
CONV_HALO = SUBLANES
MM_ROWS = 2048


def _params(*semantics):
    return pltpu.CompilerParams(dimension_semantics=semantics, vmem_limit_bytes=VMEM_LIMIT_BYTES)


def _lane_groups(width):
    return [slice(c * LANES, (c + 1) * LANES) for c in range(width // LANES)]


def _rmsnorm_kernel(x_ref, g_ref, o_ref):
    x = x_ref[...].astype(F32)
    ms = jnp.mean(x * x, axis=-1, keepdims=True)
    o_ref[...] = (x * lax.rsqrt(ms + EPS) * g_ref[...]).astype(o_ref.dtype)


def _rmsnorm(x, gain, tm=256):
    t, d = x.shape
    return pl.pallas_call(
        _rmsnorm_kernel,
        grid=(t // tm,),
        in_specs=[pl.BlockSpec((tm, d), lambda i: (i, 0)),
                  pl.BlockSpec((1, d), lambda i: (0, 0))],
        out_specs=pl.BlockSpec((tm, d), lambda i: (i, 0)),
        out_shape=jax.ShapeDtypeStruct((t, d), BF16),
        compiler_params=_params("parallel"),
        name="rmsnorm",
    )(x, gain.reshape(1, d))


def _mm_kernel(*refs, n_a, mode, dils, w_t):
    a_refs = refs[:n_a]
    w_ref = refs[n_a]
    n_extra = 1 if mode in ("headnorm", "residual") else 0
    extra = refs[n_a + 1:n_a + 1 + n_extra]
    o_refs = refs[n_a + 1 + n_extra:n_a + 1 + n_extra + len(dils)]
    scratch = refs[n_a + 1 + n_extra + len(dils):]
    acc = None
    k0 = 0
    for a_ref in a_refs:
        kk = a_ref.shape[1]
        if w_t:
            part = lax.dot_general(a_ref[...], w_ref[:, k0:k0 + kk].astype(BF16), (((1,), (1,)), ((), ())),
                                   preferred_element_type=F32)
        else:
            part = jnp.dot(a_ref[...], w_ref[k0:k0 + kk, :].astype(BF16), preferred_element_type=F32)
        acc = part if acc is None else acc + part
        k0 += kk
    tm, tn = acc.shape
    for c, sl in enumerate(_lane_groups(tn)):
        blk = acc[:, sl]
        if mode == "headnorm":
            ms = jnp.mean(blk * blk, axis=-1, keepdims=True)
            blk = blk * lax.rsqrt(ms + EPS) * extra[0][:, sl]
        elif mode == "residual":
            blk = extra[0][:, sl] + blk
        if scratch:
            scratch[0][c] = blk
        for o_ref, dil in zip(o_refs, dils):
            if dil == 1:
                o_ref[:, sl] = blk.astype(o_ref.dtype)
    for o_ref, dil in zip(o_refs, dils):
        if dil > 1:
            for r in range(dil):
                for c, sl in enumerate(_lane_groups(tn)):
                    o_ref[r, :, sl] = scratch[0][c, pl.ds(r, tm // dil, stride=dil), :].astype(o_ref.dtype)


def _mm(a_list, w, layer, wcol, ncols, out_dtype, *, mode="plain", extra=None, dils=(1,), seq=None,
        w_t=False, tm=MM_ROWS, tn=512, a_buffers=1, name="mm"):
    t = a_list[0].shape[0]
    k_total = sum(a.shape[1] for a in a_list)
    assert w.shape[-1 if w_t else -2] == k_total and ncols % tn == 0 and t % tm == 0
    in_specs = [pl.BlockSpec((tm, a.shape[1]), lambda i, j: (i, 0), pipeline_mode=pl.Buffered(a_buffers))
                for a in a_list]
    if w_t:
        in_specs.append(pl.BlockSpec((None, tn, k_total), lambda i, j: (layer, wcol(j), 0)))
    elif w.ndim == 3:
        in_specs.append(pl.BlockSpec((None, k_total, tn), lambda i, j: (layer, 0, wcol(j))))
    else:
        in_specs.append(pl.BlockSpec((k_total, tn), lambda i, j: (0, wcol(j))))
    args = list(a_list) + [w]
    if mode == "headnorm":
        in_specs.append(pl.BlockSpec((1, tn), lambda i, j: (0, j)))
        args.append(extra)
    elif mode == "residual":
        in_specs.append(pl.BlockSpec((tm, tn), lambda i, j: (i, j)))
        args.append(extra)
    out_specs, out_shapes = [], []
    for dil in dils:
        if dil == 1:
            out_specs.append(pl.BlockSpec((tm, tn), lambda i, j: (i, j)))
            out_shapes.append(jax.ShapeDtypeStruct((t, ncols), out_dtype))
        else:
            tiles = seq // tm
            assert seq % tm == 0 and tm % dil == 0
            out_specs.append(pl.BlockSpec((None, dil, tm // dil, tn), lambda i, j: (i // tiles, 0, i % tiles, j)))
            out_shapes.append(jax.ShapeDtypeStruct((t // seq, dil, seq // dil, ncols), out_dtype))
    scratch = [pltpu.VMEM((tn // LANES, tm, LANES), F32)] if any(dil > 1 for dil in dils) else []
    outs = pl.pallas_call(
        functools.partial(_mm_kernel, n_a=len(a_list), mode=mode, dils=tuple(dils), w_t=w_t),
        grid=(t // tm, ncols // tn),
        in_specs=in_specs,
        out_specs=out_specs,
        out_shape=out_shapes,
        scratch_shapes=scratch,
        compiler_params=_params("parallel", "arbitrary"),
        name=name,
    )(*args)
    return outs[0] if len(dils) == 1 else outs


def _alibi_slopes():
    n = N_BRANCH * (2048 // HEAD_DIM)
    s = np.power(np.float32(2.0), -8.0 * np.arange(1, n + 1, dtype=np.float32) / np.float32(n)).astype(np.float32)
    return s.reshape(N_BRANCH, -1)


def _band_attn_kernel(q_ref, kp_ref, kc_ref, vp_ref, vc_ref, o_ref, lse_ref, *scratch, dilation, slopes):
    blk = BAND_BLOCK
    hg = pl.program_id(2)
    has_prev = pl.program_id(1) > 0
    n_heads = len(slopes[0])
    qi = lax.broadcasted_iota(jnp.int32, (blk, 2 * blk), 0)
    ki = lax.broadcasted_iota(jnp.int32, (blk, 2 * blk), 1)
    rel = qi + blk - ki
    valid = jnp.logical_and(jnp.logical_and(rel >= 0, rel <= blk), jnp.logical_or(ki >= blk, has_prev))
    dist = (rel * dilation).astype(F32)
    lane = lax.broadcasted_iota(jnp.int32, (blk, LANES), 1)
    scale = HEAD_DIM ** -0.5
    contract_last = (((1,), (1,)), ((), ()))
    for r in range(dilation):
        lse_tile = jnp.zeros((blk, LANES), F32)
        for h in range(n_heads):
            sl = slice(h * HEAD_DIM, (h + 1) * HEAD_DIM)
            k = jnp.concatenate([kp_ref[r, :, sl], kc_ref[r, :, sl]], axis=0)
            v = jnp.concatenate([vp_ref[r, :, sl], vc_ref[r, :, sl]], axis=0)
            s = lax.dot_general(q_ref[r, :, sl], k, contract_last, preferred_element_type=F32) * scale
            slope = slopes[0][h]
            for g in range(1, len(slopes)):
                slope = jnp.where(hg == g, slopes[g][h], slope)
            s = jnp.where(valid, s - slope * dist, -jnp.inf)
            m = jnp.max(s, axis=-1, keepdims=True)
            p = jnp.exp(s - m)
            den = jnp.sum(p, axis=-1, keepdims=True)
            o = jnp.dot(p.astype(BF16), v, preferred_element_type=F32) / den
            lse_h = m + jnp.log(den)
            for g in range(len(slopes)):
                lse_tile = jnp.where(jnp.logical_and(lane == g * n_heads + h, hg == g), lse_h, lse_tile)
            if dilation == 1:
                o_ref[:, sl] = o
            else:
                scratch[0][h, pl.ds(r, blk, stride=dilation), :] = o
        if dilation == 1:
            lse_slab = lse_tile
        else:
            scratch[1][pl.ds(r, blk, stride=dilation), :] = lse_tile
    if dilation > 1:
        for h in range(n_heads):
            o_ref[:, h * HEAD_DIM:(h + 1) * HEAD_DIM] = scratch[0][h]
        lse_slab = scratch[1][...]

    @pl.when(hg == 0)
    def _():
        lse_ref[...] = lse_slab

    @pl.when(hg > 0)
    def _():
        lse_ref[...] += lse_slab


def _band_attention(qk, v, bsz, seq, branch, slopes):
    _, dilation = A_BRANCHES[branch]
    width = v.shape[-1]
    n_blk = seq // dilation // BAND_BLOCK
    hw = max(width * 4 // dilation, 512) if dilation > 1 else width
    hw = min(hw, width)
    n_hg = width // hw
    heads = hw // HEAD_DIM
    slope_tab = tuple(tuple(float(s) for s in slopes[g * heads:(g + 1) * heads]) for g in range(n_hg))
    rows = BAND_BLOCK * dilation
    blk = (None, dilation, BAND_BLOCK, hw)

    def prev(n):
        return jnp.maximum(n - 1, 0)

    scratch = []
    if dilation > 1:
        scratch = [pltpu.VMEM((heads, rows, HEAD_DIM), F32), pltpu.VMEM((rows, LANES), F32)]
    return pl.pallas_call(
        functools.partial(_band_attn_kernel, dilation=dilation, slopes=slope_tab),
        grid=(bsz, n_blk, n_hg),
        in_specs=[pl.BlockSpec(blk, lambda b, n, g: (b, 0, n, g)),
                  pl.BlockSpec(blk, lambda b, n, g: (b, 0, prev(n), n_hg + g)),
                  pl.BlockSpec(blk, lambda b, n, g: (b, 0, n, n_hg + g)),
                  pl.BlockSpec(blk, lambda b, n, g: (b, 0, prev(n), g)),
                  pl.BlockSpec(blk, lambda b, n, g: (b, 0, n, g))],
        out_specs=[pl.BlockSpec((rows, hw), lambda b, n, g: (b * n_blk + n, g)),
                   pl.BlockSpec((rows, LANES), lambda b, n, g: (b * n_blk + n, 0))],
        out_shape=[jax.ShapeDtypeStruct((bsz * seq, width), F32),
                   jax.ShapeDtypeStruct((bsz * seq, LANES), F32)],
        scratch_shapes=scratch,
        compiler_params=_params("parallel", "arbitrary", "arbitrary"),
        name=f"band_attn_d{dilation}",
    )(qk, qk, qk, v, v)


def _combine_kernel(o0_ref, o1_ref, o2_ref, l0_ref, l1_ref, l2_ref, out_ref):
    l0, l1, l2 = l0_ref[...], l1_ref[...], l2_ref[...]
    m = jnp.maximum(jnp.maximum(l0, l1), l2)
    e0, e1, e2 = jnp.exp(l0 - m), jnp.exp(l1 - m), jnp.exp(l2 - m)
    tot = e0 + e1 + e2
    w0, w1, w2 = e0 / tot, e1 / tot, e2 / tot
    for h in range(out_ref.shape[1] // HEAD_DIM):
        sl = slice(h * HEAD_DIM, (h + 1) * HEAD_DIM)
        acc = w0[:, h:h + 1] * o0_ref[:, sl] + w1[:, h:h + 1] * o1_ref[:, sl] + w2[:, h:h + 1] * o2_ref[:, sl]
        out_ref[:, sl] = acc.astype(out_ref.dtype)


def _combine_branches(outs, lses, tm=512):
    t, width = outs[0].shape
    o_spec = pl.BlockSpec((tm, width), lambda i: (i, 0))
    l_spec = pl.BlockSpec((tm, LANES), lambda i: (i, 0))
    return pl.pallas_call(
        _combine_kernel,
        grid=(t // tm,),
        in_specs=[o_spec] * 3 + [l_spec] * 3,
        out_specs=o_spec,
        out_shape=jax.ShapeDtypeStruct((t, width), BF16),
        compiler_params=_params("parallel"),
        name="combine_branches",
    )(*outs, *lses)


def _pool_kernel(u_ref, uh_ref, w_ref, sc_ref, o_ref, wb_ref, *, blocks_per_seq):
    i = pl.program_id(0)

    @pl.when(i == 0)
    def _():
        wb_ref[...] = w_ref[...].astype(BF16)

    tm = u_ref.shape[0]
    group = w_ref.shape[1]
    first = (i % blocks_per_seq) == 0
    row = lax.broadcasted_iota(jnp.int32, (tm, 1), 0)
    pos = (i % blocks_per_seq) * tm + row
    for g, win in enumerate(POOL_WINDOWS):
        sl = slice(g * group, (g + 1) * group)
        u = u_ref[:, sl]
        halo = jnp.where(first, 0.0, uh_ref[:, sl])
        s = jnp.concatenate([halo, u], axis=0)
        step = 1
        while step < win:
            s = s + pltpu.roll(s, step, axis=0)
            step *= 2
        count = jnp.minimum(pos + 1, win).astype(F32)
        pooled = s[POOL_HALO:] / count - u
        y = jnp.dot(pooled.astype(BF16), wb_ref[g], preferred_element_type=F32)
        o_ref[:, sl] = (y * sc_ref[:, sl]).astype(o_ref.dtype)


def _pool_mixer(u, pool_w, pool_scale, seq, tm=512):
    t, width = u.shape
    n_group, group, _ = pool_w.shape
    blocks_per_seq = seq // tm
    halo_blocks = tm // POOL_HALO
    return pl.pallas_call(
        functools.partial(_pool_kernel, blocks_per_seq=blocks_per_seq),
        grid=(t // tm,),
        in_specs=[pl.BlockSpec((tm, width), lambda i: (i, 0)),
                  pl.BlockSpec((POOL_HALO, width), lambda i: (jnp.maximum(i * halo_blocks - 1, 0), 0)),
                  pl.BlockSpec((n_group, group, group), lambda i: (0, 0, 0)),
                  pl.BlockSpec((1, width), lambda i: (0, 0))],
        out_specs=pl.BlockSpec((tm, width), lambda i: (i, 0)),
        out_shape=jax.ShapeDtypeStruct((t, width), BF16),
        scratch_shapes=[pltpu.VMEM((n_group, group, group), BF16)],
        compiler_params=_params("arbitrary"),
        name="pool_mixer",
    )(u, u, pool_w, pool_scale.reshape(1, width))


def _split3(x):
    hi = x.astype(BF16)
    r1 = x - hi.astype(F32)
    mid = r1.astype(BF16)
    lo = (r1 - mid.astype(F32)).astype(BF16)
    return hi, mid, lo


def _gate_kernel(a_ref, wr_ref, wa2_ref, ba_ref, o_ref):
    r = lax.dot_general(a_ref[...], wr_ref[...].astype(BF16), (((1,), (1,)), ((), ())),
                        preferred_element_type=F32)
    r_hi, r_mid, r_lo = _split3(r)
    w_hi, w_mid, w_lo = _split3(wa2_ref[...])
    g = (jnp.dot(r_hi, w_hi, preferred_element_type=F32)
         + (jnp.dot(r_hi, w_mid, preferred_element_type=F32) + jnp.dot(r_mid, w_hi, preferred_element_type=F32))
         + (jnp.dot(r_hi, w_lo, preferred_element_type=F32) + jnp.dot(r_mid, w_mid, preferred_element_type=F32)
            + jnp.dot(r_lo, w_hi, preferred_element_type=F32)))
    g = g + ba_ref[...]
    log_sig = jnp.minimum(g, 0.0) - jnp.log1p(jnp.exp(-jnp.abs(g)))
    o_ref[...] = log_sig / C_GATE_TAU


def _gla_log_decay(hn, w_r, w_a2, b_a, tm=512):
    t, d = hn.shape
    kw = w_a2.shape[1]
    return pl.pallas_call(
        _gate_kernel,
        grid=(t // tm,),
        in_specs=[pl.BlockSpec((tm, d), lambda i: (i, 0)),
                  pl.BlockSpec((LANES, d), lambda i: (0, 0)),
                  pl.BlockSpec((LANES, kw), lambda i: (0, 0)),
                  pl.BlockSpec((1, kw), lambda i: (0, 0))],
        out_specs=pl.BlockSpec((tm, kw), lambda i: (i, 0)),
        out_shape=jax.ShapeDtypeStruct((t, kw), F32),
        compiler_params=_params("parallel"),
        name="gla_log_decay",
    )(hn, w_r, w_a2, b_a.reshape(1, kw))


def _gla_kernel(q_ref, k_ref, v_ref, la_ref, gate_ref, og_ref, o_ref, state_ref, *, chunks_per_step):
    @pl.when(pl.program_id(2) == 0)
    def _():
        state_ref[...] = jnp.zeros_like(state_ref)

    c = C_CHUNK
    dk = q_ref.shape[1]
    ri = lax.broadcasted_iota(jnp.int32, (c, c), 0)
    ci = lax.broadcasted_iota(jnp.int32, (c, c), 1)
    causal = ci <= ri
    tri = causal.astype(BF16)
    contract_last = (((1,), (1,)), ((), ()))
    contract_first = (((0,), (0,)), ((), ()))
    for n in range(chunks_per_step):
        rows = slice(n * c, (n + 1) * c)
        la = la_ref[rows, :]
        la_hi, la_mid, la_lo = _split3(la)
        bc = (jnp.dot(tri, la_hi, preferred_element_type=F32)
              + jnp.dot(tri, la_mid, preferred_element_type=F32)
              + jnp.dot(tri, la_lo, preferred_element_type=F32))
        b_last = bc[c - 1:c, :]
        q_in = q_ref[rows, :] * (dk ** -0.5) * jnp.exp(bc)
        k = k_ref[rows, :]
        k_in = k * jnp.exp(-bc)
        k_st = k * jnp.exp(b_last - bc)
        v = v_ref[rows, :]
        q_b = q_in.astype(BF16)
        att = lax.dot_general(q_b, k_in.astype(BF16), contract_last, preferred_element_type=F32)
        att = jnp.where(causal, att, 0.0)
        o = jnp.dot(att.astype(BF16), v, preferred_element_type=F32)
        state = state_ref[...]
        o = o + lax.dot_general(q_b, state.astype(BF16), contract_last, preferred_element_type=F32)
        upd = lax.dot_general(v, k_st.astype(BF16), contract_first, preferred_element_type=F32)
        state_ref[...] = state * jnp.exp(b_last) + upd
        ms = jnp.mean(o * o, axis=-1, keepdims=True)
        gate = gate_ref[rows, :]
        y = o * lax.rsqrt(ms + EPS) * og_ref[...] * (gate * jax.nn.sigmoid(gate))
        o_ref[rows, :] = y.astype(o_ref.dtype)


def _gla(qk, v, gate, log_a, o_gain, bsz, seq, rows=256):
    t = bsz * seq
    dk = log_a.shape[1] // C_HEADS
    dv = v.shape[1] // C_HEADS
    steps = seq // rows

    def row_block(b, n):
        return b * steps + n

    return pl.pallas_call(
        functools.partial(_gla_kernel, chunks_per_step=rows // C_CHUNK),
        grid=(bsz, C_HEADS, steps),
        in_specs=[pl.BlockSpec((rows, dk), lambda b, h, n: (row_block(b, n), h)),
                  pl.BlockSpec((rows, dk), lambda b, h, n: (row_block(b, n), C_HEADS + h)),
                  pl.BlockSpec((rows, dv), lambda b, h, n: (row_block(b, n), h)),
                  pl.BlockSpec((rows, dk), lambda b, h, n: (row_block(b, n), h)),
                  pl.BlockSpec((rows, dv), lambda b, h, n: (row_block(b, n), h)),
                  pl.BlockSpec((1, dv), lambda b, h, n: (0, 0))],
        out_specs=pl.BlockSpec((rows, dv), lambda b, h, n: (row_block(b, n), h)),
        out_shape=jax.ShapeDtypeStruct((t, v.shape[1]), BF16),
        scratch_shapes=[pltpu.VMEM((dv, dk), F32)],
        compiler_params=_params("parallel", "parallel", "arbitrary"),
        name="gla",
    )(qk, qk, v, log_a, gate, o_gain.reshape(1, dv))


def _xattn_kernel(h_ref, g_ref, wq_ref, qg_ref, k_ref, v_ref, wo_ref, fg_ref, o_ref, on_ref):
    x = h_ref[...]
    ms = jnp.mean(x * x, axis=-1, keepdims=True)
    hn = (x * lax.rsqrt(ms + EPS) * g_ref[...]).astype(BF16)
    q = jnp.dot(hn, wq_ref[...], preferred_element_type=F32)
    scale = HEAD_DIM ** -0.5
    contract_last = (((1,), (1,)), ((), ()))
    heads = []
    for hd in range(X_HEADS):
        sl = slice(hd * HEAD_DIM, (hd + 1) * HEAD_DIM)
        qh = q[:, sl]
        qms = jnp.mean(qh * qh, axis=-1, keepdims=True)
        qh = (qh * lax.rsqrt(qms + EPS) * qg_ref[...]).astype(BF16)
        s = lax.dot_general(qh, k_ref[0, :, sl], contract_last, preferred_element_type=F32) * scale
        m = jnp.max(s, axis=-1, keepdims=True)
        p = jnp.exp(s - m)
        p = p / jnp.sum(p, axis=-1, keepdims=True)
        heads.append(jnp.dot(p.astype(BF16), v_ref[0, :, sl], preferred_element_type=F32))
    o = jnp.concatenate(heads, axis=-1).astype(BF16)
    y = x + jnp.dot(o, wo_ref[...], preferred_element_type=F32)
    o_ref[...] = y
    yms = jnp.mean(y * y, axis=-1, keepdims=True)
    on_ref[...] = (y * lax.rsqrt(yms + EPS) * fg_ref[...]).astype(on_ref.dtype)


def _cross_attention(h, norm_gain, wq, q_gain, k, v, wo, next_gain, seq, tm=256):
    t, d = h.shape
    xw = wq.shape[1]
    mlen = k.shape[1]
    tiles_per_seq = seq // tm
    return pl.pallas_call(
        _xattn_kernel,
        grid=(t // tm,),
        in_specs=[pl.BlockSpec((tm, d), lambda i: (i, 0)),
                  pl.BlockSpec((1, d), lambda i: (0, 0)),
                  pl.BlockSpec((d, xw), lambda i: (0, 0)),
                  pl.BlockSpec((1, HEAD_DIM), lambda i: (0, 0)),
                  pl.BlockSpec((1, mlen, xw), lambda i: (i // tiles_per_seq, 0, 0)),
                  pl.BlockSpec((1, mlen, xw), lambda i: (i // tiles_per_seq, 0, 0)),
                  pl.BlockSpec((xw, d), lambda i: (0, 0)),
                  pl.BlockSpec((1, d), lambda i: (0, 0))],
        out_specs=[pl.BlockSpec((tm, d), lambda i: (i, 0)),
                   pl.BlockSpec((tm, d), lambda i: (i, 0))],
        out_shape=[jax.ShapeDtypeStruct((t, d), F32),
                   jax.ShapeDtypeStruct((t, d), BF16)],
        compiler_params=_params("parallel"),
        name="cross_attention",
    )(h, norm_gain.reshape(1, d), wq, q_gain.reshape(1, HEAD_DIM), k, v, wo, next_gain.reshape(1, d))


def _ffn_up_kernel(a_ref, ah_ref, wg_ref, wv_ref, cwg_ref, cwv_ref, cbg_ref, cbv_ref, wd_ref, o_ref, wdo_ref, *,
                   blocks_per_seq):
    wdo_ref[...] = wd_ref[...].astype(wdo_ref.dtype)
    tm = a_ref.shape[0]
    first = (pl.program_id(0) % blocks_per_seq) == 0
    row = lax.broadcasted_iota(jnp.int32, (tm, 1), 0)
    a = a_ref[...]
    ah = ah_ref[...]

    def conv_half(w_ref, cw_ref, cb_ref):
        w = w_ref[...].astype(BF16)
        u = jnp.dot(a, w, preferred_element_type=F32)
        uh = jnp.where(first, 0.0, jnp.dot(ah, w, preferred_element_type=F32))
        u1 = jnp.where(row == 0, uh[CONV_HALO - 1:CONV_HALO], pltpu.roll(u, 1, axis=0))
        u2 = jnp.where(row == 0, uh[CONV_HALO - 2:CONV_HALO - 1],
                       jnp.where(row == 1, uh[CONV_HALO - 1:CONV_HALO], pltpu.roll(u, 2, axis=0)))
        cw = cw_ref[...]
        return cb_ref[...] + cw[0:1] * u2 + cw[1:2] * u1 + cw[2:3] * u

    cg = conv_half(wg_ref, cwg_ref, cbg_ref)
    cv = conv_half(wv_ref, cwv_ref, cbv_ref)
    o_ref[...] = (cg * jax.nn.sigmoid(cg) * cv).astype(o_ref.dtype)


def _ffn_up(hn, w_up, conv_w, conv_b, w_down, layer, seq, tm=MM_ROWS, tn=256):
    t, d = hn.shape
    d_ff = w_up.shape[2] // 2
    n_tiles = d_ff // tn
    n_steps = (t // tm) * n_tiles
    slab = d_ff // n_steps
    assert d_ff % n_steps == 0 and slab % (2 * SUBLANES) == 0
    blocks_per_seq = seq // tm
    halo_blocks = tm // CONV_HALO
    cw = conv_w[layer]
    cb = conv_b[layer].reshape(1, 2 * d_ff)
    return pl.pallas_call(
        functools.partial(_ffn_up_kernel, blocks_per_seq=blocks_per_seq),
        grid=(t // tm, n_tiles),
        in_specs=[pl.BlockSpec((tm, d), lambda i, j: (i, 0), pipeline_mode=pl.Buffered(1)),
                  pl.BlockSpec((CONV_HALO, d), lambda i, j: (jnp.maximum(i * halo_blocks - 1, 0), 0)),
                  pl.BlockSpec((None, d, tn), lambda i, j: (layer, 0, j)),
                  pl.BlockSpec((None, d, tn), lambda i, j: (layer, 0, n_tiles + j)),
                  pl.BlockSpec((CONV_WIDTH, tn), lambda i, j: (0, j)),
                  pl.BlockSpec((CONV_WIDTH, tn), lambda i, j: (0, n_tiles + j)),
                  pl.BlockSpec((1, tn), lambda i, j: (0, j)),
                  pl.BlockSpec((1, tn), lambda i, j: (0, n_tiles + j)),
                  pl.BlockSpec((None, slab, d), lambda i, j: (layer, i * n_tiles + j, 0))],
        out_specs=[pl.BlockSpec((tm, tn), lambda i, j: (i, j)),
                   pl.BlockSpec((slab, d), lambda i, j: (i * n_tiles + j, 0))],
        out_shape=[jax.ShapeDtypeStruct((t, d_ff), BF16),
                   jax.ShapeDtypeStruct((d_ff, d), BF16)],
        compiler_params=_params("parallel", "arbitrary"),
        name="ffn_up",
    )(hn, hn, w_up, w_up, cw, cw, cb, cb, w_down)


def _dilated_pool_layer(h, hn, w_in, q_gain, k_gain, pool_w, pool_scale, w_out, j, bsz, seq):
    d = h.shape[1]
    a_width = d // 2
    heads = a_width // HEAD_DIM
    tn = 512
    tn_v = 256
    q_blocks = a_width // tn
    k_base = N_BRANCH * q_blocks
    v_cols = 2 * N_BRANCH * a_width
    gains = jnp.concatenate([jnp.tile(q_gain, heads), jnp.tile(k_gain, heads)]).reshape(1, 2 * a_width)
    dils = tuple(dil for _, dil in A_BRANCHES)
    v_list = _mm([hn], w_in, j, lambda c: v_cols // tn_v + c, a_width, BF16, dils=dils, seq=seq, tn=tn_v,
                 name="ab_in_v")
    u = _mm([hn], w_in, j, lambda c: (v_cols + a_width) // tn + c, d - a_width, F32, tn=tn, name="ab_in_u")
    slopes = _alibi_slopes()
    outs, lses = [], []
    for g, dil in enumerate(dils):
        def wcol(c, g=g):
            return jnp.where(c < q_blocks, g * q_blocks + c, k_base + g * q_blocks + c - q_blocks)
        qk = _mm([hn], w_in, j, wcol, 2 * a_width, BF16, mode="headnorm", extra=gains, dils=(dil,), seq=seq,
                 tn=tn, name=f"ab_in_qk{g}")
        if dil == 1:
            qk = qk.reshape(bsz, 1, seq, 2 * a_width)
            v_g = v_list[g].reshape(bsz, 1, seq, a_width)
        else:
            v_g = v_list[g]
        o, lse = _band_attention(qk, v_g, bsz, seq, g, slopes[g])
        outs.append(o)
        lses.append(lse)
    a_out = _combine_branches(outs, lses)
    b_out = _pool_mixer(u, pool_w, pool_scale, seq)
    return _mm([a_out, b_out], w_out, j, lambda c: c, d, F32, mode="residual", extra=h, tn=256, name="ab_out")


def _gla_layer(h, hn, w_in, w_a2, b_a, o_gain, w_out, j, bsz, seq):
    d = h.shape[1]
    kw = w_a2.shape[2]
    vw = w_out.shape[1]
    tn = 512
    w_in_t = jnp.swapaxes(w_in, 1, 2)
    qk = _mm([hn], w_in_t, j, lambda c: c, 2 * kw, F32, w_t=True, tn=tn, name="c_in_qk")
    v = _mm([hn], w_in_t, j, lambda c: 2 * kw // tn + c, vw, BF16, w_t=True, tn=tn, name="c_in_v")
    gate = _mm([hn], w_in_t, j, lambda c: (2 * kw + vw) // tn + c, vw, F32, w_t=True, tn=tn, name="c_in_gate")
    rank = w_a2.shape[1]
    w_r = jnp.pad(w_in_t[j, 2 * kw + 2 * vw:, :], ((0, LANES - rank), (0, 0)))
    w_a2p = jnp.pad(w_a2[j], ((0, LANES - rank), (0, 0)))
    log_a = _gla_log_decay(hn, w_r, w_a2p, b_a[j])
    o = _gla(qk, v, gate, log_a, o_gain[j], bsz, seq)
    return _mm([o], w_out, j, lambda c: c, d, F32, mode="residual", extra=h, tn=256, name="c_out")


def _memory_kv(mem, gain, wkv, k_gain, layer, bsz):
    xw = wkv.shape[2] // 2
    tn = 256
    mem_n = _rmsnorm(mem, gain)
    rows = mem.shape[0]
    gains = jnp.tile(k_gain, xw // HEAD_DIM).reshape(1, xw)
    k = _mm([mem_n], wkv, layer, lambda c: c, xw, BF16, mode="headnorm", extra=gains, tm=rows, tn=tn, name="mem_k")
    v = _mm([mem_n], wkv, layer, lambda c: xw // tn + c, xw, BF16, tm=rows, tn=tn, name="mem_v")
    return k.reshape(bsz, rows // bsz, xw), v.reshape(bsz, rows // bsz, xw)


def kernel(x, mem, mix_norm, ab_w_in, ab_q_norm, ab_k_norm, ab_pool_w, ab_pool_scale, ab_w_out, c_w_in, c_w_a2, c_b_a, c_o_norm, c_w_out, x_norm, x_mem_norm, x_wq, x_wkv, x_q_norm, x_k_norm, x_wo, f_norm, f_w_up, f_conv_w, f_conv_b, f_w_down):
    bsz, seq, d = x.shape
    depth = mix_norm.shape[0]
    h = x.reshape(bsz * seq, d)
    mem2 = mem.reshape(bsz * mem.shape[1], d)
    for layer in range(depth):
        j = layer // 2
        hn = _rmsnorm(h, mix_norm[layer])
        if layer % 2 == 0:
            h = _dilated_pool_layer(h, hn, ab_w_in, ab_q_norm[j], ab_k_norm[j], ab_pool_w[j], ab_pool_scale[j],
                                    ab_w_out, j, bsz, seq)
        else:
            h = _gla_layer(h, hn, c_w_in, c_w_a2, c_b_a, c_o_norm, c_w_out, j, bsz, seq)
        k, v = _memory_kv(mem2, x_mem_norm[layer], x_wkv, x_k_norm[layer], layer, bsz)
        h, hn = _cross_attention(h, x_norm[layer], x_wq[layer].astype(BF16), x_q_norm[layer], k, v,
                                 x_wo[layer].astype(BF16), f_norm[layer], seq)
        act, w_down = _ffn_up(hn, f_w_up, f_conv_w, f_conv_b, f_w_down, layer, seq)
        h = _mm([act], w_down, None, lambda c: c, d, F32, mode="residual", extra=h, tm=512, a_buffers=2,
                name="ffn_down")
    return h.reshape(bsz, seq, d)
```

```python
import functools

import numpy as np
import jax
import jax.numpy as jnp
from jax import lax
from jax.experimental import pallas as pl
from jax.experimental.pallas import tpu as pltpu

F32 = jnp.float32
BF16 = jnp.bfloat16

LANES = 128
SUBLANES = 8
VMEM_LIMIT_BYTES = 56 * 2 ** 20

EPS = 1e-6
HEAD_DIM = 128
A_BRANCHES = ((128, 1), (512, 4), (2048, 16))
N_BRANCH = len(A_BRANCHES)
BAND_BLOCK = 128
POOL_WINDOWS = (2, 4, 8, 16)
POOL_HALO = 16
C_HEADS = 8
C_GATE_RANK = 16
C_GATE_TAU = 16.0
C_CHUNK = 64
X_HEADS = 4
CONV_WIDTH = 3
CONV_HALO = SUBLANES
MM_ROWS = 1024
ATTN_STAGE_HEADS = {1: 8, 4: 16, 16: 1}
FFN_ROWS = 2048


def _params(*semantics):
    return pltpu.CompilerParams(dimension_semantics=semantics, vmem_limit_bytes=VMEM_LIMIT_BYTES)


def _lane_groups(width):
    return [slice(c * LANES, (c + 1) * LANES) for c in range(width // LANES)]


def _rmsnorm_kernel(x_ref, g_ref, o_ref):
    x = x_ref[...].astype(F32)
    ms = jnp.mean(x * x, axis=-1, keepdims=True)
    o_ref[...] = (x * lax.rsqrt(ms + EPS) * g_ref[...]).astype(o_ref.dtype)


def _rmsnorm(x, gain, tm=256):
    t, d = x.shape
    return pl.pallas_call(
        _rmsnorm_kernel,
        grid=(t // tm,),
        in_specs=[pl.BlockSpec((tm, d), lambda i: (i, 0)),
                  pl.BlockSpec((1, d), lambda i: (0, 0))],
        out_specs=pl.BlockSpec((tm, d), lambda i: (i, 0)),
        out_shape=jax.ShapeDtypeStruct((t, d), BF16),
        compiler_params=_params("parallel"),
        name="rmsnorm",
    )(x, gain.reshape(1, d))


def _mm_kernel(*refs, n_a, mode, dils, w_t):
    a_refs = refs[:n_a]
    w_ref = refs[n_a]
    n_extra = 1 if mode in ("headnorm", "residual") else 0
    extra = refs[n_a + 1:n_a + 1 + n_extra]
    o_refs = refs[n_a + 1 + n_extra:n_a + 1 + n_extra + len(dils)]
    scratch = refs[n_a + 1 + n_extra + len(dils):]
    acc = None
    k0 = 0
    for a_ref in a_refs:
        kk = a_ref.shape[1]
        if w_t:
            part = lax.dot_general(a_ref[...], w_ref[:, k0:k0 + kk].astype(BF16), (((1,), (1,)), ((), ())),
                                   preferred_element_type=F32)
        else:
            part = jnp.dot(a_ref[...], w_ref[k0:k0 + kk, :].astype(BF16), preferred_element_type=F32)
        acc = part if acc is None else acc + part
        k0 += kk
    tm, tn = acc.shape
    for c, sl in enumerate(_lane_groups(tn)):
        blk = acc[:, sl]
        if mode == "headnorm":
            ms = jnp.mean(blk * blk, axis=-1, keepdims=True)
            blk = blk * lax.rsqrt(ms + EPS) * extra[0][:, sl]
        elif mode == "residual":
            blk = extra[0][:, sl] + blk
        if scratch:
            scratch[0][c] = blk
        for o_ref, dil in zip(o_refs, dils):
            if dil == 1:
                o_ref[:, sl] = blk.astype(o_ref.dtype)
    for o_ref, dil in zip(o_refs, dils):
        if dil > 1:
            for r in range(dil):
                for c, sl in enumerate(_lane_groups(tn)):
                    o_ref[r, :, sl] = scratch[0][c, pl.ds(r, tm // dil, stride=dil), :].astype(o_ref.dtype)


def _mm(a_list, w, layer, wcol, ncols, out_dtype, *, mode="plain", extra=None, dils=(1,), seq=None,
        w_t=False, tm=MM_ROWS, tn=512, name="mm"):
    t = a_list[0].shape[0]
    k_total = sum(a.shape[1] for a in a_list)
    assert w.shape[-1 if w_t else -2] == k_total and ncols % tn == 0 and t % tm == 0
    in_specs = [pl.BlockSpec((tm, a.shape[1]), lambda i, j: (i, 0)) for a in a_list]
    if w_t:
        in_specs.append(pl.BlockSpec((None, tn, k_total), lambda i, j: (layer, wcol(j), 0)))
    elif w.ndim == 3:
        in_specs.append(pl.BlockSpec((None, k_total, tn), lambda i, j: (layer, 0, wcol(j))))
    else:
        in_specs.append(pl.BlockSpec((k_total, tn), lambda i, j: (0, wcol(j))))
    args = list(a_list) + [w]
    if mode == "headnorm":
        in_specs.append(pl.BlockSpec((1, tn), lambda i, j: (0, j)))
        args.append(extra)
    elif mode == "residual":
        in_specs.append(pl.BlockSpec((tm, tn), lambda i, j: (i, j)))
        args.append(extra)
    out_specs, out_shapes = [], []
    for dil in dils:
        if dil == 1:
            out_specs.append(pl.BlockSpec((tm, tn), lambda i, j: (i, j)))
            out_shapes.append(jax.ShapeDtypeStruct((t, ncols), out_dtype))
        else:
            tiles = seq // tm
            assert seq % tm == 0 and tm % dil == 0
            out_specs.append(pl.BlockSpec((None, dil, tm // dil, tn), lambda i, j: (i // tiles, 0, i % tiles, j)))
            out_shapes.append(jax.ShapeDtypeStruct((t // seq, dil, seq // dil, ncols), out_dtype))
    scratch = [pltpu.VMEM((tn // LANES, tm, LANES), F32)] if any(dil > 1 for dil in dils) else []
    outs = pl.pallas_call(
        functools.partial(_mm_kernel, n_a=len(a_list), mode=mode, dils=tuple(dils), w_t=w_t),
        grid=(t // tm, ncols // tn),
        in_specs=in_specs,
        out_specs=out_specs,
        out_shape=out_shapes,
        scratch_shapes=scratch,
        compiler_params=_params("parallel", "arbitrary"),
        name=name,
    )(*args)
    return outs[0] if len(dils) == 1 else outs


def _alibi_slopes():
    n = N_BRANCH * (2048 // HEAD_DIM)
    s = np.power(np.float32(2.0), -8.0 * np.arange(1, n + 1, dtype=np.float32) / np.float32(n)).astype(np.float32)
    return s.reshape(N_BRANCH, -1)


def _band_attn_kernel(q_ref, kp_ref, kc_ref, vp_ref, vc_ref, o_ref, lse_ref, *scratch, dilation, slopes,
                      stage_heads):
    blk = BAND_BLOCK
    hg = pl.program_id(2)
    has_prev = pl.program_id(1) > 0
    n_heads = len(slopes[0])
    qi = lax.broadcasted_iota(jnp.int32, (blk, 2 * blk), 0)
    ki = lax.broadcasted_iota(jnp.int32, (blk, 2 * blk), 1)
    rel = qi + blk - ki
    valid = jnp.logical_and(jnp.logical_and(rel >= 0, rel <= blk), jnp.logical_or(ki >= blk, has_prev))
    dist = (rel * dilation).astype(F32)
    lane = lax.broadcasted_iota(jnp.int32, (blk, LANES), 1)
    scale = HEAD_DIM ** -0.5
    contract_last = (((1,), (1,)), ((), ()))
    head_cols = [slice(h * HEAD_DIM, (h + 1) * HEAD_DIM) for h in range(n_heads)]
    head_slope = []
    for h in range(n_heads):
        slope = slopes[0][h]
        for g in range(1, len(slopes)):
            slope = jnp.where(hg == g, slopes[g][h], slope)
        head_slope.append(slope)
    for r in range(dilation):
        lse_tile = jnp.zeros((blk, LANES), F32)
        for h0 in range(0, n_heads, stage_heads):
            group = list(range(h0, min(h0 + stage_heads, n_heads)))
            scores = {h: lax.dot_general(q_ref[r, :, head_cols[h]],
                                         jnp.concatenate([kp_ref[r, :, head_cols[h]], kc_ref[r, :, head_cols[h]]],
                                                         axis=0),
                                         contract_last, preferred_element_type=F32) for h in group}
            probs, dens, maxes = {}, {}, {}
            for h in group:
                s = jnp.where(valid, scores[h] * scale - head_slope[h] * dist, -jnp.inf)
                maxes[h] = jnp.max(s, axis=-1, keepdims=True)
                p = jnp.exp(s - maxes[h])
                dens[h] = jnp.sum(p, axis=-1, keepdims=True)
                probs[h] = p.astype(BF16)
            outs = {h: jnp.dot(probs[h],
                               jnp.concatenate([vp_ref[r, :, head_cols[h]], vc_ref[r, :, head_cols[h]]], axis=0),
                               preferred_element_type=F32) for h in group}
            for h in group:
                o = outs[h] / dens[h]
                lse_h = maxes[h] + jnp.log(dens[h])
                for g in range(len(slopes)):
                    lse_tile = jnp.where(jnp.logical_and(lane == g * n_heads + h, hg == g), lse_h, lse_tile)
                if dilation == 1:
                    o_ref[:, head_cols[h]] = o
                else:
                    scratch[0][h, pl.ds(r, blk, stride=dilation), :] = o
        if dilation == 1:
            lse_slab = lse_tile
        else:
            scratch[1][pl.ds(r, blk, stride=dilation), :] = lse_tile
    if dilation > 1:
        for h in range(n_heads):
            o_ref[:, h * HEAD_DIM:(h + 1) * HEAD_DIM] = scratch[0][h]
        lse_slab = scratch[1][...]

    @pl.when(hg == 0)
    def _():
        lse_ref[...] = lse_slab

    @pl.when(hg > 0)
    def _():
        lse_ref[...] += lse_slab


def _band_attention(qk, v, bsz, seq, branch, slopes):
    _, dilation = A_BRANCHES[branch]
    width = v.shape[-1]
    n_blk = seq // dilation // BAND_BLOCK
    hw = max(width * 4 // dilation, 512) if dilation > 1 else width
    hw = min(hw, width)
    n_hg = width // hw
    heads = hw // HEAD_DIM
    slope_tab = tuple(tuple(float(s) for s in slopes[g * heads:(g + 1) * heads]) for g in range(n_hg))
    rows = BAND_BLOCK * dilation
    blk = (None, dilation, BAND_BLOCK, hw)

    def prev(n):
        return jnp.maximum(n - 1, 0)

    scratch = []
    if dilation > 1:
        scratch = [pltpu.VMEM((heads, rows, HEAD_DIM), F32), pltpu.VMEM((rows, LANES), F32)]
    return pl.pallas_call(
        functools.partial(_band_attn_kernel, dilation=dilation, slopes=slope_tab,
                          stage_heads=ATTN_STAGE_HEADS[dilation]),
        grid=(bsz, n_blk, n_hg),
        in_specs=[pl.BlockSpec(blk, lambda b, n, g: (b, 0, n, g)),
                  pl.BlockSpec(blk, lambda b, n, g: (b, 0, prev(n), n_hg + g)),
                  pl.BlockSpec(blk, lambda b, n, g: (b, 0, n, n_hg + g)),
                  pl.BlockSpec(blk, lambda b, n, g: (b, 0, prev(n), g)),
                  pl.BlockSpec(blk, lambda b, n, g: (b, 0, n, g))],
        out_specs=[pl.BlockSpec((rows, hw), lambda b, n, g: (b * n_blk + n, g)),
                   pl.BlockSpec((rows, LANES), lambda b, n, g: (b * n_blk + n, 0))],
        out_shape=[jax.ShapeDtypeStruct((bsz * seq, width), F32),
                   jax.ShapeDtypeStruct((bsz * seq, LANES), F32)],
        scratch_shapes=scratch,
        compiler_params=_params("parallel", "arbitrary", "arbitrary"),
        name=f"band_attn_d{dilation}",
    )(qk, qk, qk, v, v)


def _combine_kernel(o0_ref, o1_ref, o2_ref, l0_ref, l1_ref, l2_ref, out_ref):
    l0, l1, l2 = l0_ref[...], l1_ref[...], l2_ref[...]
    m = jnp.maximum(jnp.maximum(l0, l1), l2)
    e0, e1, e2 = jnp.exp(l0 - m), jnp.exp(l1 - m), jnp.exp(l2 - m)
    tot = e0 + e1 + e2
    w0, w1, w2 = e0 / tot, e1 / tot, e2 / tot
    for h in range(out_ref.shape[1] // HEAD_DIM):
        sl = slice(h * HEAD_DIM, (h + 1) * HEAD_DIM)
        acc = w0[:, h:h + 1] * o0_ref[:, sl] + w1[:, h:h + 1] * o1_ref[:, sl] + w2[:, h:h + 1] * o2_ref[:, sl]
        out_ref[:, sl] = acc.astype(out_ref.dtype)


def _combine_branches(outs, lses, tm=512):
    t, width = outs[0].shape
    o_spec = pl.BlockSpec((tm, width), lambda i: (i, 0))
    l_spec = pl.BlockSpec((tm, LANES), lambda i: (i, 0))
    return pl.pallas_call(
        _combine_kernel,
        grid=(t // tm,),
        in_specs=[o_spec] * 3 + [l_spec] * 3,
        out_specs=o_spec,
        out_shape=jax.ShapeDtypeStruct((t, width), BF16),
        compiler_params=_params("parallel"),
        name="combine_branches",
    )(*outs, *lses)


def _pool_kernel(u_ref, uh_ref, w_ref, sc_ref, o_ref, wb_ref, *, blocks_per_seq):
    i = pl.program_id(0)

    @pl.when(i == 0)
    def _():
        wb_ref[...] = w_ref[...].astype(BF16)

    tm = u_ref.shape[0]
    group = w_ref.shape[1]
    first = (i % blocks_per_seq) == 0
    row = lax.broadcasted_iota(jnp.int32, (tm, 1), 0)
    pos = (i % blocks_per_seq) * tm + row
    for g, win in enumerate(POOL_WINDOWS):
        sl = slice(g * group, (g + 1) * group)
        u = u_ref[:, sl]
        halo = jnp.where(first, 0.0, uh_ref[:, sl])
        s = jnp.concatenate([halo, u], axis=0)
        step = 1
        while step < win:
            s = s + pltpu.roll(s, step, axis=0)
            step *= 2
        count = jnp.minimum(pos + 1, win).astype(F32)
        pooled = s[POOL_HALO:] / count - u
        y = jnp.dot(pooled.astype(BF16), wb_ref[g], preferred_element_type=F32)
        o_ref[:, sl] = (y * sc_ref[:, sl]).astype(o_ref.dtype)


def _pool_mixer(u, pool_w, pool_scale, seq, tm=512):
    t, width = u.shape
    n_group, group, _ = pool_w.shape
    blocks_per_seq = seq // tm
    halo_blocks = tm // POOL_HALO
    return pl.pallas_call(
        functools.partial(_pool_kernel, blocks_per_seq=blocks_per_seq),
        grid=(t // tm,),
        in_specs=[pl.BlockSpec((tm, width), lambda i: (i, 0)),
                  pl.BlockSpec((POOL_HALO, width), lambda i: (jnp.maximum(i * halo_blocks - 1, 0), 0)),
                  pl.BlockSpec((n_group, group, group), lambda i: (0, 0, 0)),
                  pl.BlockSpec((1, width), lambda i: (0, 0))],
        out_specs=pl.BlockSpec((tm, width), lambda i: (i, 0)),
        out_shape=jax.ShapeDtypeStruct((t, width), BF16),
        scratch_shapes=[pltpu.VMEM((n_group, group, group), BF16)],
        compiler_params=_params("arbitrary"),
        name="pool_mixer",
    )(u, u, pool_w, pool_scale.reshape(1, width))


def _split3(x):
    hi = x.astype(BF16)
    r1 = x - hi.astype(F32)
    mid = r1.astype(BF16)
    lo = (r1 - mid.astype(F32)).astype(BF16)
    return hi, mid, lo


def _gate_kernel(a_ref, wr_ref, wa2_ref, ba_ref, o_ref):
    r = lax.dot_general(a_ref[...], wr_ref[...].astype(BF16), (((1,), (1,)), ((), ())),
                        preferred_element_type=F32)
    r_hi, r_mid, _ = _split3(r)
    w_hi, w_mid, _ = _split3(wa2_ref[...])
    g = (jnp.dot(r_hi, w_hi, preferred_element_type=F32)
         + (jnp.dot(r_hi, w_mid, preferred_element_type=F32) + jnp.dot(r_mid, w_hi, preferred_element_type=F32)))
    g = g + ba_ref[...]
    log_sig = jnp.minimum(g, 0.0) - jnp.log1p(jnp.exp(-jnp.abs(g)))
    o_ref[...] = log_sig / C_GATE_TAU


def _gla_log_decay(hn, w_r, w_a2, b_a, tm=512):
    t, d = hn.shape
    kw = w_a2.shape[1]
    return pl.pallas_call(
        _gate_kernel,
        grid=(t // tm,),
        in_specs=[pl.BlockSpec((tm, d), lambda i: (i, 0)),
                  pl.BlockSpec((LANES, d), lambda i: (0, 0)),
                  pl.BlockSpec((LANES, kw), lambda i: (0, 0)),
                  pl.BlockSpec((1, kw), lambda i: (0, 0))],
        out_specs=pl.BlockSpec((tm, kw), lambda i: (i, 0)),
        out_shape=jax.ShapeDtypeStruct((t, kw), F32),
        compiler_params=_params("parallel"),
        name="gla_log_decay",
    )(hn, w_r, w_a2, b_a.reshape(1, kw))


def _gla_kernel(q_ref, k_ref, v_ref, la_ref, gate_ref, og_ref, o_ref, state_ref, *, chunks, heads):
    @pl.when(pl.program_id(2) == 0)
    def _():
        state_ref[...] = jnp.zeros_like(state_ref)

    c = C_CHUNK
    dk = q_ref.shape[1] // heads
    dv = v_ref.shape[1] // heads
    ri = lax.broadcasted_iota(jnp.int32, (c, c), 0)
    ci = lax.broadcasted_iota(jnp.int32, (c, c), 1)
    causal = ci <= ri
    tri = causal.astype(BF16)
    contract_last = (((1,), (1,)), ((), ()))
    contract_first = (((0,), (0,)), ((), ()))
    pairs = [(n, h) for n in range(chunks) for h in range(heads)]

    def rows(n):
        return slice(n * c, (n + 1) * c)

    def kcols(h):
        return slice(h * dk, (h + 1) * dk)

    def vcols(h):
        return slice(h * dv, (h + 1) * dv)

    bc = {}
    for n, h in pairs:
        la_hi, la_mid, la_lo = _split3(la_ref[rows(n), kcols(h)])
        bc[n, h] = (jnp.dot(tri, la_hi, preferred_element_type=F32)
                    + jnp.dot(tri, la_mid, preferred_element_type=F32)
                    + jnp.dot(tri, la_lo, preferred_element_type=F32))
    q_b, k_in, k_st, decay = {}, {}, {}, {}
    for n, h in pairs:
        b = bc[n, h]
        b_last = b[c - 1:c, :]
        q_b[n, h] = (q_ref[rows(n), kcols(h)] * (dk ** -0.5) * jnp.exp(b)).astype(BF16)
        k = k_ref[rows(n), kcols(h)]
        k_in[n, h] = (k * jnp.exp(-b)).astype(BF16)
        k_st[n, h] = (k * jnp.exp(b_last - b)).astype(BF16)
        decay[n, h] = jnp.exp(b_last)
    att = {}
    for p in pairs:
        a = lax.dot_general(q_b[p], k_in[p], contract_last, preferred_element_type=F32)
        att[p] = jnp.where(causal, a, 0.0).astype(BF16)
    o_intra, upd = {}, {}
    for n, h in pairs:
        v = v_ref[rows(n), vcols(h)]
        o_intra[n, h] = jnp.dot(att[n, h], v, preferred_element_type=F32)
        upd[n, h] = lax.dot_general(v, k_st[n, h], contract_first, preferred_element_type=F32)
    for h in range(heads):
        state = state_ref[h]
        for n in range(chunks):
            o = o_intra[n, h] + lax.dot_general(q_b[n, h], state.astype(BF16), contract_last,
                                                preferred_element_type=F32)
            state = state * decay[n, h] + upd[n, h]
            ms = jnp.mean(o * o, axis=-1, keepdims=True)
            gate = gate_ref[rows(n), vcols(h)]
            y = o * lax.rsqrt(ms + EPS) * og_ref[...] * (gate * jax.nn.sigmoid(gate))
            o_ref[rows(n), vcols(h)] = y.astype(o_ref.dtype)
        state_ref[h] = state


def _gla(qk, v, gate, log_a, o_gain, bsz, seq, rows=256, heads=2):
    t = bsz * seq
    dk = log_a.shape[1] // C_HEADS
    dv = v.shape[1] // C_HEADS
    steps = seq // rows
    groups = C_HEADS // heads

    def row_block(b, n):
        return b * steps + n

    return pl.pallas_call(
        functools.partial(_gla_kernel, chunks=rows // C_CHUNK, heads=heads),
        grid=(bsz, groups, steps),
        in_specs=[pl.BlockSpec((rows, heads * dk), lambda b, g, n: (row_block(b, n), g)),
                  pl.BlockSpec((rows, heads * dk), lambda b, g, n: (row_block(b, n), groups + g)),
                  pl.BlockSpec((rows, heads * dv), lambda b, g, n: (row_block(b, n), g)),
                  pl.BlockSpec((rows, heads * dk), lambda b, g, n: (row_block(b, n), g)),
                  pl.BlockSpec((rows, heads * dv), lambda b, g, n: (row_block(b, n), g)),
                  pl.BlockSpec((1, dv), lambda b, g, n: (0, 0))],
        out_specs=pl.BlockSpec((rows, heads * dv), lambda b, g, n: (row_block(b, n), g)),
        out_shape=jax.ShapeDtypeStruct((t, v.shape[1]), BF16),
        scratch_shapes=[pltpu.VMEM((heads, dv, dk), F32)],
        compiler_params=_params("parallel", "parallel", "arbitrary"),
        name="gla",
    )(qk, qk, v, log_a, gate, o_gain.reshape(1, dv))


def _xattn_kernel(h_ref, g_ref, wq_ref, qg_ref, k_ref, v_ref, wo_ref, fg_ref, o_ref, on_ref):
    x = h_ref[...]
    ms = jnp.mean(x * x, axis=-1, keepdims=True)
    hn = (x * lax.rsqrt(ms + EPS) * g_ref[...]).astype(BF16)
    q = jnp.dot(hn, wq_ref[...], preferred_element_type=F32)
    scale = HEAD_DIM ** -0.5
    contract_last = (((1,), (1,)), ((), ()))
    cols = [slice(hd * HEAD_DIM, (hd + 1) * HEAD_DIM) for hd in range(X_HEADS)]
    q_heads = []
    for sl in cols:
        qh = q[:, sl]
        qms = jnp.mean(qh * qh, axis=-1, keepdims=True)
        q_heads.append((qh * lax.rsqrt(qms + EPS) * qg_ref[...]).astype(BF16))
    scores = [lax.dot_general(qh, k_ref[0, :, sl], contract_last, preferred_element_type=F32) * scale
              for qh, sl in zip(q_heads, cols)]
    probs = []
    for s in scores:
        m = jnp.max(s, axis=-1, keepdims=True)
        p = jnp.exp(s - m)
        probs.append((p / jnp.sum(p, axis=-1, keepdims=True)).astype(BF16))
    heads = [jnp.dot(p, v_ref[0, :, sl], preferred_element_type=F32) for p, sl in zip(probs, cols)]
    o = jnp.concatenate(heads, axis=-1).astype(BF16)
    y = x + jnp.dot(o, wo_ref[...], preferred_element_type=F32)
    o_ref[...] = y
    yms = jnp.mean(y * y, axis=-1, keepdims=True)
    on_ref[...] = (y * lax.rsqrt(yms + EPS) * fg_ref[...]).astype(on_ref.dtype)


def _cross_attention(h, norm_gain, wq, q_gain, k, v, wo, next_gain, seq, tm=256):
    t, d = h.shape
    xw = wq.shape[1]
    mlen = k.shape[1]
    tiles_per_seq = seq // tm
    return pl.pallas_call(
        _xattn_kernel,
        grid=(t // tm,),
        in_specs=[pl.BlockSpec((tm, d), lambda i: (i, 0)),
                  pl.BlockSpec((1, d), lambda i: (0, 0)),
                  pl.BlockSpec((d, xw), lambda i: (0, 0)),
                  pl.BlockSpec((1, HEAD_DIM), lambda i: (0, 0)),
                  pl.BlockSpec((1, mlen, xw), lambda i: (i // tiles_per_seq, 0, 0)),
                  pl.BlockSpec((1, mlen, xw), lambda i: (i // tiles_per_seq, 0, 0)),
                  pl.BlockSpec((xw, d), lambda i: (0, 0)),
                  pl.BlockSpec((1, d), lambda i: (0, 0))],
        out_specs=[pl.BlockSpec((tm, d), lambda i: (i, 0)),
                   pl.BlockSpec((tm, d), lambda i: (i, 0))],
        out_shape=[jax.ShapeDtypeStruct((t, d), F32),
                   jax.ShapeDtypeStruct((t, d), BF16)],
        compiler_params=_params("parallel"),
        name="cross_attention",
    )(h, norm_gain.reshape(1, d), wq, q_gain.reshape(1, HEAD_DIM), k, v, wo, next_gain.reshape(1, d))


def _ffn_up_kernel(a_ref, ah_ref, wg_ref, wv_ref, cwg_ref, cwv_ref, cbg_ref, cbv_ref, wd_ref, o_ref, wdo_ref, *,
                   blocks_per_seq):
    wdo_ref[...] = wd_ref[...].astype(wdo_ref.dtype)
    tm = a_ref.shape[0]
    first = (pl.program_id(0) % blocks_per_seq) == 0
    row = lax.broadcasted_iota(jnp.int32, (tm, 1), 0)
    a = a_ref[...]
    ah = ah_ref[...]

    def conv_half(w_ref, cw_ref, cb_ref):
        w = w_ref[...].astype(BF16)
        u = jnp.dot(a, w, preferred_element_type=F32)
        uh = jnp.where(first, 0.0, jnp.dot(ah, w, preferred_element_type=F32))
        u1 = jnp.where(row == 0, uh[CONV_HALO - 1:CONV_HALO], pltpu.roll(u, 1, axis=0))
        u2 = jnp.where(row == 0, uh[CONV_HALO - 2:CONV_HALO - 1],
                       jnp.where(row == 1, uh[CONV_HALO - 1:CONV_HALO], pltpu.roll(u, 2, axis=0)))
        cw = cw_ref[...]
        return cb_ref[...] + cw[0:1] * u2 + cw[1:2] * u1 + cw[2:3] * u

    cg = conv_half(wg_ref, cwg_ref, cbg_ref)
    cv = conv_half(wv_ref, cwv_ref, cbv_ref)
    o_ref[...] = (cg * jax.nn.sigmoid(cg) * cv).astype(o_ref.dtype)


def _ffn_up(hn, w_up, conv_w, conv_b, w_down, layer, seq, tm=FFN_ROWS, tn=256):
    t, d = hn.shape
    d_ff = w_up.shape[2] // 2
    n_tiles = d_ff // tn
    n_steps = (t // tm) * n_tiles
    slab = d_ff // n_steps
    assert d_ff % n_steps == 0 and slab % (2 * SUBLANES) == 0
    blocks_per_seq = seq // tm
    halo_blocks = tm // CONV_HALO
    cw = conv_w[layer]
    cb = conv_b[layer].reshape(1, 2 * d_ff)
    return pl.pallas_call(
        functools.partial(_ffn_up_kernel, blocks_per_seq=blocks_per_seq),
        grid=(t // tm, n_tiles),
        in_specs=[pl.BlockSpec((tm, d), lambda i, j: (i, 0), pipeline_mode=pl.Buffered(1)),
                  pl.BlockSpec((CONV_HALO, d), lambda i, j: (jnp.maximum(i * halo_blocks - 1, 0), 0)),
                  pl.BlockSpec((None, d, tn), lambda i, j: (layer, 0, j)),
                  pl.BlockSpec((None, d, tn), lambda i, j: (layer, 0, n_tiles + j)),
                  pl.BlockSpec((CONV_WIDTH, tn), lambda i, j: (0, j)),
                  pl.BlockSpec((CONV_WIDTH, tn), lambda i, j: (0, n_tiles + j)),
                  pl.BlockSpec((1, tn), lambda i, j: (0, j)),
                  pl.BlockSpec((1, tn), lambda i, j: (0, n_tiles + j)),
                  pl.BlockSpec((None, slab, d), lambda i, j: (layer, i * n_tiles + j, 0))],
        out_specs=[pl.BlockSpec((tm, tn), lambda i, j: (i, j)),
                   pl.BlockSpec((slab, d), lambda i, j: (i * n_tiles + j, 0))],
        out_shape=[jax.ShapeDtypeStruct((t, d_ff), BF16),
                   jax.ShapeDtypeStruct((d_ff, d), BF16)],
        compiler_params=_params("parallel", "arbitrary"),
        name="ffn_up",
    )(hn, hn, w_up, w_up, cw, cw, cb, cb, w_down)


def _dilated_pool_layer(h, hn, w_in, q_gain, k_gain, pool_w, pool_scale, w_out, j, bsz, seq):
    d = h.shape[1]
    a_width = d // 2
    heads = a_width // HEAD_DIM
    tn = 512
    q_blocks = a_width // tn
    k_base = N_BRANCH * q_blocks
    v_cols = 2 * N_BRANCH * a_width
    gains = jnp.concatenate([jnp.tile(q_gain, heads), jnp.tile(k_gain, heads)]).reshape(1, 2 * a_width)
    dils = tuple(dil for _, dil in A_BRANCHES)
    v_list = _mm([hn], w_in, j, lambda c: v_cols // tn + c, a_width, BF16, dils=dils, seq=seq, tn=tn,
                 name="ab_in_v")
    u = _mm([hn], w_in, j, lambda c: (v_cols + a_width) // tn + c, d - a_width, F32, tn=tn, name="ab_in_u")
    slopes = _alibi_slopes()
    outs, lses = [], []
    for g, dil in enumerate(dils):
        def wcol(c, g=g):
            return jnp.where(c < q_blocks, g * q_blocks + c, k_base + g * q_blocks + c - q_blocks)
        qk = _mm([hn], w_in, j, wcol, 2 * a_width, BF16, mode="headnorm", extra=gains, dils=(dil,), seq=seq,
                 tn=tn, name=f"ab_in_qk{g}")
        if dil == 1:
            qk = qk.reshape(bsz, 1, seq, 2 * a_width)
            v_g = v_list[g].reshape(bsz, 1, seq, a_width)
        else:
            v_g = v_list[g]
        o, lse = _band_attention(qk, v_g, bsz, seq, g, slopes[g])
        outs.append(o)
        lses.append(lse)
    a_out = _combine_branches(outs, lses)
    b_out = _pool_mixer(u, pool_w, pool_scale, seq)
    return _mm([a_out, b_out], w_out, j, lambda c: c, d, F32, mode="residual", extra=h, name="ab_out")


def _gla_layer(h, hn, w_in, w_a2, b_a, o_gain, w_out, j, bsz, seq):
    d = h.shape[1]
    kw = w_a2.shape[2]
    vw = w_out.shape[1]
    tn = 512
    w_in_t = jnp.swapaxes(w_in, 1, 2)
    qk = _mm([hn], w_in_t, j, lambda c: c, 2 * kw, F32, w_t=True, tn=tn, name="c_in_qk")
    v = _mm([hn], w_in_t, j, lambda c: 2 * kw // tn + c, vw, BF16, w_t=True, tn=tn, name="c_in_v")
    gate = _mm([hn], w_in_t, j, lambda c: (2 * kw + vw) // tn + c, vw, F32, w_t=True, tn=tn, name="c_in_gate")
    rank = w_a2.shape[1]
    w_r = jnp.pad(w_in_t[j, 2 * kw + 2 * vw:, :], ((0, LANES - rank), (0, 0)))
    w_a2p = jnp.pad(w_a2[j], ((0, LANES - rank), (0, 0)))
    log_a = _gla_log_decay(hn, w_r, w_a2p, b_a[j])
    o = _gla(qk, v, gate, log_a, o_gain[j], bsz, seq)
    return _mm([o], w_out, j, lambda c: c, d, F32, mode="residual", extra=h, name="c_out")


def _memory_kv(mem, gain, wkv, k_gain, layer, bsz):
    xw = wkv.shape[2] // 2
    tn = 256
    mem_n = _rmsnorm(mem, gain)
    rows = mem.shape[0]
    gains = jnp.tile(k_gain, xw // HEAD_DIM).reshape(1, xw)
    k = _mm([mem_n], wkv, layer, lambda c: c, xw, BF16, mode="headnorm", extra=gains, tm=rows, tn=tn, name="mem_k")
    v = _mm([mem_n], wkv, layer, lambda c: xw // tn + c, xw, BF16, tm=rows, tn=tn, name="mem_v")
    return k.reshape(bsz, rows // bsz, xw), v.reshape(bsz, rows // bsz, xw)


def kernel(x, mem, mix_norm, ab_w_in, ab_q_norm, ab_k_norm, ab_pool_w, ab_pool_scale, ab_w_out, c_w_in, c_w_a2, c_b_a, c_o_norm, c_w_out, x_norm, x_mem_norm, x_wq, x_wkv, x_q_norm, x_k_norm, x_wo, f_norm, f_w_up, f_conv_w, f_conv_b, f_w_down):
    bsz, seq, d = x.shape
    depth = mix_norm.shape[0]
    h = x.reshape(bsz * seq, d)
    mem2 = mem.reshape(bsz * mem.shape[1], d)
    for layer in range(depth):
        j = layer // 2
        hn = _rmsnorm(h, mix_norm[layer])
        if layer % 2 == 0:
            h = _dilated_pool_layer(h, hn, ab_w_in, ab_q_norm[j], ab_k_norm[j], ab_pool_w[j], ab_pool_scale[j],
                                    ab_w_out, j, bsz, seq)
        else:
            h = _gla_layer(h, hn, c_w_in, c_w_a2, c_b_a, c_o_norm, c_w_out, j, bsz, seq)
        k, v = _memory_kv(mem2, x_mem_norm[layer], x_wkv, x_k_norm[layer], layer, bsz)
        h, hn = _cross_attention(h, x_norm[layer], x_wq[layer].astype(BF16), x_q_norm[layer], k, v,
                                 x_wo[layer].astype(BF16), f_norm[layer], seq)
        act, w_down = _ffn_up(hn, f_w_up, f_conv_w, f_conv_b, f_w_down, layer, seq)
        h = _mm([act], w_down, None, lambda c: c, d, F32, mode="residual", extra=h, tm=512, name="ffn_down")
    return h.reshape(bsz, seq, d)
```

```python
import functools

import numpy as np
import jax
import jax.numpy as jnp
from jax import lax
from jax.experimental import pallas as pl
from jax.experimental.pallas import tpu as pltpu

F32 = jnp.float32
BF16 = jnp.bfloat16

LANES = 128
SUBLANES = 8
VMEM_LIMIT_BYTES = 56 * 2 ** 20

EPS = 1e-6
HEAD_DIM = 128
A_BRANCHES = ((128, 1), (512, 4), (2048, 16))
N_BRANCH = len(A_BRANCHES)
BAND_BLOCK = 128
POOL_WINDOWS = (2, 4, 8, 16)
POOL_HALO = 16
C_HEADS = 8
C_GATE_RANK = 16
C_GATE_TAU = 16.0
C_CHUNK = 64
X_HEADS = 4
CONV_WIDTH = 3
CONV_HALO = SUBLANES
MM_ROWS = 1024
ATTN_STAGE_HEADS = {1: 8, 4: 16, 16: 1}
ATTN_PAIRS_PER_STEP = 64
FFN_ROWS = 2048


def _params(*semantics):
    return pltpu.CompilerParams(dimension_semantics=semantics, vmem_limit_bytes=VMEM_LIMIT_BYTES)


def _lane_groups(width):
    return [slice(c * LANES, (c + 1) * LANES) for c in range(width // LANES)]


def _rmsnorm_kernel(x_ref, g_ref, o_ref):
    x = x_ref[...].astype(F32)
    ms = jnp.mean(x * x, axis=-1, keepdims=True)
    o_ref[...] = (x * lax.rsqrt(ms + EPS) * g_ref[...]).astype(o_ref.dtype)


def _rmsnorm(x, gain, tm=256):
    t, d = x.shape
    return pl.pallas_call(
        _rmsnorm_kernel,
        grid=(t // tm,),
        in_specs=[pl.BlockSpec((tm, d), lambda i: (i, 0)),
                  pl.BlockSpec((1, d), lambda i: (0, 0))],
        out_specs=pl.BlockSpec((tm, d), lambda i: (i, 0)),
        out_shape=jax.ShapeDtypeStruct((t, d), BF16),
        compiler_params=_params("parallel"),
        name="rmsnorm",
    )(x, gain.reshape(1, d))


def _mm_kernel(*refs, n_a, mode, dils, w_t):
    a_refs = refs[:n_a]
    w_ref = refs[n_a]
    n_extra = 1 if mode in ("headnorm", "residual") else 0
    extra = refs[n_a + 1:n_a + 1 + n_extra]
    o_refs = refs[n_a + 1 + n_extra:n_a + 1 + n_extra + len(dils)]
    scratch = refs[n_a + 1 + n_extra + len(dils):]
    acc = None
    k0 = 0
    for a_ref in a_refs:
        kk = a_ref.shape[1]
        if w_t:
            part = lax.dot_general(a_ref[...], w_ref[:, k0:k0 + kk].astype(BF16), (((1,), (1,)), ((), ())),
                                   preferred_element_type=F32)
        else:
            part = jnp.dot(a_ref[...], w_ref[k0:k0 + kk, :].astype(BF16), preferred_element_type=F32)
        acc = part if acc is None else acc + part
        k0 += kk
    tm, tn = acc.shape
    for c, sl in enumerate(_lane_groups(tn)):
        blk = acc[:, sl]
        if mode == "headnorm":
            ms = jnp.mean(blk * blk, axis=-1, keepdims=True)
            blk = blk * lax.rsqrt(ms + EPS) * extra[0][:, sl]
        elif mode == "residual":
            blk = extra[0][:, sl] + blk
        if scratch:
            scratch[0][c] = blk
        for o_ref, dil in zip(o_refs, dils):
            if dil == 1:
                o_ref[:, sl] = blk.astype(o_ref.dtype)
    for o_ref, dil in zip(o_refs, dils):
        if dil > 1:
            for r in range(dil):
                for c, sl in enumerate(_lane_groups(tn)):
                    o_ref[r, :, sl] = scratch[0][c, pl.ds(r, tm // dil, stride=dil), :].astype(o_ref.dtype)


def _mm(a_list, w, layer, wcol, ncols, out_dtype, *, mode="plain", extra=None, dils=(1,), seq=None,
        w_t=False, tm=MM_ROWS, tn=512, name="mm"):
    t = a_list[0].shape[0]
    k_total = sum(a.shape[1] for a in a_list)
    assert w.shape[-1 if w_t else -2] == k_total and ncols % tn == 0 and t % tm == 0
    in_specs = [pl.BlockSpec((tm, a.shape[1]), lambda i, j: (i, 0)) for a in a_list]
    if w_t:
        in_specs.append(pl.BlockSpec((None, tn, k_total), lambda i, j: (layer, wcol(j), 0)))
    elif w.ndim == 3:
        in_specs.append(pl.BlockSpec((None, k_total, tn), lambda i, j: (layer, 0, wcol(j))))
    else:
        in_specs.append(pl.BlockSpec((k_total, tn), lambda i, j: (0, wcol(j))))
    args = list(a_list) + [w]
    if mode == "headnorm":
        in_specs.append(pl.BlockSpec((1, tn), lambda i, j: (0, j)))
        args.append(extra)
    elif mode == "residual":
        in_specs.append(pl.BlockSpec((tm, tn), lambda i, j: (i, j)))
        args.append(extra)
    out_specs, out_shapes = [], []
    for dil in dils:
        if dil == 1:
            out_specs.append(pl.BlockSpec((tm, tn), lambda i, j: (i, j)))
            out_shapes.append(jax.ShapeDtypeStruct((t, ncols), out_dtype))
        else:
            tiles = seq // tm
            assert seq % tm == 0 and tm % dil == 0
            out_specs.append(pl.BlockSpec((None, dil, tm // dil, tn), lambda i, j: (i // tiles, 0, i % tiles, j)))
            out_shapes.append(jax.ShapeDtypeStruct((t // seq, dil, seq // dil, ncols), out_dtype))
    scratch = [pltpu.VMEM((tn // LANES, tm, LANES), F32)] if any(dil > 1 for dil in dils) else []
    outs = pl.pallas_call(
        functools.partial(_mm_kernel, n_a=len(a_list), mode=mode, dils=tuple(dils), w_t=w_t),
        grid=(t // tm, ncols // tn),
        in_specs=in_specs,
        out_specs=out_specs,
        out_shape=out_shapes,
        scratch_shapes=scratch,
        compiler_params=_params("parallel", "arbitrary"),
        name=name,
    )(*args)
    return outs[0] if len(dils) == 1 else outs


def _alibi_slopes(heads):
    n = N_BRANCH * heads
    s = np.power(np.float32(2.0), -8.0 * np.arange(1, n + 1, dtype=np.float32) / np.float32(n)).astype(np.float32)
    return s.reshape(N_BRANCH, -1)


def _band_attn_kernel(q_ref, kp_ref, kc_ref, vp_ref, vc_ref, o_ref, lse_ref, *scratch, dilation, slopes,
                      stage_heads):
    blk = BAND_BLOCK
    hg = pl.program_id(2)
    has_prev = pl.program_id(1) > 0
    n_heads = len(slopes[0])
    qi = lax.broadcasted_iota(jnp.int32, (blk, 2 * blk), 0)
    ki = lax.broadcasted_iota(jnp.int32, (blk, 2 * blk), 1)
    rel = qi + blk - ki
    valid = jnp.logical_and(jnp.logical_and(rel >= 0, rel <= blk), jnp.logical_or(ki >= blk, has_prev))
    dist = (rel * dilation).astype(F32)
    lane = lax.broadcasted_iota(jnp.int32, (blk, LANES), 1)
    scale = HEAD_DIM ** -0.5
    contract_last = (((1,), (1,)), ((), ()))
    head_cols = [slice(h * HEAD_DIM, (h + 1) * HEAD_DIM) for h in range(n_heads)]
    head_slope = []
    for h in range(n_heads):
        slope = slopes[0][h]
        for g in range(1, len(slopes)):
            slope = jnp.where(hg == g, slopes[g][h], slope)
        head_slope.append(slope)
    for r in range(dilation):
        lse_tile = jnp.zeros((blk, LANES), F32)
        for h0 in range(0, n_heads, stage_heads):
            group = list(range(h0, min(h0 + stage_heads, n_heads)))
            scores = {h: lax.dot_general(q_ref[r, :, head_cols[h]],
                                         jnp.concatenate([kp_ref[r, :, head_cols[h]], kc_ref[r, :, head_cols[h]]],
                                                         axis=0),
                                         contract_last, preferred_element_type=F32) for h in group}
            probs, dens, maxes = {}, {}, {}
            for h in group:
                s = jnp.where(valid, scores[h] * scale - head_slope[h] * dist, -jnp.inf)
                maxes[h] = jnp.max(s, axis=-1, keepdims=True)
                p = jnp.exp(s - maxes[h])
                dens[h] = jnp.sum(p, axis=-1, keepdims=True)
                probs[h] = p.astype(BF16)
            outs = {h: jnp.dot(probs[h],
                               jnp.concatenate([vp_ref[r, :, head_cols[h]], vc_ref[r, :, head_cols[h]]], axis=0),
                               preferred_element_type=F32) for h in group}
            for h in group:
                o = outs[h] / dens[h]
                lse_h = maxes[h] + jnp.log(dens[h])
                for g in range(len(slopes)):
                    lse_tile = jnp.where(jnp.logical_and(lane == g * n_heads + h, hg == g), lse_h, lse_tile)
                if dilation == 1:
                    o_ref[:, head_cols[h]] = o.astype(o_ref.dtype)
                else:
                    scratch[0][h, pl.ds(r, blk, stride=dilation), :] = o
        if dilation == 1:
            lse_slab = lse_tile
        else:
            scratch[1][pl.ds(r, blk, stride=dilation), :] = lse_tile
    if dilation > 1:
        for h in range(n_heads):
            o_ref[:, h * HEAD_DIM:(h + 1) * HEAD_DIM] = scratch[0][h].astype(o_ref.dtype)
        lse_slab = scratch[1][...]

    @pl.when(hg == 0)
    def _():
        lse_ref[...] = lse_slab

    @pl.when(hg > 0)
    def _():
        lse_ref[...] += lse_slab


def _band_attention(qk, v, bsz, seq, branch, slopes):
    _, dilation = A_BRANCHES[branch]
    width = v.shape[-1]
    n_blk = seq // dilation // BAND_BLOCK
    heads = max(min(width // HEAD_DIM, ATTN_PAIRS_PER_STEP // dilation), 1)
    hw = heads * HEAD_DIM
    n_hg = width // hw
    slope_tab = tuple(tuple(float(s) for s in slopes[g * heads:(g + 1) * heads]) for g in range(n_hg))
    rows = BAND_BLOCK * dilation
    blk = (None, dilation, BAND_BLOCK, hw)

    def prev(n):
        return jnp.maximum(n - 1, 0)

    scratch = []
    if dilation > 1:
        scratch = [pltpu.VMEM((heads, rows, HEAD_DIM), F32), pltpu.VMEM((rows, LANES), F32)]
    return pl.pallas_call(
        functools.partial(_band_attn_kernel, dilation=dilation, slopes=slope_tab,
                          stage_heads=ATTN_STAGE_HEADS[dilation]),
        grid=(bsz, n_blk, n_hg),
        in_specs=[pl.BlockSpec(blk, lambda b, n, g: (b, 0, n, g)),
                  pl.BlockSpec(blk, lambda b, n, g: (b, 0, prev(n), n_hg + g)),
                  pl.BlockSpec(blk, lambda b, n, g: (b, 0, n, n_hg + g)),
                  pl.BlockSpec(blk, lambda b, n, g: (b, 0, prev(n), g)),
                  pl.BlockSpec(blk, lambda b, n, g: (b, 0, n, g))],
        out_specs=[pl.BlockSpec((rows, hw), lambda b, n, g: (b * n_blk + n, g)),
                   pl.BlockSpec((rows, LANES), lambda b, n, g: (b * n_blk + n, 0))],
        out_shape=[jax.ShapeDtypeStruct((bsz * seq, width), BF16),
                   jax.ShapeDtypeStruct((bsz * seq, LANES), F32)],
        scratch_shapes=scratch,
        compiler_params=_params("parallel", "arbitrary", "arbitrary"),
        name=f"band_attn_d{dilation}",
    )(qk, qk, qk, v, v)


def _combine_kernel(o0_ref, o1_ref, o2_ref, l0_ref, l1_ref, l2_ref, out_ref):
    l0, l1, l2 = l0_ref[...], l1_ref[...], l2_ref[...]
    m = jnp.maximum(jnp.maximum(l0, l1), l2)
    e0, e1, e2 = jnp.exp(l0 - m), jnp.exp(l1 - m), jnp.exp(l2 - m)
    tot = e0 + e1 + e2
    w0, w1, w2 = e0 / tot, e1 / tot, e2 / tot
    for h in range(out_ref.shape[1] // HEAD_DIM):
        sl = slice(h * HEAD_DIM, (h + 1) * HEAD_DIM)
        acc = (w0[:, h:h + 1] * o0_ref[:, sl].astype(F32) + w1[:, h:h + 1] * o1_ref[:, sl].astype(F32)
               + w2[:, h:h + 1] * o2_ref[:, sl].astype(F32))
        out_ref[:, sl] = acc.astype(out_ref.dtype)


def _combine_branches(outs, lses, tm=512):
    t, width = outs[0].shape
    o_spec = pl.BlockSpec((tm, width), lambda i: (i, 0))
    l_spec = pl.BlockSpec((tm, LANES), lambda i: (i, 0))
    return pl.pallas_call(
        _combine_kernel,
        grid=(t // tm,),
        in_specs=[o_spec] * 3 + [l_spec] * 3,
        out_specs=o_spec,
        out_shape=jax.ShapeDtypeStruct((t, width), BF16),
        compiler_params=_params("parallel"),
        name="combine_branches",
    )(*outs, *lses)


def _pool_kernel(u_ref, uh_ref, w_ref, sc_ref, o_ref, wb_ref, *, blocks_per_seq):
    i = pl.program_id(0)

    @pl.when(i == 0)
    def _():
        wb_ref[...] = w_ref[...].astype(BF16)

    tm = u_ref.shape[0]
    group = w_ref.shape[1]
    first = (i % blocks_per_seq) == 0
    row = lax.broadcasted_iota(jnp.int32, (tm, 1), 0)
    pos = (i % blocks_per_seq) * tm + row
    for g, win in enumerate(POOL_WINDOWS):
        sl = slice(g * group, (g + 1) * group)
        u = u_ref[:, sl]
        halo = jnp.where(first, 0.0, uh_ref[:, sl])
        s = jnp.concatenate([halo, u], axis=0)
        step = 1
        while step < win:
            s = s + pltpu.roll(s, step, axis=0)
            step *= 2
        count = jnp.minimum(pos + 1, win).astype(F32)
        pooled = s[POOL_HALO:] / count - u
        y = jnp.dot(pooled.astype(BF16), wb_ref[g], preferred_element_type=F32)
        o_ref[:, sl] = (y * sc_ref[:, sl]).astype(o_ref.dtype)


def _pool_mixer(u, pool_w, pool_scale, seq, tm=512):
    t, width = u.shape
    n_group, group, _ = pool_w.shape
    blocks_per_seq = seq // tm
    halo_blocks = tm // POOL_HALO
    return pl.pallas_call(
        functools.partial(_pool_kernel, blocks_per_seq=blocks_per_seq),
        grid=(t // tm,),
        in_specs=[pl.BlockSpec((tm, width), lambda i: (i, 0)),
                  pl.BlockSpec((POOL_HALO, width), lambda i: (jnp.maximum(i * halo_blocks - 1, 0), 0)),
                  pl.BlockSpec((n_group, group, group), lambda i: (0, 0, 0)),
                  pl.BlockSpec((1, width), lambda i: (0, 0))],
        out_specs=pl.BlockSpec((tm, width), lambda i: (i, 0)),
        out_shape=jax.ShapeDtypeStruct((t, width), BF16),
        scratch_shapes=[pltpu.VMEM((n_group, group, group), BF16)],
        compiler_params=_params("arbitrary"),
        name="pool_mixer",
    )(u, u, pool_w, pool_scale.reshape(1, width))


def _split3(x):
    hi = x.astype(BF16)
    r1 = x - hi.astype(F32)
    mid = r1.astype(BF16)
    lo = (r1 - mid.astype(F32)).astype(BF16)
    return hi, mid, lo


def _gate_kernel(h_ref, ng_ref, wr_ref, wa2_ref, ba_ref, hn_ref, o_ref):
    x = h_ref[...]
    ms = jnp.mean(x * x, axis=-1, keepdims=True)
    hn = (x * lax.rsqrt(ms + EPS) * ng_ref[...]).astype(hn_ref.dtype)
    hn_ref[...] = hn
    r = lax.dot_general(hn, wr_ref[...].astype(BF16), (((1,), (1,)), ((), ())), preferred_element_type=F32)
    r_hi, r_mid, _ = _split3(r)
    w_hi, w_mid, _ = _split3(wa2_ref[...])
    g = (jnp.dot(r_hi, w_hi, preferred_element_type=F32)
         + (jnp.dot(r_hi, w_mid, preferred_element_type=F32) + jnp.dot(r_mid, w_hi, preferred_element_type=F32)))
    g = g + ba_ref[...]
    log_sig = jnp.minimum(g, 0.0) - jnp.log1p(jnp.exp(-jnp.abs(g)))
    o_ref[...] = log_sig / C_GATE_TAU


def _gla_norm_log_decay(h, norm_gain, w_r, w_a2, b_a, tm=512):
    t, d = h.shape
    kw = w_a2.shape[1]
    return pl.pallas_call(
        _gate_kernel,
        grid=(t // tm,),
        in_specs=[pl.BlockSpec((tm, d), lambda i: (i, 0)),
                  pl.BlockSpec((1, d), lambda i: (0, 0)),
                  pl.BlockSpec((LANES, d), lambda i: (0, 0)),
                  pl.BlockSpec((LANES, kw), lambda i: (0, 0)),
                  pl.BlockSpec((1, kw), lambda i: (0, 0))],
        out_specs=[pl.BlockSpec((tm, d), lambda i: (i, 0)),
                   pl.BlockSpec((tm, kw), lambda i: (i, 0))],
        out_shape=[jax.ShapeDtypeStruct((t, d), BF16),
                   jax.ShapeDtypeStruct((t, kw), F32)],
        compiler_params=_params("parallel"),
        name="gla_norm_log_decay",
    )(h, norm_gain.reshape(1, d), w_r, w_a2, b_a.reshape(1, kw))


def _gla_kernel(q_ref, k_ref, v_ref, la_ref, gate_ref, og_ref, o_ref, state_ref, *, chunks, heads):
    @pl.when(pl.program_id(2) == 0)
    def _():
        state_ref[...] = jnp.zeros_like(state_ref)

    c = C_CHUNK
    dk = q_ref.shape[1] // heads
    dv = v_ref.shape[1] // heads
    ri = lax.broadcasted_iota(jnp.int32, (c, c), 0)
    ci = lax.broadcasted_iota(jnp.int32, (c, c), 1)
    causal = ci <= ri
    tri = causal.astype(BF16)
    contract_last = (((1,), (1,)), ((), ()))
    contract_first = (((0,), (0,)), ((), ()))
    pairs = [(n, h) for n in range(chunks) for h in range(heads)]

    def rows(n):
        return slice(n * c, (n + 1) * c)

    def kcols(h):
        return slice(h * dk, (h + 1) * dk)

    def vcols(h):
        return slice(h * dv, (h + 1) * dv)

    bc = {}
    for n, h in pairs:
        la_hi, la_mid, la_lo = _split3(la_ref[rows(n), kcols(h)])
        bc[n, h] = (jnp.dot(tri, la_hi, preferred_element_type=F32)
                    + jnp.dot(tri, la_mid, preferred_element_type=F32)
                    + jnp.dot(tri, la_lo, preferred_element_type=F32))
    q_b, k_in, k_st, decay = {}, {}, {}, {}
    for n, h in pairs:
        b = bc[n, h]
        b_last = b[c - 1:c, :]
        q_b[n, h] = (q_ref[rows(n), kcols(h)] * (dk ** -0.5) * jnp.exp(b)).astype(BF16)
        k = k_ref[rows(n), kcols(h)]
        k_in[n, h] = (k * jnp.exp(-b)).astype(BF16)
        k_st[n, h] = (k * jnp.exp(b_last - b)).astype(BF16)
        decay[n, h] = jnp.exp(b_last)
    att = {}
    for p in pairs:
        a = lax.dot_general(q_b[p], k_in[p], contract_last, preferred_element_type=F32)
        att[p] = jnp.where(causal, a, 0.0).astype(BF16)
    o_intra, upd = {}, {}
    for n, h in pairs:
        v = v_ref[rows(n), vcols(h)]
        o_intra[n, h] = jnp.dot(att[n, h], v, preferred_element_type=F32)
        upd[n, h] = lax.dot_general(v, k_st[n, h], contract_first, preferred_element_type=F32)
    for h in range(heads):
        state = state_ref[h]
        for n in range(chunks):
            o = o_intra[n, h] + lax.dot_general(q_b[n, h], state.astype(BF16), contract_last,
                                                preferred_element_type=F32)
            state = state * decay[n, h] + upd[n, h]
            ms = jnp.mean(o * o, axis=-1, keepdims=True)
            gate = gate_ref[rows(n), vcols(h)]
            y = o * lax.rsqrt(ms + EPS) * og_ref[...] * (gate * jax.nn.sigmoid(gate))
            o_ref[rows(n), vcols(h)] = y.astype(o_ref.dtype)
        state_ref[h] = state


def _gla(qk, v, gate, log_a, o_gain, bsz, seq, rows=256, heads=2):
    t = bsz * seq
    dk = log_a.shape[1] // C_HEADS
    dv = v.shape[1] // C_HEADS
    steps = seq // rows
    groups = C_HEADS // heads

    def row_block(b, n):
        return b * steps + n

    return pl.pallas_call(
        functools.partial(_gla_kernel, chunks=rows // C_CHUNK, heads=heads),
        grid=(bsz, groups, steps),
        in_specs=[pl.BlockSpec((rows, heads * dk), lambda b, g, n: (row_block(b, n), g)),
                  pl.BlockSpec((rows, heads * dk), lambda b, g, n: (row_block(b, n), groups + g)),
                  pl.BlockSpec((rows, heads * dv), lambda b, g, n: (row_block(b, n), g)),
                  pl.BlockSpec((rows, heads * dk), lambda b, g, n: (row_block(b, n), g)),
                  pl.BlockSpec((rows, heads * dv), lambda b, g, n: (row_block(b, n), g)),
                  pl.BlockSpec((1, dv), lambda b, g, n: (0, 0))],
        out_specs=pl.BlockSpec((rows, heads * dv), lambda b, g, n: (row_block(b, n), g)),
        out_shape=jax.ShapeDtypeStruct((t, v.shape[1]), BF16),
        scratch_shapes=[pltpu.VMEM((heads, dv, dk), F32)],
        compiler_params=_params("parallel", "parallel", "arbitrary"),
        name="gla",
    )(qk, qk, v, log_a, gate, o_gain.reshape(1, dv))


def _xattn_kernel(h_ref, g_ref, wq_ref, qg_ref, k_ref, v_ref, wo_ref, fg_ref, o_ref, on_ref):
    x = h_ref[...]
    ms = jnp.mean(x * x, axis=-1, keepdims=True)
    hn = (x * lax.rsqrt(ms + EPS) * g_ref[...]).astype(BF16)
    q = jnp.dot(hn, wq_ref[...], preferred_element_type=F32)
    scale = HEAD_DIM ** -0.5
    contract_last = (((1,), (1,)), ((), ()))
    cols = [slice(hd * HEAD_DIM, (hd + 1) * HEAD_DIM) for hd in range(X_HEADS)]
    q_heads = []
    for sl in cols:
        qh = q[:, sl]
        qms = jnp.mean(qh * qh, axis=-1, keepdims=True)
        q_heads.append((qh * lax.rsqrt(qms + EPS) * qg_ref[...]).astype(BF16))
    scores = [lax.dot_general(qh, k_ref[0, :, sl], contract_last, preferred_element_type=F32) * scale
              for qh, sl in zip(q_heads, cols)]
    probs = []
    for s in scores:
        m = jnp.max(s, axis=-1, keepdims=True)
        p = jnp.exp(s - m)
        probs.append((p / jnp.sum(p, axis=-1, keepdims=True)).astype(BF16))
    heads = [jnp.dot(p, v_ref[0, :, sl], preferred_element_type=F32) for p, sl in zip(probs, cols)]
    o = jnp.concatenate(heads, axis=-1).astype(BF16)
    y = x + jnp.dot(o, wo_ref[...], preferred_element_type=F32)
    o_ref[...] = y
    yms = jnp.mean(y * y, axis=-1, keepdims=True)
    on_ref[...] = (y * lax.rsqrt(yms + EPS) * fg_ref[...]).astype(on_ref.dtype)


def _cross_attention(h, norm_gain, wq, q_gain, k, v, wo, next_gain, seq, tm=256):
    t, d = h.shape
    xw = wq.shape[1]
    mlen = k.shape[1]
    tiles_per_seq = seq // tm
    return pl.pallas_call(
        _xattn_kernel,
        grid=(t // tm,),
        in_specs=[pl.BlockSpec((tm, d), lambda i: (i, 0)),
                  pl.BlockSpec((1, d), lambda i: (0, 0)),
                  pl.BlockSpec((d, xw), lambda i: (0, 0)),
                  pl.BlockSpec((1, HEAD_DIM), lambda i: (0, 0)),
                  pl.BlockSpec((1, mlen, xw), lambda i: (i // tiles_per_seq, 0, 0)),
                  pl.BlockSpec((1, mlen, xw), lambda i: (i // tiles_per_seq, 0, 0)),
                  pl.BlockSpec((xw, d), lambda i: (0, 0)),
                  pl.BlockSpec((1, d), lambda i: (0, 0))],
        out_specs=[pl.BlockSpec((tm, d), lambda i: (i, 0)),
                   pl.BlockSpec((tm, d), lambda i: (i, 0))],
        out_shape=[jax.ShapeDtypeStruct((t, d), F32),
                   jax.ShapeDtypeStruct((t, d), BF16)],
        compiler_params=_params("parallel"),
        name="cross_attention",
    )(h, norm_gain.reshape(1, d), wq, q_gain.reshape(1, HEAD_DIM), k, v, wo, next_gain.reshape(1, d))


def _ffn_up_kernel(a_ref, ah_ref, wg_ref, wv_ref, cwg_ref, cwv_ref, cbg_ref, cbv_ref, wd_ref, o_ref, wdo_ref, *,
                   blocks_per_seq):
    wdo_ref[...] = wd_ref[...].astype(wdo_ref.dtype)
    tm = a_ref.shape[0]
    first = (pl.program_id(0) % blocks_per_seq) == 0
    row = lax.broadcasted_iota(jnp.int32, (tm, 1), 0)
    a = a_ref[...]
    ah = ah_ref[...]

    def conv_half(w_ref, cw_ref, cb_ref):
        w = w_ref[...].astype(BF16)
        u = jnp.dot(a, w, preferred_element_type=F32)
        uh = jnp.where(first, 0.0, jnp.dot(ah, w, preferred_element_type=F32))
        u1 = jnp.where(row == 0, uh[CONV_HALO - 1:CONV_HALO], pltpu.roll(u, 1, axis=0))
        u2 = jnp.where(row == 0, uh[CONV_HALO - 2:CONV_HALO - 1],
                       jnp.where(row == 1, uh[CONV_HALO - 1:CONV_HALO], pltpu.roll(u, 2, axis=0)))
        cw = cw_ref[...]
        return cb_ref[...] + cw[0:1] * u2 + cw[1:2] * u1 + cw[2:3] * u

    cg = conv_half(wg_ref, cwg_ref, cbg_ref)
    cv = conv_half(wv_ref, cwv_ref, cbv_ref)
    o_ref[...] = (cg * jax.nn.sigmoid(cg) * cv).astype(o_ref.dtype)


def _ffn_up(hn, w_up, conv_w, conv_b, w_down, layer, seq, tm=FFN_ROWS, tn=256):
    t, d = hn.shape
    d_ff = w_up.shape[2] // 2
    n_tiles = d_ff // tn
    n_steps = (t // tm) * n_tiles
    slab = d_ff // n_steps
    assert d_ff % n_steps == 0 and slab % (2 * SUBLANES) == 0
    blocks_per_seq = seq // tm
    halo_blocks = tm // CONV_HALO
    cw = conv_w[layer]
    cb = conv_b[layer].reshape(1, 2 * d_ff)
    return pl.pallas_call(
        functools.partial(_ffn_up_kernel, blocks_per_seq=blocks_per_seq),
        grid=(t // tm, n_tiles),
        in_specs=[pl.BlockSpec((tm, d), lambda i, j: (i, 0), pipeline_mode=pl.Buffered(1)),
                  pl.BlockSpec((CONV_HALO, d), lambda i, j: (jnp.maximum(i * halo_blocks - 1, 0), 0)),
                  pl.BlockSpec((None, d, tn), lambda i, j: (layer, 0, j)),
                  pl.BlockSpec((None, d, tn), lambda i, j: (layer, 0, n_tiles + j)),
                  pl.BlockSpec((CONV_WIDTH, tn), lambda i, j: (0, j)),
                  pl.BlockSpec((CONV_WIDTH, tn), lambda i, j: (0, n_tiles + j)),
                  pl.BlockSpec((1, tn), lambda i, j: (0, j)),
                  pl.BlockSpec((1, tn), lambda i, j: (0, n_tiles + j)),
                  pl.BlockSpec((None, slab, d), lambda i, j: (layer, i * n_tiles + j, 0))],
        out_specs=[pl.BlockSpec((tm, tn), lambda i, j: (i, j)),
                   pl.BlockSpec((slab, d), lambda i, j: (i * n_tiles + j, 0))],
        out_shape=[jax.ShapeDtypeStruct((t, d_ff), BF16),
                   jax.ShapeDtypeStruct((d_ff, d), BF16)],
        compiler_params=_params("parallel", "arbitrary"),
        name="ffn_up",
    )(hn, hn, w_up, w_up, cw, cw, cb, cb, w_down)


def _dilated_pool_layer(h, hn, w_in, q_gain, k_gain, pool_w, pool_scale, w_out, j, bsz, seq):
    d = h.shape[1]
    a_width = d // 2
    heads = a_width // HEAD_DIM
    tn = 512
    q_blocks = a_width // tn
    k_base = N_BRANCH * q_blocks
    v_cols = 2 * N_BRANCH * a_width
    gains = jnp.concatenate([jnp.tile(q_gain, heads), jnp.tile(k_gain, heads)]).reshape(1, 2 * a_width)
    dils = tuple(dil for _, dil in A_BRANCHES)
    v_list = _mm([hn], w_in, j, lambda c: v_cols // tn + c, a_width, BF16, dils=dils, seq=seq, tn=tn,
                 name="ab_in_v")
    u = _mm([hn], w_in, j, lambda c: (v_cols + a_width) // tn + c, d - a_width, F32, tn=tn, name="ab_in_u")
    slopes = _alibi_slopes(heads)
    outs, lses = [], []
    for g, dil in enumerate(dils):
        def wcol(c, g=g):
            return jnp.where(c < q_blocks, g * q_blocks + c, k_base + g * q_blocks + c - q_blocks)
        qk = _mm([hn], w_in, j, wcol, 2 * a_width, BF16, mode="headnorm", extra=gains, dils=(dil,), seq=seq,
                 tn=tn, name=f"ab_in_qk{g}")
        if dil == 1:
            qk = qk.reshape(bsz, 1, seq, 2 * a_width)
            v_g = v_list[g].reshape(bsz, 1, seq, a_width)
        else:
            v_g = v_list[g]
        o, lse = _band_attention(qk, v_g, bsz, seq, g, slopes[g])
        outs.append(o)
        lses.append(lse)
    a_out = _combine_branches(outs, lses)
    b_out = _pool_mixer(u, pool_w, pool_scale, seq)
    return _mm([a_out, b_out], w_out, j, lambda c: c, d, F32, mode="residual", extra=h, name="ab_out")


def _gla_layer(h, norm_gain, w_in, w_a2, b_a, o_gain, w_out, j, bsz, seq):
    d = h.shape[1]
    kw = w_a2.shape[2]
    vw = w_out.shape[1]
    tn = 512
    w_in_t = jnp.swapaxes(w_in, 1, 2)
    rank = w_a2.shape[1]
    w_r = jnp.pad(w_in_t[j, 2 * kw + 2 * vw:, :], ((0, LANES - rank), (0, 0)))
    w_a2p = jnp.pad(w_a2[j], ((0, LANES - rank), (0, 0)))
    hn, log_a = _gla_norm_log_decay(h, norm_gain, w_r, w_a2p, b_a[j])
    qk = _mm([hn], w_in_t, j, lambda c: c, 2 * kw, F32, w_t=True, tn=tn, name="c_in_qk")
    v = _mm([hn], w_in_t, j, lambda c: 2 * kw // tn + c, vw, BF16, w_t=True, tn=tn, name="c_in_v")
    gate = _mm([hn], w_in_t, j, lambda c: (2 * kw + vw) // tn + c, vw, F32, w_t=True, tn=tn, name="c_in_gate")
    o = _gla(qk, v, gate, log_a, o_gain[j], bsz, seq)
    return _mm([o], w_out, j, lambda c: c, d, F32, mode="residual", extra=h, name="c_out")


def _memory_kv(mem, gain, wkv, k_gain, layer, bsz):
    xw = wkv.shape[2] // 2
    tn = 256
    mem_n = _rmsnorm(mem, gain)
    rows = mem.shape[0]
    gains = jnp.tile(k_gain, xw // HEAD_DIM).reshape(1, xw)
    k = _mm([mem_n], wkv, layer, lambda c: c, xw, BF16, mode="headnorm", extra=gains, tm=rows, tn=tn, name="mem_k")
    v = _mm([mem_n], wkv, layer, lambda c: xw // tn + c, xw, BF16, tm=rows, tn=tn, name="mem_v")
    return k.reshape(bsz, rows // bsz, xw), v.reshape(bsz, rows // bsz, xw)


def kernel(x, mem, mix_norm, ab_w_in, ab_q_norm, ab_k_norm, ab_pool_w, ab_pool_scale, ab_w_out, c_w_in, c_w_a2, c_b_a, c_o_norm, c_w_out, x_norm, x_mem_norm, x_wq, x_wkv, x_q_norm, x_k_norm, x_wo, f_norm, f_w_up, f_conv_w, f_conv_b, f_w_down):
    bsz, seq, d = x.shape
    depth = mix_norm.shape[0]
    h = x.reshape(bsz * seq, d)
    mem2 = mem.reshape(bsz * mem.shape[1], d)
    for layer in range(depth):
        j = layer // 2
        if layer % 2 == 0:
            hn = _rmsnorm(h, mix_norm[layer])
            h = _dilated_pool_layer(h, hn, ab_w_in, ab_q_norm[j], ab_k_norm[j], ab_pool_w[j], ab_pool_scale[j],
                                    ab_w_out, j, bsz, seq)
        else:
            h = _gla_layer(h, mix_norm[layer], c_w_in, c_w_a2, c_b_a, c_o_norm, c_w_out, j, bsz, seq)
        k, v = _memory_kv(mem2, x_mem_norm[layer], x_wkv, x_k_norm[layer], layer, bsz)
        h, hn = _cross_attention(h, x_norm[layer], x_wq[layer].astype(BF16), x_q_norm[layer], k, v,
                                 x_wo[layer].astype(BF16), f_norm[layer], seq)
        act, w_down = _ffn_up(hn, f_w_up, f_conv_w, f_conv_b, f_w_down, layer, seq)
        h = _mm([act], w_down, None, lambda c: c, d, F32, mode="residual", extra=h, tm=512, name="ffn_down")
    return h.reshape(bsz, seq, d)
```

```python
import functools

import numpy as np
import jax
import jax.numpy as jnp
from jax import lax
from jax.experimental import pallas as pl
from jax.experimental.pallas import tpu as pltpu

F32 = jnp.float32
BF16 = jnp.bfloat16

LANES = 128
SUBLANES = 8
VMEM_LIMIT_BYTES = 56 * 2 ** 20

EPS = 1e-6
HEAD_DIM = 128
A_BRANCHES = ((128, 1), (512, 4), (2048, 16))
N_BRANCH = len(A_BRANCHES)
BAND_BLOCK = 128
POOL_WINDOWS = (2, 4, 8, 16)
POOL_HALO = 16
C_HEADS = 8
C_GATE_RANK = 16
C_GATE_TAU = 16.0
C_CHUNK = 64
X_HEADS = 4
CONV_WIDTH = 3
CONV_HALO = SUBLANES
MM_ROWS = 1024
ATTN_STAGE_HEADS = {1: 8, 4: 16, 16: 1}
ATTN_PAIRS_PER_STEP = 64
FFN_ROWS = 2048


def _params(*semantics):
    return pltpu.CompilerParams(dimension_semantics=semantics, vmem_limit_bytes=VMEM_LIMIT_BYTES)


def _lane_groups(width):
    return [slice(c * LANES, (c + 1) * LANES) for c in range(width // LANES)]


def _rmsnorm_kernel(x_ref, g_ref, o_ref):
    x = x_ref[...].astype(F32)
    ms = jnp.mean(x * x, axis=-1, keepdims=True)
    o_ref[...] = (x * lax.rsqrt(ms + EPS) * g_ref[...]).astype(o_ref.dtype)


def _rmsnorm(x, gain, tm=256):
    t, d = x.shape
    return pl.pallas_call(
        _rmsnorm_kernel,
        grid=(t // tm,),
        in_specs=[pl.BlockSpec((tm, d), lambda i: (i, 0)),
                  pl.BlockSpec((1, d), lambda i: (0, 0))],
        out_specs=pl.BlockSpec((tm, d), lambda i: (i, 0)),
        out_shape=jax.ShapeDtypeStruct((t, d), BF16),
        compiler_params=_params("parallel"),
        name="rmsnorm",
    )(x, gain.reshape(1, d))


def _mm_kernel(*refs, n_a, mode, dils, w_t):
    a_refs = refs[:n_a]
    w_ref = refs[n_a]
    n_extra = 1 if mode in ("headnorm", "residual") else 0
    extra = refs[n_a + 1:n_a + 1 + n_extra]
    o_refs = refs[n_a + 1 + n_extra:n_a + 1 + n_extra + len(dils)]
    scratch = refs[n_a + 1 + n_extra + len(dils):]
    acc = None
    k0 = 0
    for a_ref in a_refs:
        kk = a_ref.shape[1]
        if w_t:
            part = lax.dot_general(a_ref[...], w_ref[:, k0:k0 + kk].astype(BF16), (((1,), (1,)), ((), ())),
                                   preferred_element_type=F32)
        else:
            part = jnp.dot(a_ref[...], w_ref[k0:k0 + kk, :].astype(BF16), preferred_element_type=F32)
        acc = part if acc is None else acc + part
        k0 += kk
    tm, tn = acc.shape
    for c, sl in enumerate(_lane_groups(tn)):
        blk = acc[:, sl]
        if mode == "headnorm":
            ms = jnp.mean(blk * blk, axis=-1, keepdims=True)
            blk = blk * lax.rsqrt(ms + EPS) * extra[0][:, sl]
        elif mode == "residual":
            blk = extra[0][:, sl] + blk
        if scratch:
            scratch[0][c] = blk
        for o_ref, dil in zip(o_refs, dils):
            if dil == 1:
                o_ref[:, sl] = blk.astype(o_ref.dtype)
    for o_ref, dil in zip(o_refs, dils):
        if dil > 1:
            for r in range(dil):
                for c, sl in enumerate(_lane_groups(tn)):
                    o_ref[r, :, sl] = scratch[0][c, pl.ds(r, tm // dil, stride=dil), :].astype(o_ref.dtype)


def _mm(a_list, w, layer, wcol, ncols, out_dtype, *, mode="plain", extra=None, dils=(1,), seq=None,
        w_t=False, tm=MM_ROWS, tn=512, name="mm"):
    t = a_list[0].shape[0]
    k_total = sum(a.shape[1] for a in a_list)
    assert w.shape[-1 if w_t else -2] == k_total and ncols % tn == 0 and t % tm == 0
    in_specs = [pl.BlockSpec((tm, a.shape[1]), lambda i, j: (i, 0)) for a in a_list]
    if w_t:
        in_specs.append(pl.BlockSpec((None, tn, k_total), lambda i, j: (layer, wcol(j), 0)))
    elif w.ndim == 3:
        in_specs.append(pl.BlockSpec((None, k_total, tn), lambda i, j: (layer, 0, wcol(j))))
    else:
        in_specs.append(pl.BlockSpec((k_total, tn), lambda i, j: (0, wcol(j))))
    args = list(a_list) + [w]
    if mode == "headnorm":
        in_specs.append(pl.BlockSpec((1, tn), lambda i, j: (0, j)))
        args.append(extra)
    elif mode == "residual":
        in_specs.append(pl.BlockSpec((tm, tn), lambda i, j: (i, j)))
        args.append(extra)
    out_specs, out_shapes = [], []
    for dil in dils:
        if dil == 1:
            out_specs.append(pl.BlockSpec((tm, tn), lambda i, j: (i, j)))
            out_shapes.append(jax.ShapeDtypeStruct((t, ncols), out_dtype))
        else:
            tiles = seq // tm
            assert seq % tm == 0 and tm % dil == 0
            out_specs.append(pl.BlockSpec((None, dil, tm // dil, tn), lambda i, j: (i // tiles, 0, i % tiles, j)))
            out_shapes.append(jax.ShapeDtypeStruct((t // seq, dil, seq // dil, ncols), out_dtype))
    scratch = [pltpu.VMEM((tn // LANES, tm, LANES), F32)] if any(dil > 1 for dil in dils) else []
    outs = pl.pallas_call(
        functools.partial(_mm_kernel, n_a=len(a_list), mode=mode, dils=tuple(dils), w_t=w_t),
        grid=(t // tm, ncols // tn),
        in_specs=in_specs,
        out_specs=out_specs,
        out_shape=out_shapes,
        scratch_shapes=scratch,
        compiler_params=_params("parallel", "arbitrary"),
        name=name,
    )(*args)
    return outs[0] if len(dils) == 1 else outs


def _alibi_slopes(heads):
    n = N_BRANCH * heads
    s = np.power(np.float32(2.0), -8.0 * np.arange(1, n + 1, dtype=np.float32) / np.float32(n)).astype(np.float32)
    return s.reshape(N_BRANCH, -1)


def _band_attn_kernel(q_ref, kp_ref, kc_ref, vp_ref, vc_ref, o_ref, lse_ref, *scratch, dilation, slopes,
                      stage_heads):
    blk = BAND_BLOCK
    hg = pl.program_id(2)
    has_prev = pl.program_id(1) > 0
    n_heads = len(slopes[0])
    qi = lax.broadcasted_iota(jnp.int32, (blk, 2 * blk), 0)
    ki = lax.broadcasted_iota(jnp.int32, (blk, 2 * blk), 1)
    rel = qi + blk - ki
    valid = jnp.logical_and(jnp.logical_and(rel >= 0, rel <= blk), jnp.logical_or(ki >= blk, has_prev))
    dist = (rel * dilation).astype(F32)
    lane = lax.broadcasted_iota(jnp.int32, (blk, LANES), 1)
    scale = HEAD_DIM ** -0.5
    contract_last = (((1,), (1,)), ((), ()))
    head_cols = [slice(h * HEAD_DIM, (h + 1) * HEAD_DIM) for h in range(n_heads)]
    head_slope = []
    for h in range(n_heads):
        slope = slopes[0][h]
        for g in range(1, len(slopes)):
            slope = jnp.where(hg == g, slopes[g][h], slope)
        head_slope.append(slope)
    for r in range(dilation):
        lse_tile = jnp.zeros((blk, LANES), F32)
        for h0 in range(0, n_heads, stage_heads):
            group = list(range(h0, min(h0 + stage_heads, n_heads)))
            scores = {h: lax.dot_general(q_ref[r, :, head_cols[h]],
                                         jnp.concatenate([kp_ref[r, :, head_cols[h]], kc_ref[r, :, head_cols[h]]],
                                                         axis=0),
                                         contract_last, preferred_element_type=F32) for h in group}
            probs, dens, maxes = {}, {}, {}
            for h in group:
                s = jnp.where(valid, scores[h] * scale - head_slope[h] * dist, -jnp.inf)
                maxes[h] = jnp.max(s, axis=-1, keepdims=True)
                p = jnp.exp(s - maxes[h])
                dens[h] = jnp.sum(p, axis=-1, keepdims=True)
                probs[h] = p.astype(BF16)
            outs = {h: jnp.dot(probs[h],
                               jnp.concatenate([vp_ref[r, :, head_cols[h]], vc_ref[r, :, head_cols[h]]], axis=0),
                               preferred_element_type=F32) for h in group}
            for h in group:
                o = outs[h] / dens[h]
                lse_h = maxes[h] + jnp.log(dens[h])
                for g in range(len(slopes)):
                    lse_tile = jnp.where(jnp.logical_and(lane == g * n_heads + h, hg == g), lse_h, lse_tile)
                if dilation == 1:
                    o_ref[:, head_cols[h]] = o.astype(o_ref.dtype)
                else:
                    scratch[0][h, pl.ds(r, blk, stride=dilation), :] = o
        if dilation == 1:
            lse_slab = lse_tile
        else:
            scratch[1][pl.ds(r, blk, stride=dilation), :] = lse_tile
    if dilation > 1:
        for h in range(n_heads):
            o_ref[:, h * HEAD_DIM:(h + 1) * HEAD_DIM] = scratch[0][h].astype(o_ref.dtype)
        lse_slab = scratch[1][...]

    @pl.when(hg == 0)
    def _():
        lse_ref[...] = lse_slab

    @pl.when(hg > 0)
    def _():
        lse_ref[...] += lse_slab


def _band_attention(qk, v, bsz, seq, branch, slopes):
    _, dilation = A_BRANCHES[branch]
    width = v.shape[-1]
    n_blk = seq // dilation // BAND_BLOCK
    heads = max(min(width // HEAD_DIM, ATTN_PAIRS_PER_STEP // dilation), 1)
    hw = heads * HEAD_DIM
    n_hg = width // hw
    slope_tab = tuple(tuple(float(s) for s in slopes[g * heads:(g + 1) * heads]) for g in range(n_hg))
    rows = BAND_BLOCK * dilation
    blk = (None, dilation, BAND_BLOCK, hw)

    def prev(n):
        return jnp.maximum(n - 1, 0)

    scratch = []
    if dilation > 1:
        scratch = [pltpu.VMEM((heads, rows, HEAD_DIM), F32), pltpu.VMEM((rows, LANES), F32)]
    return pl.pallas_call(
        functools.partial(_band_attn_kernel, dilation=dilation, slopes=slope_tab,
                          stage_heads=ATTN_STAGE_HEADS[dilation]),
        grid=(bsz, n_blk, n_hg),
        in_specs=[pl.BlockSpec(blk, lambda b, n, g: (b, 0, n, g)),
                  pl.BlockSpec(blk, lambda b, n, g: (b, 0, prev(n), n_hg + g)),
                  pl.BlockSpec(blk, lambda b, n, g: (b, 0, n, n_hg + g)),
                  pl.BlockSpec(blk, lambda b, n, g: (b, 0, prev(n), g)),
                  pl.BlockSpec(blk, lambda b, n, g: (b, 0, n, g))],
        out_specs=[pl.BlockSpec((rows, hw), lambda b, n, g: (b * n_blk + n, g)),
                   pl.BlockSpec((rows, LANES), lambda b, n, g: (b * n_blk + n, 0))],
        out_shape=[jax.ShapeDtypeStruct((bsz * seq, width), BF16),
                   jax.ShapeDtypeStruct((bsz * seq, LANES), F32)],
        scratch_shapes=scratch,
        compiler_params=_params("parallel", "arbitrary", "arbitrary"),
        name=f"band_attn_d{dilation}",
    )(qk, qk, qk, v, v)


def _combine_kernel(o0_ref, o1_ref, o2_ref, l0_ref, l1_ref, l2_ref, out_ref):
    l0, l1, l2 = l0_ref[...], l1_ref[...], l2_ref[...]
    m = jnp.maximum(jnp.maximum(l0, l1), l2)
    e0, e1, e2 = jnp.exp(l0 - m), jnp.exp(l1 - m), jnp.exp(l2 - m)
    tot = e0 + e1 + e2
    w0, w1, w2 = e0 / tot, e1 / tot, e2 / tot
    for h in range(out_ref.shape[1] // HEAD_DIM):
        sl = slice(h * HEAD_DIM, (h + 1) * HEAD_DIM)
        acc = (w0[:, h:h + 1] * o0_ref[:, sl].astype(F32) + w1[:, h:h + 1] * o1_ref[:, sl].astype(F32)
               + w2[:, h:h + 1] * o2_ref[:, sl].astype(F32))
        out_ref[:, sl] = acc.astype(out_ref.dtype)


def _combine_branches(outs, lses, tm=512):
    t, width = outs[0].shape
    o_spec = pl.BlockSpec((tm, width), lambda i: (i, 0))
    l_spec = pl.BlockSpec((tm, LANES), lambda i: (i, 0))
    return pl.pallas_call(
        _combine_kernel,
        grid=(t // tm,),
        in_specs=[o_spec] * 3 + [l_spec] * 3,
        out_specs=o_spec,
        out_shape=jax.ShapeDtypeStruct((t, width), BF16),
        compiler_params=_params("parallel"),
        name="combine_branches",
    )(*outs, *lses)


def _pool_kernel(u_ref, uh_ref, w_ref, sc_ref, o_ref, wb_ref, *, blocks_per_seq):
    i = pl.program_id(0)

    @pl.when(i == 0)
    def _():
        wb_ref[...] = w_ref[...].astype(BF16)

    tm = u_ref.shape[0]
    group = w_ref.shape[1]
    first = (i % blocks_per_seq) == 0
    row = lax.broadcasted_iota(jnp.int32, (tm, 1), 0)
    pos = (i % blocks_per_seq) * tm + row
    for g, win in enumerate(POOL_WINDOWS):
        sl = slice(g * group, (g + 1) * group)
        u = u_ref[:, sl]
        halo = jnp.where(first, 0.0, uh_ref[:, sl])
        s = jnp.concatenate([halo, u], axis=0)
        step = 1
        while step < win:
            s = s + pltpu.roll(s, step, axis=0)
            step *= 2
        count = jnp.minimum(pos + 1, win).astype(F32)
        pooled = s[POOL_HALO:] / count - u
        y = jnp.dot(pooled.astype(BF16), wb_ref[g], preferred_element_type=F32)
        o_ref[:, sl] = (y * sc_ref[:, sl]).astype(o_ref.dtype)


def _pool_mixer(u, pool_w, pool_scale, seq, tm=512):
    t, width = u.shape
    n_group, group, _ = pool_w.shape
    blocks_per_seq = seq // tm
    halo_blocks = tm // POOL_HALO
    return pl.pallas_call(
        functools.partial(_pool_kernel, blocks_per_seq=blocks_per_seq),
        grid=(t // tm,),
        in_specs=[pl.BlockSpec((tm, width), lambda i: (i, 0)),
                  pl.BlockSpec((POOL_HALO, width), lambda i: (jnp.maximum(i * halo_blocks - 1, 0), 0)),
                  pl.BlockSpec((n_group, group, group), lambda i: (0, 0, 0)),
                  pl.BlockSpec((1, width), lambda i: (0, 0))],
        out_specs=pl.BlockSpec((tm, width), lambda i: (i, 0)),
        out_shape=jax.ShapeDtypeStruct((t, width), BF16),
        scratch_shapes=[pltpu.VMEM((n_group, group, group), BF16)],
        compiler_params=_params("arbitrary"),
        name="pool_mixer",
    )(u, u, pool_w, pool_scale.reshape(1, width))


def _split3(x):
    hi = x.astype(BF16)
    r1 = x - hi.astype(F32)
    mid = r1.astype(BF16)
    lo = (r1 - mid.astype(F32)).astype(BF16)
    return hi, mid, lo


def _gate_kernel(h_ref, ng_ref, wr_ref, wa2_ref, ba_ref, hn_ref, o_ref):
    x = h_ref[...]
    ms = jnp.mean(x * x, axis=-1, keepdims=True)
    hn = (x * lax.rsqrt(ms + EPS) * ng_ref[...]).astype(hn_ref.dtype)
    hn_ref[...] = hn
    r = lax.dot_general(hn, wr_ref[...].astype(BF16), (((1,), (1,)), ((), ())), preferred_element_type=F32)
    r_hi, r_mid, _ = _split3(r)
    w_hi, w_mid, _ = _split3(wa2_ref[...])
    g = (jnp.dot(r_hi, w_hi, preferred_element_type=F32)
         + (jnp.dot(r_hi, w_mid, preferred_element_type=F32) + jnp.dot(r_mid, w_hi, preferred_element_type=F32)))
    g = g + ba_ref[...]
    log_sig = jnp.minimum(g, 0.0) - jnp.log1p(jnp.exp(-jnp.abs(g)))
    o_ref[...] = log_sig / C_GATE_TAU


def _gla_norm_log_decay(h, norm_gain, w_r, w_a2, b_a, tm=512):
    t, d = h.shape
    kw = w_a2.shape[1]
    return pl.pallas_call(
        _gate_kernel,
        grid=(t // tm,),
        in_specs=[pl.BlockSpec((tm, d), lambda i: (i, 0)),
                  pl.BlockSpec((1, d), lambda i: (0, 0)),
                  pl.BlockSpec((LANES, d), lambda i: (0, 0)),
                  pl.BlockSpec((LANES, kw), lambda i: (0, 0)),
                  pl.BlockSpec((1, kw), lambda i: (0, 0))],
        out_specs=[pl.BlockSpec((tm, d), lambda i: (i, 0)),
                   pl.BlockSpec((tm, kw), lambda i: (i, 0))],
        out_shape=[jax.ShapeDtypeStruct((t, d), BF16),
                   jax.ShapeDtypeStruct((t, kw), F32)],
        compiler_params=_params("parallel"),
        name="gla_norm_log_decay",
    )(h, norm_gain.reshape(1, d), w_r, w_a2, b_a.reshape(1, kw))


def _gla_kernel(q_ref, k_ref, v_ref, la_ref, gate_ref, og_ref, o_ref, state_ref, *, chunks, heads):
    @pl.when(pl.program_id(2) == 0)
    def _():
        state_ref[...] = jnp.zeros_like(state_ref)

    c = C_CHUNK
    dk = q_ref.shape[1] // heads
    dv = v_ref.shape[1] // heads
    ri = lax.broadcasted_iota(jnp.int32, (c, c), 0)
    ci = lax.broadcasted_iota(jnp.int32, (c, c), 1)
    causal = ci <= ri
    tri = causal.astype(BF16)
    contract_last = (((1,), (1,)), ((), ()))
    contract_first = (((0,), (0,)), ((), ()))
    pairs = [(n, h) for n in range(chunks) for h in range(heads)]

    def rows(n):
        return slice(n * c, (n + 1) * c)

    def kcols(h):
        return slice(h * dk, (h + 1) * dk)

    def vcols(h):
        return slice(h * dv, (h + 1) * dv)

    bc = {}
    for n, h in pairs:
        la_hi, la_mid, la_lo = _split3(la_ref[rows(n), kcols(h)])
        bc[n, h] = (jnp.dot(tri, la_hi, preferred_element_type=F32)
                    + jnp.dot(tri, la_mid, preferred_element_type=F32)
                    + jnp.dot(tri, la_lo, preferred_element_type=F32))
    q_b, k_in, k_st, decay = {}, {}, {}, {}
    for n, h in pairs:
        b = bc[n, h]
        b_last = b[c - 1:c, :]
        q_b[n, h] = (q_ref[rows(n), kcols(h)] * (dk ** -0.5) * jnp.exp(b)).astype(BF16)
        k = k_ref[rows(n), kcols(h)]
        k_in[n, h] = (k * jnp.exp(-b)).astype(BF16)
        k_st[n, h] = (k * jnp.exp(b_last - b)).astype(BF16)
        decay[n, h] = jnp.exp(b_last)
    att = {}
    for p in pairs:
        a = lax.dot_general(q_b[p], k_in[p], contract_last, preferred_element_type=F32)
        att[p] = jnp.where(causal, a, 0.0).astype(BF16)
    o_intra, upd = {}, {}
    for n, h in pairs:
        v = v_ref[rows(n), vcols(h)]
        o_intra[n, h] = jnp.dot(att[n, h], v, preferred_element_type=F32)
        upd[n, h] = lax.dot_general(v, k_st[n, h], contract_first, preferred_element_type=F32)
    for h in range(heads):
        state = state_ref[h]
        for n in range(chunks):
            o = o_intra[n, h] + lax.dot_general(q_b[n, h], state.astype(BF16), contract_last,
                                                preferred_element_type=F32)
            state = state * decay[n, h] + upd[n, h]
            ms = jnp.mean(o * o, axis=-1, keepdims=True)
            gate = gate_ref[rows(n), vcols(h)]
            y = o * lax.rsqrt(ms + EPS) * og_ref[...] * (gate * jax.nn.sigmoid(gate))
            o_ref[rows(n), vcols(h)] = y.astype(o_ref.dtype)
        state_ref[h] = state


def _gla(qk, v, gate, log_a, o_gain, bsz, seq, rows=256, heads=2):
    t = bsz * seq
    dk = log_a.shape[1] // C_HEADS
    dv = v.shape[1] // C_HEADS
    steps = seq // rows
    groups = C_HEADS // heads

    def row_block(b, n):
        return b * steps + n

    return pl.pallas_call(
        functools.partial(_gla_kernel, chunks=rows // C_CHUNK, heads=heads),
        grid=(bsz, groups, steps),
        in_specs=[pl.BlockSpec((rows, heads * dk), lambda b, g, n: (row_block(b, n), g)),
                  pl.BlockSpec((rows, heads * dk), lambda b, g, n: (row_block(b, n), groups + g)),
                  pl.BlockSpec((rows, heads * dv), lambda b, g, n: (row_block(b, n), g)),
                  pl.BlockSpec((rows, heads * dk), lambda b, g, n: (row_block(b, n), g)),
                  pl.BlockSpec((rows, heads * dv), lambda b, g, n: (row_block(b, n), g)),
                  pl.BlockSpec((1, dv), lambda b, g, n: (0, 0))],
        out_specs=pl.BlockSpec((rows, heads * dv), lambda b, g, n: (row_block(b, n), g)),
        out_shape=jax.ShapeDtypeStruct((t, v.shape[1]), BF16),
        scratch_shapes=[pltpu.VMEM((heads, dv, dk), F32)],
        compiler_params=_params("parallel", "parallel", "arbitrary"),
        name="gla",
    )(qk, qk, v, log_a, gate, o_gain.reshape(1, dv))


def _xattn_kernel(h_ref, g_ref, wq_ref, qg_ref, k_ref, v_ref, wo_ref, fg_ref, o_ref, on_ref):
    x = h_ref[...]
    ms = jnp.mean(x * x, axis=-1, keepdims=True)
    hn = (x * lax.rsqrt(ms + EPS) * g_ref[...]).astype(BF16)
    q = jnp.dot(hn, wq_ref[...], preferred_element_type=F32)
    scale = HEAD_DIM ** -0.5
    contract_last = (((1,), (1,)), ((), ()))
    cols = [slice(hd * HEAD_DIM, (hd + 1) * HEAD_DIM) for hd in range(X_HEADS)]
    q_heads = []
    for sl in cols:
        qh = q[:, sl]
        qms = jnp.mean(qh * qh, axis=-1, keepdims=True)
        q_heads.append((qh * lax.rsqrt(qms + EPS) * qg_ref[...]).astype(BF16))
    scores = [lax.dot_general(qh, k_ref[0, :, sl], contract_last, preferred_element_type=F32) * scale
              for qh, sl in zip(q_heads, cols)]
    probs = []
    for s in scores:
        m = jnp.max(s, axis=-1, keepdims=True)
        p = jnp.exp(s - m)
        probs.append((p / jnp.sum(p, axis=-1, keepdims=True)).astype(BF16))
    heads = [jnp.dot(p, v_ref[0, :, sl], preferred_element_type=F32) for p, sl in zip(probs, cols)]
    o = jnp.concatenate(heads, axis=-1).astype(BF16)
    y = x + jnp.dot(o, wo_ref[...], preferred_element_type=F32)
    o_ref[...] = y
    yms = jnp.mean(y * y, axis=-1, keepdims=True)
    on_ref[...] = (y * lax.rsqrt(yms + EPS) * fg_ref[...]).astype(on_ref.dtype)


def _cross_attention(h, norm_gain, wq, q_gain, k, v, wo, next_gain, seq, tm=256):
    t, d = h.shape
    xw = wq.shape[1]
    mlen = k.shape[1]
    tiles_per_seq = seq // tm
    return pl.pallas_call(
        _xattn_kernel,
        grid=(t // tm,),
        in_specs=[pl.BlockSpec((tm, d), lambda i: (i, 0)),
                  pl.BlockSpec((1, d), lambda i: (0, 0)),
                  pl.BlockSpec((d, xw), lambda i: (0, 0)),
                  pl.BlockSpec((1, HEAD_DIM), lambda i: (0, 0)),
                  pl.BlockSpec((1, mlen, xw), lambda i: (i // tiles_per_seq, 0, 0)),
                  pl.BlockSpec((1, mlen, xw), lambda i: (i // tiles_per_seq, 0, 0)),
                  pl.BlockSpec((xw, d), lambda i: (0, 0)),
                  pl.BlockSpec((1, d), lambda i: (0, 0))],
        out_specs=[pl.BlockSpec((tm, d), lambda i: (i, 0)),
                   pl.BlockSpec((tm, d), lambda i: (i, 0))],
        out_shape=[jax.ShapeDtypeStruct((t, d), F32),
                   jax.ShapeDtypeStruct((t, d), BF16)],
        compiler_params=_params("parallel"),
        name="cross_attention",
    )(h, norm_gain.reshape(1, d), wq, q_gain.reshape(1, HEAD_DIM), k, v, wo, next_gain.reshape(1, d))


def _ffn_up_kernel(a_ref, ah_ref, wg_ref, wv_ref, cwg_ref, cwv_ref, cbg_ref, cbv_ref, wd_ref, o_ref, wdo_ref, *,
                   blocks_per_seq):
    wdo_ref[...] = wd_ref[...].astype(wdo_ref.dtype)
    tm = a_ref.shape[0]
    first = (pl.program_id(0) % blocks_per_seq) == 0
    row = lax.broadcasted_iota(jnp.int32, (tm, 1), 0)
    a = a_ref[...]
    ah = ah_ref[...]

    def conv_half(w_ref, cw_ref, cb_ref):
        w = w_ref[...].astype(BF16)
        u = jnp.dot(a, w, preferred_element_type=F32)
        uh = jnp.where(first, 0.0, jnp.dot(ah, w, preferred_element_type=F32))
        u1 = jnp.where(row == 0, uh[CONV_HALO - 1:CONV_HALO], pltpu.roll(u, 1, axis=0))
        u2 = jnp.where(row == 0, uh[CONV_HALO - 2:CONV_HALO - 1],
                       jnp.where(row == 1, uh[CONV_HALO - 1:CONV_HALO], pltpu.roll(u, 2, axis=0)))
        cw = cw_ref[...]
        return cb_ref[...] + cw[0:1] * u2 + cw[1:2] * u1 + cw[2:3] * u

    cg = conv_half(wg_ref, cwg_ref, cbg_ref)
    cv = conv_half(wv_ref, cwv_ref, cbv_ref)
    o_ref[...] = (cg * jax.nn.sigmoid(cg) * cv).astype(o_ref.dtype)


def _ffn_up(hn, w_up, conv_w, conv_b, w_down, layer, seq, tm=FFN_ROWS, tn=256):
    t, d = hn.shape
    d_ff = w_up.shape[2] // 2
    n_tiles = d_ff // tn
    n_steps = (t // tm) * n_tiles
    slab = d_ff // n_steps
    assert d_ff % n_steps == 0 and slab % (2 * SUBLANES) == 0
    blocks_per_seq = seq // tm
    halo_blocks = tm // CONV_HALO
    cw = conv_w[layer]
    cb = conv_b[layer].reshape(1, 2 * d_ff)
    return pl.pallas_call(
        functools.partial(_ffn_up_kernel, blocks_per_seq=blocks_per_seq),
        grid=(t // tm, n_tiles),
        in_specs=[pl.BlockSpec((tm, d), lambda i, j: (i, 0), pipeline_mode=pl.Buffered(1)),
                  pl.BlockSpec((CONV_HALO, d), lambda i, j: (jnp.maximum(i * halo_blocks - 1, 0), 0)),
                  pl.BlockSpec((None, d, tn), lambda i, j: (layer, 0, j)),
                  pl.BlockSpec((None, d, tn), lambda i, j: (layer, 0, n_tiles + j)),
                  pl.BlockSpec((CONV_WIDTH, tn), lambda i, j: (0, j)),
                  pl.BlockSpec((CONV_WIDTH, tn), lambda i, j: (0, n_tiles + j)),
                  pl.BlockSpec((1, tn), lambda i, j: (0, j)),
                  pl.BlockSpec((1, tn), lambda i, j: (0, n_tiles + j)),
                  pl.BlockSpec((None, slab, d), lambda i, j: (layer, i * n_tiles + j, 0))],
        out_specs=[pl.BlockSpec((tm, tn), lambda i, j: (i, j)),
                   pl.BlockSpec((slab, d), lambda i, j: (i * n_tiles + j, 0))],
        out_shape=[jax.ShapeDtypeStruct((t, d_ff), BF16),
                   jax.ShapeDtypeStruct((d_ff, d), BF16)],
        compiler_params=_params("parallel", "arbitrary"),
        name="ffn_up",
    )(hn, hn, w_up, w_up, cw, cw, cb, cb, w_down)


def _dilated_pool_layer(h, hn, w_in, q_gain, k_gain, pool_w, pool_scale, w_out, j, bsz, seq):
    d = h.shape[1]
    a_width = d // 2
    heads = a_width // HEAD_DIM
    tn = 512
    q_blocks = a_width // tn
    k_base = N_BRANCH * q_blocks
    v_cols = 2 * N_BRANCH * a_width
    gains = jnp.concatenate([jnp.tile(q_gain, heads), jnp.tile(k_gain, heads)]).reshape(1, 2 * a_width)
    dils = tuple(dil for _, dil in A_BRANCHES)
    v_list = _mm([hn], w_in, j, lambda c: v_cols // tn + c, a_width, BF16, dils=dils, seq=seq, tn=tn,
                 name="ab_in_v")
    u = _mm([hn], w_in, j, lambda c: (v_cols + a_width) // tn + c, d - a_width, F32, tn=tn, name="ab_in_u")
    slopes = _alibi_slopes(heads)
    outs, lses = [], []
    for g, dil in enumerate(dils):
        def wcol(c, g=g):
            return jnp.where(c < q_blocks, g * q_blocks + c, k_base + g * q_blocks + c - q_blocks)
        qk = _mm([hn], w_in, j, wcol, 2 * a_width, BF16, mode="headnorm", extra=gains, dils=(dil,), seq=seq,
                 tn=tn, name=f"ab_in_qk{g}")
        if dil == 1:
            qk = qk.reshape(bsz, 1, seq, 2 * a_width)
            v_g = v_list[g].reshape(bsz, 1, seq, a_width)
        else:
            v_g = v_list[g]
        o, lse = _band_attention(qk, v_g, bsz, seq, g, slopes[g])
        outs.append(o)
        lses.append(lse)
    a_out = _combine_branches(outs, lses)
    b_out = _pool_mixer(u, pool_w, pool_scale, seq)
    return _mm([a_out, b_out], w_out, j, lambda c: c, d, F32, mode="residual", extra=h, name="ab_out")


def _gla_layer(h, norm_gain, w_in, w_a2, b_a, o_gain, w_out, j, bsz, seq):
    d = h.shape[1]
    kw = w_a2.shape[2]
    vw = w_out.shape[1]
    tm, tn = 2048, 256
    w_in_t = jnp.swapaxes(w_in, 1, 2)
    rank = w_a2.shape[1]
    w_r = jnp.pad(w_in_t[j, 2 * kw + 2 * vw:, :], ((0, LANES - rank), (0, 0)))
    w_a2p = jnp.pad(w_a2[j], ((0, LANES - rank), (0, 0)))
    hn, log_a = _gla_norm_log_decay(h, norm_gain, w_r, w_a2p, b_a[j])
    qk = _mm([hn], w_in_t, j, lambda c: c, 2 * kw, F32, w_t=True, tm=tm, tn=tn, name="c_in_qk")
    v = _mm([hn], w_in_t, j, lambda c: 2 * kw // tn + c, vw, BF16, w_t=True, tm=tm, tn=tn, name="c_in_v")
    gate = _mm([hn], w_in_t, j, lambda c: (2 * kw + vw) // tn + c, vw, F32, w_t=True, tm=tm, tn=tn, name="c_in_gate")
    o = _gla(qk, v, gate, log_a, o_gain[j], bsz, seq)
    return _mm([o], w_out, j, lambda c: c, d, F32, mode="residual", extra=h, name="c_out")


def _memory_kv(mem, gain, wkv, k_gain, layer, bsz):
    xw = wkv.shape[2] // 2
    tn = 256
    mem_n = _rmsnorm(mem, gain)
    rows = mem.shape[0]
    gains = jnp.tile(k_gain, xw // HEAD_DIM).reshape(1, xw)
    k = _mm([mem_n], wkv, layer, lambda c: c, xw, BF16, mode="headnorm", extra=gains, tm=rows, tn=tn, name="mem_k")
    v = _mm([mem_n], wkv, layer, lambda c: xw // tn + c, xw, BF16, tm=rows, tn=tn, name="mem_v")
    return k.reshape(bsz, rows // bsz, xw), v.reshape(bsz, rows // bsz, xw)


def kernel(x, mem, mix_norm, ab_w_in, ab_q_norm, ab_k_norm, ab_pool_w, ab_pool_scale, ab_w_out, c_w_in, c_w_a2, c_b_a, c_o_norm, c_w_out, x_norm, x_mem_norm, x_wq, x_wkv, x_q_norm, x_k_norm, x_wo, f_norm, f_w_up, f_conv_w, f_conv_b, f_w_down):
    bsz, seq, d = x.shape
    depth = mix_norm.shape[0]
    h = x.reshape(bsz * seq, d)
    mem2 = mem.reshape(bsz * mem.shape[1], d)
    for layer in range(depth):
        j = layer // 2
        if layer % 2 == 0:
            hn = _rmsnorm(h, mix_norm[layer])
            h = _dilated_pool_layer(h, hn, ab_w_in, ab_q_norm[j], ab_k_norm[j], ab_pool_w[j], ab_pool_scale[j],
                                    ab_w_out, j, bsz, seq)
        else:
            h = _gla_layer(h, mix_norm[layer], c_w_in, c_w_a2, c_b_a, c_o_norm, c_w_out, j, bsz, seq)
        k, v = _memory_kv(mem2, x_mem_norm[layer], x_wkv, x_k_norm[layer], layer, bsz)
        h, hn = _cross_attention(h, x_norm[layer], x_wq[layer].astype(BF16), x_q_norm[layer], k, v,
                                 x_wo[layer].astype(BF16), f_norm[layer], seq)
        act, w_down = _ffn_up(hn, f_w_up, f_conv_w, f_conv_b, f_w_down, layer, seq)
        h = _mm([act], w_down, None, lambda c: c, d, F32, mode="residual", extra=h, tm=512, name="ffn_down")
    return h.reshape(bsz, seq, d)
```

```python
import functools

import numpy as np
import jax
import jax.numpy as jnp
from jax import lax
from jax.experimental import pallas as pl
from jax.experimental.pallas import tpu as pltpu

F32 = jnp.float32
BF16 = jnp.bfloat16

LANES = 128
SUBLANES = 8
VMEM_LIMIT_BYTES = 56 * 2 ** 20

EPS = 1e-6
HEAD_DIM = 128
A_BRANCHES = ((128, 1), (512, 4), (2048, 16))
N_BRANCH = len(A_BRANCHES)
BAND_BLOCK = 128
POOL_WINDOWS = (2, 4, 8, 16)
POOL_HALO = 16
C_HEADS = 8
C_GATE_RANK = 16
C_GATE_TAU = 16.0
C_CHUNK = 64
X_HEADS = 4
CONV_WIDTH = 3
CONV_HALO = SUBLANES
MM_ROWS = 1024
ATTN_STAGE_HEADS = {1: 8, 4: 16, 16: 1}
ATTN_PAIRS_PER_STEP = 64
FFN_ROWS = 2048


def _params(*semantics):
    return pltpu.CompilerParams(dimension_semantics=semantics, vmem_limit_bytes=VMEM_LIMIT_BYTES)


def _lane_groups(width):
    return [slice(c * LANES, (c + 1) * LANES) for c in range(width // LANES)]


def _rmsnorm_kernel(x_ref, g_ref, o_ref):
    x = x_ref[...].astype(F32)
    ms = jnp.mean(x * x, axis=-1, keepdims=True)
    o_ref[...] = (x * lax.rsqrt(ms + EPS) * g_ref[...]).astype(o_ref.dtype)


def _rmsnorm(x, gain, tm=256):
    t, d = x.shape
    return pl.pallas_call(
        _rmsnorm_kernel,
        grid=(t // tm,),
        in_specs=[pl.BlockSpec((tm, d), lambda i: (i, 0)),
                  pl.BlockSpec((1, d), lambda i: (0, 0))],
        out_specs=pl.BlockSpec((tm, d), lambda i: (i, 0)),
        out_shape=jax.ShapeDtypeStruct((t, d), BF16),
        compiler_params=_params("parallel"),
        name="rmsnorm",
    )(x, gain.reshape(1, d))


def _mm_kernel(*refs, n_a, mode, dils, w_t):
    a_refs = refs[:n_a]
    w_ref = refs[n_a]
    n_extra = 1 if mode in ("headnorm", "residual") else 0
    extra = refs[n_a + 1:n_a + 1 + n_extra]
    o_refs = refs[n_a + 1 + n_extra:n_a + 1 + n_extra + len(dils)]
    scratch = refs[n_a + 1 + n_extra + len(dils):]
    acc = None
    k0 = 0
    for a_ref in a_refs:
        kk = a_ref.shape[1]
        if w_t:
            part = lax.dot_general(a_ref[...], w_ref[:, k0:k0 + kk].astype(BF16), (((1,), (1,)), ((), ())),
                                   preferred_element_type=F32)
        else:
            part = jnp.dot(a_ref[...], w_ref[k0:k0 + kk, :].astype(BF16), preferred_element_type=F32)
        acc = part if acc is None else acc + part
        k0 += kk
    tm, tn = acc.shape
    for c, sl in enumerate(_lane_groups(tn)):
        blk = acc[:, sl]
        if mode == "headnorm":
            ms = jnp.mean(blk * blk, axis=-1, keepdims=True)
            blk = blk * lax.rsqrt(ms + EPS) * extra[0][:, sl]
        elif mode == "residual":
            blk = extra[0][:, sl] + blk
        if scratch:
            scratch[0][c] = blk
        for o_ref, dil in zip(o_refs, dils):
            if dil == 1:
                o_ref[:, sl] = blk.astype(o_ref.dtype)
    for o_ref, dil in zip(o_refs, dils):
        if dil > 1:
            for r in range(dil):
                for c, sl in enumerate(_lane_groups(tn)):
                    o_ref[r, :, sl] = scratch[0][c, pl.ds(r, tm // dil, stride=dil), :].astype(o_ref.dtype)


def _mm(a_list, w, layer, wcol, ncols, out_dtype, *, mode="plain", extra=None, dils=(1,), seq=None,
        w_t=False, tm=MM_ROWS, tn=512, name="mm"):
    t = a_list[0].shape[0]
    k_total = sum(a.shape[1] for a in a_list)
    assert w.shape[-1 if w_t else -2] == k_total and ncols % tn == 0 and t % tm == 0
    in_specs = [pl.BlockSpec((tm, a.shape[1]), lambda i, j: (i, 0)) for a in a_list]
    if w_t:
        in_specs.append(pl.BlockSpec((None, tn, k_total), lambda i, j: (layer, wcol(j), 0)))
    elif w.ndim == 3:
        in_specs.append(pl.BlockSpec((None, k_total, tn), lambda i, j: (layer, 0, wcol(j))))
    else:
        in_specs.append(pl.BlockSpec((k_total, tn), lambda i, j: (0, wcol(j))))
    args = list(a_list) + [w]
    if mode == "headnorm":
        in_specs.append(pl.BlockSpec((1, tn), lambda i, j: (0, j)))
        args.append(extra)
    elif mode == "residual":
        in_specs.append(pl.BlockSpec((tm, tn), lambda i, j: (i, j)))
        args.append(extra)
    out_specs, out_shapes = [], []
    for dil in dils:
        if dil == 1:
            out_specs.append(pl.BlockSpec((tm, tn), lambda i, j: (i, j)))
            out_shapes.append(jax.ShapeDtypeStruct((t, ncols), out_dtype))
        else:
            tiles = seq // tm
            assert seq % tm == 0 and tm % dil == 0
            out_specs.append(pl.BlockSpec((None, dil, tm // dil, tn), lambda i, j: (i // tiles, 0, i % tiles, j)))
            out_shapes.append(jax.ShapeDtypeStruct((t // seq, dil, seq // dil, ncols), out_dtype))
    scratch = [pltpu.VMEM((tn // LANES, tm, LANES), F32)] if any(dil > 1 for dil in dils) else []
    outs = pl.pallas_call(
        functools.partial(_mm_kernel, n_a=len(a_list), mode=mode, dils=tuple(dils), w_t=w_t),
        grid=(t // tm, ncols // tn),
        in_specs=in_specs,
        out_specs=out_specs,
        out_shape=out_shapes,
        scratch_shapes=scratch,
        compiler_params=_params("parallel", "arbitrary"),
        name=name,
    )(*args)
    return outs[0] if len(dils) == 1 else outs


def _alibi_slopes(heads):
    n = N_BRANCH * heads
    s = np.power(np.float32(2.0), -8.0 * np.arange(1, n + 1, dtype=np.float32) / np.float32(n)).astype(np.float32)
    return s.reshape(N_BRANCH, -1)


def _band_attn_kernel(q_ref, kp_ref, kc_ref, vp_ref, vc_ref, o_ref, lse_ref, *scratch, dilation, slopes,
                      stage_heads):
    blk = BAND_BLOCK
    hg = pl.program_id(2)
    has_prev = pl.program_id(1) > 0
    n_heads = len(slopes[0])
    qi = lax.broadcasted_iota(jnp.int32, (blk, 2 * blk), 0)
    ki = lax.broadcasted_iota(jnp.int32, (blk, 2 * blk), 1)
    rel = qi + blk - ki
    valid = jnp.logical_and(jnp.logical_and(rel >= 0, rel <= blk), jnp.logical_or(ki >= blk, has_prev))
    dist = (rel * dilation).astype(F32)
    lane = lax.broadcasted_iota(jnp.int32, (blk, LANES), 1)
    scale = HEAD_DIM ** -0.5
    contract_last = (((1,), (1,)), ((), ()))
    head_cols = [slice(h * HEAD_DIM, (h + 1) * HEAD_DIM) for h in range(n_heads)]
    head_slope = []
    for h in range(n_heads):
        slope = slopes[0][h]
        for g in range(1, len(slopes)):
            slope = jnp.where(hg == g, slopes[g][h], slope)
        head_slope.append(slope)
    for r in range(dilation):
        lse_tile = jnp.zeros((blk, LANES), F32)
        for h0 in range(0, n_heads, stage_heads):
            group = list(range(h0, min(h0 + stage_heads, n_heads)))
            scores = {h: lax.dot_general(q_ref[r, :, head_cols[h]],
                                         jnp.concatenate([kp_ref[r, :, head_cols[h]], kc_ref[r, :, head_cols[h]]],
                                                         axis=0),
                                         contract_last, preferred_element_type=F32) for h in group}
            probs, dens, maxes = {}, {}, {}
            for h in group:
                s = jnp.where(valid, scores[h] * scale - head_slope[h] * dist, -jnp.inf)
                maxes[h] = jnp.max(s, axis=-1, keepdims=True)
                p = jnp.exp(s - maxes[h])
                dens[h] = jnp.sum(p, axis=-1, keepdims=True)
                probs[h] = p.astype(BF16)
            outs = {h: jnp.dot(probs[h],
                               jnp.concatenate([vp_ref[r, :, head_cols[h]], vc_ref[r, :, head_cols[h]]], axis=0),
                               preferred_element_type=F32) for h in group}
            for h in group:
                o = outs[h] / dens[h]
                lse_h = maxes[h] + jnp.log(dens[h])
                for g in range(len(slopes)):
                    lse_tile = jnp.where(jnp.logical_and(lane == g * n_heads + h, hg == g), lse_h, lse_tile)
                if dilation == 1:
                    o_ref[:, head_cols[h]] = o.astype(o_ref.dtype)
                else:
                    scratch[0][h, pl.ds(r, blk, stride=dilation), :] = o
        if dilation == 1:
            lse_slab = lse_tile
        else:
            scratch[1][pl.ds(r, blk, stride=dilation), :] = lse_tile
    if dilation > 1:
        for h in range(n_heads):
            o_ref[:, h * HEAD_DIM:(h + 1) * HEAD_DIM] = scratch[0][h].astype(o_ref.dtype)
        lse_slab = scratch[1][...]

    @pl.when(hg == 0)
    def _():
        lse_ref[...] = lse_slab

    @pl.when(hg > 0)
    def _():
        lse_ref[...] += lse_slab


def _band_attention(qk, v, bsz, seq, branch, slopes):
    _, dilation = A_BRANCHES[branch]
    width = v.shape[-1]
    n_blk = seq // dilation // BAND_BLOCK
    heads = max(min(width // HEAD_DIM, ATTN_PAIRS_PER_STEP // dilation), 1)
    hw = heads * HEAD_DIM
    n_hg = width // hw
    slope_tab = tuple(tuple(float(s) for s in slopes[g * heads:(g + 1) * heads]) for g in range(n_hg))
    rows = BAND_BLOCK * dilation
    blk = (None, dilation, BAND_BLOCK, hw)

    def prev(n):
        return jnp.maximum(n - 1, 0)

    scratch = []
    if dilation > 1:
        scratch = [pltpu.VMEM((heads, rows, HEAD_DIM), F32), pltpu.VMEM((rows, LANES), F32)]
    return pl.pallas_call(
        functools.partial(_band_attn_kernel, dilation=dilation, slopes=slope_tab,
                          stage_heads=ATTN_STAGE_HEADS[dilation]),
        grid=(bsz, n_blk, n_hg),
        in_specs=[pl.BlockSpec(blk, lambda b, n, g: (b, 0, n, g)),
                  pl.BlockSpec(blk, lambda b, n, g: (b, 0, prev(n), n_hg + g)),
                  pl.BlockSpec(blk, lambda b, n, g: (b, 0, n, n_hg + g)),
                  pl.BlockSpec(blk, lambda b, n, g: (b, 0, prev(n), g)),
                  pl.BlockSpec(blk, lambda b, n, g: (b, 0, n, g))],
        out_specs=[pl.BlockSpec((rows, hw), lambda b, n, g: (b * n_blk + n, g)),
                   pl.BlockSpec((rows, LANES), lambda b, n, g: (b * n_blk + n, 0))],
        out_shape=[jax.ShapeDtypeStruct((bsz * seq, width), BF16),
                   jax.ShapeDtypeStruct((bsz * seq, LANES), F32)],
        scratch_shapes=scratch,
        compiler_params=_params("parallel", "arbitrary", "arbitrary"),
        name=f"band_attn_d{dilation}",
    )(qk, qk, qk, v, v)


def _combine_kernel(o0_ref, o1_ref, o2_ref, l0_ref, l1_ref, l2_ref, out_ref):
    l0, l1, l2 = l0_ref[...], l1_ref[...], l2_ref[...]
    m = jnp.maximum(jnp.maximum(l0, l1), l2)
    e0, e1, e2 = jnp.exp(l0 - m), jnp.exp(l1 - m), jnp.exp(l2 - m)
    tot = e0 + e1 + e2
    w0, w1, w2 = e0 / tot, e1 / tot, e2 / tot
    for h in range(out_ref.shape[1] // HEAD_DIM):
        sl = slice(h * HEAD_DIM, (h + 1) * HEAD_DIM)
        acc = (w0[:, h:h + 1] * o0_ref[:, sl].astype(F32) + w1[:, h:h + 1] * o1_ref[:, sl].astype(F32)
               + w2[:, h:h + 1] * o2_ref[:, sl].astype(F32))
        out_ref[:, sl] = acc.astype(out_ref.dtype)


def _combine_branches(outs, lses, tm=512):
    t, width = outs[0].shape
    o_spec = pl.BlockSpec((tm, width), lambda i: (i, 0))
    l_spec = pl.BlockSpec((tm, LANES), lambda i: (i, 0))
    return pl.pallas_call(
        _combine_kernel,
        grid=(t // tm,),
        in_specs=[o_spec] * 3 + [l_spec] * 3,
        out_specs=o_spec,
        out_shape=jax.ShapeDtypeStruct((t, width), BF16),
        compiler_params=_params("parallel"),
        name="combine_branches",
    )(*outs, *lses)


def _pool_kernel(u_ref, uh_ref, w_ref, sc_ref, o_ref, wb_ref, *, blocks_per_seq):
    i = pl.program_id(0)

    @pl.when(i == 0)
    def _():
        wb_ref[...] = w_ref[...].astype(BF16)

    tm = u_ref.shape[0]
    group = w_ref.shape[1]
    first = (i % blocks_per_seq) == 0
    row = lax.broadcasted_iota(jnp.int32, (tm, 1), 0)
    pos = (i % blocks_per_seq) * tm + row
    for g, win in enumerate(POOL_WINDOWS):
        sl = slice(g * group, (g + 1) * group)
        u = u_ref[:, sl]
        halo = jnp.where(first, 0.0, uh_ref[:, sl])
        s = jnp.concatenate([halo, u], axis=0)
        step = 1
        while step < win:
            s = s + pltpu.roll(s, step, axis=0)
            step *= 2
        count = jnp.minimum(pos + 1, win).astype(F32)
        pooled = s[POOL_HALO:] / count - u
        y = jnp.dot(pooled.astype(BF16), wb_ref[g], preferred_element_type=F32)
        o_ref[:, sl] = (y * sc_ref[:, sl]).astype(o_ref.dtype)


def _pool_mixer(u, pool_w, pool_scale, seq, tm=512):
    t, width = u.shape
    n_group, group, _ = pool_w.shape
    blocks_per_seq = seq // tm
    halo_blocks = tm // POOL_HALO
    return pl.pallas_call(
        functools.partial(_pool_kernel, blocks_per_seq=blocks_per_seq),
        grid=(t // tm,),
        in_specs=[pl.BlockSpec((tm, width), lambda i: (i, 0)),
                  pl.BlockSpec((POOL_HALO, width), lambda i: (jnp.maximum(i * halo_blocks - 1, 0), 0)),
                  pl.BlockSpec((n_group, group, group), lambda i: (0, 0, 0)),
                  pl.BlockSpec((1, width), lambda i: (0, 0))],
        out_specs=pl.BlockSpec((tm, width), lambda i: (i, 0)),
        out_shape=jax.ShapeDtypeStruct((t, width), BF16),
        scratch_shapes=[pltpu.VMEM((n_group, group, group), BF16)],
        compiler_params=_params("arbitrary"),
        name="pool_mixer",
    )(u, u, pool_w, pool_scale.reshape(1, width))


def _split3(x):
    hi = x.astype(BF16)
    r1 = x - hi.astype(F32)
    mid = r1.astype(BF16)
    lo = (r1 - mid.astype(F32)).astype(BF16)
    return hi, mid, lo


def _gate_kernel(h_ref, ng_ref, wr_ref, wa2_ref, ba_ref, hn_ref, o_ref):
    x = h_ref[...]
    ms = jnp.mean(x * x, axis=-1, keepdims=True)
    hn = (x * lax.rsqrt(ms + EPS) * ng_ref[...]).astype(hn_ref.dtype)
    hn_ref[...] = hn
    r = lax.dot_general(hn, wr_ref[...].astype(BF16), (((1,), (1,)), ((), ())), preferred_element_type=F32)
    r_hi, r_mid, _ = _split3(r)
    w_hi, w_mid, _ = _split3(wa2_ref[...])
    g = (jnp.dot(r_hi, w_hi, preferred_element_type=F32)
         + (jnp.dot(r_hi, w_mid, preferred_element_type=F32) + jnp.dot(r_mid, w_hi, preferred_element_type=F32)))
    g = g + ba_ref[...]
    log_sig = jnp.minimum(g, 0.0) - jnp.log1p(jnp.exp(-jnp.abs(g)))
    o_ref[...] = log_sig / C_GATE_TAU


def _gla_norm_log_decay(h, norm_gain, w_r, w_a2, b_a, tm=512):
    t, d = h.shape
    kw = w_a2.shape[1]
    return pl.pallas_call(
        _gate_kernel,
        grid=(t // tm,),
        in_specs=[pl.BlockSpec((tm, d), lambda i: (i, 0)),
                  pl.BlockSpec((1, d), lambda i: (0, 0)),
                  pl.BlockSpec((LANES, d), lambda i: (0, 0)),
                  pl.BlockSpec((LANES, kw), lambda i: (0, 0)),
                  pl.BlockSpec((1, kw), lambda i: (0, 0))],
        out_specs=[pl.BlockSpec((tm, d), lambda i: (i, 0)),
                   pl.BlockSpec((tm, kw), lambda i: (i, 0))],
        out_shape=[jax.ShapeDtypeStruct((t, d), BF16),
                   jax.ShapeDtypeStruct((t, kw), F32)],
        compiler_params=_params("parallel"),
        name="gla_norm_log_decay",
    )(h, norm_gain.reshape(1, d), w_r, w_a2, b_a.reshape(1, kw))


def _gla_kernel(q_ref, k_ref, v_ref, la_ref, gate_ref, og_ref, o_ref, state_ref, *, chunks, heads):
    @pl.when(pl.program_id(2) == 0)
    def _():
        state_ref[...] = jnp.zeros_like(state_ref)

    c = C_CHUNK
    dk = q_ref.shape[1] // heads
    dv = v_ref.shape[1] // heads
    ri = lax.broadcasted_iota(jnp.int32, (c, c), 0)
    ci = lax.broadcasted_iota(jnp.int32, (c, c), 1)
    causal = ci <= ri
    tri = causal.astype(BF16)
    contract_last = (((1,), (1,)), ((), ()))
    contract_first = (((0,), (0,)), ((), ()))
    pairs = [(n, h) for n in range(chunks) for h in range(heads)]

    def rows(n):
        return slice(n * c, (n + 1) * c)

    def kcols(h):
        return slice(h * dk, (h + 1) * dk)

    def vcols(h):
        return slice(h * dv, (h + 1) * dv)

    bc = {}
    for n, h in pairs:
        la_hi, la_mid, la_lo = _split3(la_ref[rows(n), kcols(h)])
        bc[n, h] = (jnp.dot(tri, la_hi, preferred_element_type=F32)
                    + jnp.dot(tri, la_mid, preferred_element_type=F32)
                    + jnp.dot(tri, la_lo, preferred_element_type=F32))
    q_b, k_in, k_st, decay = {}, {}, {}, {}
    for n, h in pairs:
        b = bc[n, h]
        b_last = b[c - 1:c, :]
        q_b[n, h] = (q_ref[rows(n), kcols(h)] * (dk ** -0.5) * jnp.exp(b)).astype(BF16)
        k = k_ref[rows(n), kcols(h)]
        k_in[n, h] = (k * jnp.exp(-b)).astype(BF16)
        k_st[n, h] = (k * jnp.exp(b_last - b)).astype(BF16)
        decay[n, h] = jnp.exp(b_last)
    att = {}
    for p in pairs:
        a = lax.dot_general(q_b[p], k_in[p], contract_last, preferred_element_type=F32)
        att[p] = jnp.where(causal, a, 0.0).astype(BF16)
    o_intra, upd = {}, {}
    for n, h in pairs:
        v = v_ref[rows(n), vcols(h)]
        o_intra[n, h] = jnp.dot(att[n, h], v, preferred_element_type=F32)
        upd[n, h] = lax.dot_general(v, k_st[n, h], contract_first, preferred_element_type=F32)
    for h in range(heads):
        state = state_ref[h]
        for n in range(chunks):
            o = o_intra[n, h] + lax.dot_general(q_b[n, h], state.astype(BF16), contract_last,
                                                preferred_element_type=F32)
            state = state * decay[n, h] + upd[n, h]
            ms = jnp.mean(o * o, axis=-1, keepdims=True)
            gate = gate_ref[rows(n), vcols(h)]
            y = o * lax.rsqrt(ms + EPS) * og_ref[...] * (gate * jax.nn.sigmoid(gate))
            o_ref[rows(n), vcols(h)] = y.astype(o_ref.dtype)
        state_ref[h] = state


def _gla(qk, v, gate, log_a, o_gain, bsz, seq, rows=256, heads=2):
    t = bsz * seq
    dk = log_a.shape[1] // C_HEADS
    dv = v.shape[1] // C_HEADS
    steps = seq // rows
    groups = C_HEADS // heads

    def row_block(b, n):
        return b * steps + n

    return pl.pallas_call(
        functools.partial(_gla_kernel, chunks=rows // C_CHUNK, heads=heads),
        grid=(bsz, groups, steps),
        in_specs=[pl.BlockSpec((rows, heads * dk), lambda b, g, n: (row_block(b, n), g)),
                  pl.BlockSpec((rows, heads * dk), lambda b, g, n: (row_block(b, n), groups + g)),
                  pl.BlockSpec((rows, heads * dv), lambda b, g, n: (row_block(b, n), g)),
                  pl.BlockSpec((rows, heads * dk), lambda b, g, n: (row_block(b, n), g)),
                  pl.BlockSpec((rows, heads * dv), lambda b, g, n: (row_block(b, n), g)),
                  pl.BlockSpec((1, dv), lambda b, g, n: (0, 0))],
        out_specs=pl.BlockSpec((rows, heads * dv), lambda b, g, n: (row_block(b, n), g)),
        out_shape=jax.ShapeDtypeStruct((t, v.shape[1]), BF16),
        scratch_shapes=[pltpu.VMEM((heads, dv, dk), F32)],
        compiler_params=_params("parallel", "parallel", "arbitrary"),
        name="gla",
    )(qk, qk, v, log_a, gate, o_gain.reshape(1, dv))


def _xattn_kernel(h_ref, g_ref, wq_ref, qg_ref, k_ref, v_ref, wo_ref, fg_ref, o_ref, on_ref):
    x = h_ref[...]
    ms = jnp.mean(x * x, axis=-1, keepdims=True)
    hn = (x * lax.rsqrt(ms + EPS) * g_ref[...]).astype(BF16)
    q = jnp.dot(hn, wq_ref[...], preferred_element_type=F32)
    scale = HEAD_DIM ** -0.5
    contract_last = (((1,), (1,)), ((), ()))
    cols = [slice(hd * HEAD_DIM, (hd + 1) * HEAD_DIM) for hd in range(X_HEADS)]
    q_heads = []
    for sl in cols:
        qh = q[:, sl]
        qms = jnp.mean(qh * qh, axis=-1, keepdims=True)
        q_heads.append((qh * lax.rsqrt(qms + EPS) * qg_ref[...]).astype(BF16))
    scores = [lax.dot_general(qh, k_ref[0, :, sl], contract_last, preferred_element_type=F32) * scale
              for qh, sl in zip(q_heads, cols)]
    probs = []
    for s in scores:
        m = jnp.max(s, axis=-1, keepdims=True)
        p = jnp.exp(s - m)
        probs.append((p / jnp.sum(p, axis=-1, keepdims=True)).astype(BF16))
    heads = [jnp.dot(p, v_ref[0, :, sl], preferred_element_type=F32) for p, sl in zip(probs, cols)]
    o = jnp.concatenate(heads, axis=-1).astype(BF16)
    y = x + jnp.dot(o, wo_ref[...], preferred_element_type=F32)
    o_ref[...] = y
    yms = jnp.mean(y * y, axis=-1, keepdims=True)
    on_ref[...] = (y * lax.rsqrt(yms + EPS) * fg_ref[...]).astype(on_ref.dtype)


def _cross_attention(h, norm_gain, wq, q_gain, k, v, wo, next_gain, seq, tm=256):
    t, d = h.shape
    xw = wq.shape[1]
    mlen = k.shape[1]
    tiles_per_seq = seq // tm
    return pl.pallas_call(
        _xattn_kernel,
        grid=(t // tm,),
        in_specs=[pl.BlockSpec((tm, d), lambda i: (i, 0)),
                  pl.BlockSpec((1, d), lambda i: (0, 0)),
                  pl.BlockSpec((d, xw), lambda i: (0, 0)),
                  pl.BlockSpec((1, HEAD_DIM), lambda i: (0, 0)),
                  pl.BlockSpec((1, mlen, xw), lambda i: (i // tiles_per_seq, 0, 0)),
                  pl.BlockSpec((1, mlen, xw), lambda i: (i // tiles_per_seq, 0, 0)),
                  pl.BlockSpec((xw, d), lambda i: (0, 0)),
                  pl.BlockSpec((1, d), lambda i: (0, 0))],
        out_specs=[pl.BlockSpec((tm, d), lambda i: (i, 0)),
                   pl.BlockSpec((tm, d), lambda i: (i, 0))],
        out_shape=[jax.ShapeDtypeStruct((t, d), F32),
                   jax.ShapeDtypeStruct((t, d), BF16)],
        compiler_params=_params("parallel"),
        name="cross_attention",
    )(h, norm_gain.reshape(1, d), wq, q_gain.reshape(1, HEAD_DIM), k, v, wo, next_gain.reshape(1, d))


def _ffn_up_kernel(a_ref, ah_ref, wg_ref, wv_ref, cwg_ref, cwv_ref, cbg_ref, cbv_ref, wd_ref, o_ref, wdo_ref, *,
                   blocks_per_seq):
    wdo_ref[...] = wd_ref[...].astype(wdo_ref.dtype)
    tm = a_ref.shape[0]
    first = (pl.program_id(0) % blocks_per_seq) == 0
    row = lax.broadcasted_iota(jnp.int32, (tm, 1), 0)
    a = a_ref[...]
    ah = ah_ref[...]

    def conv_half(w_ref, cw_ref, cb_ref):
        w = w_ref[...].astype(BF16)
        u = jnp.dot(a, w, preferred_element_type=F32)
        uh = jnp.where(first, 0.0, jnp.dot(ah, w, preferred_element_type=F32))
        u1 = jnp.where(row == 0, uh[CONV_HALO - 1:CONV_HALO], pltpu.roll(u, 1, axis=0))
        u2 = jnp.where(row == 0, uh[CONV_HALO - 2:CONV_HALO - 1],
                       jnp.where(row == 1, uh[CONV_HALO - 1:CONV_HALO], pltpu.roll(u, 2, axis=0)))
        cw = cw_ref[...]
        return cb_ref[...] + cw[0:1] * u2 + cw[1:2] * u1 + cw[2:3] * u

    cg = conv_half(wg_ref, cwg_ref, cbg_ref)
    cv = conv_half(wv_ref, cwv_ref, cbv_ref)
    o_ref[...] = (cg * jax.nn.sigmoid(cg) * cv).astype(o_ref.dtype)


def _ffn_up(hn, w_up, conv_w, conv_b, w_down, layer, seq, tm=FFN_ROWS, tn=256):
    t, d = hn.shape
    d_ff = w_up.shape[2] // 2
    n_tiles = d_ff // tn
    n_steps = (t // tm) * n_tiles
    slab = d_ff // n_steps
    assert d_ff % n_steps == 0 and slab % (2 * SUBLANES) == 0
    blocks_per_seq = seq // tm
    halo_blocks = tm // CONV_HALO
    cw = conv_w[layer]
    cb = conv_b[layer].reshape(1, 2 * d_ff)
    return pl.pallas_call(
        functools.partial(_ffn_up_kernel, blocks_per_seq=blocks_per_seq),
        grid=(t // tm, n_tiles),
        in_specs=[pl.BlockSpec((tm, d), lambda i, j: (i, 0), pipeline_mode=pl.Buffered(1)),
                  pl.BlockSpec((CONV_HALO, d), lambda i, j: (jnp.maximum(i * halo_blocks - 1, 0), 0)),
                  pl.BlockSpec((None, d, tn), lambda i, j: (layer, 0, j)),
                  pl.BlockSpec((None, d, tn), lambda i, j: (layer, 0, n_tiles + j)),
                  pl.BlockSpec((CONV_WIDTH, tn), lambda i, j: (0, j)),
                  pl.BlockSpec((CONV_WIDTH, tn), lambda i, j: (0, n_tiles + j)),
                  pl.BlockSpec((1, tn), lambda i, j: (0, j)),
                  pl.BlockSpec((1, tn), lambda i, j: (0, n_tiles + j)),
                  pl.BlockSpec((None, slab, d), lambda i, j: (layer, i * n_tiles + j, 0))],
        out_specs=[pl.BlockSpec((tm, tn), lambda i, j: (i, j)),
                   pl.BlockSpec((slab, d), lambda i, j: (i * n_tiles + j, 0))],
        out_shape=[jax.ShapeDtypeStruct((t, d_ff), BF16),
                   jax.ShapeDtypeStruct((d_ff, d), BF16)],
        compiler_params=_params("parallel", "arbitrary"),
        name="ffn_up",
    )(hn, hn, w_up, w_up, cw, cw, cb, cb, w_down)


def _dilated_pool_layer(h, hn, w_in, q_gain, k_gain, pool_w, pool_scale, w_out, j, bsz, seq):
    d = h.shape[1]
    a_width = d // 2
    heads = a_width // HEAD_DIM
    tn = 512
    q_blocks = a_width // tn
    k_base = N_BRANCH * q_blocks
    v_cols = 2 * N_BRANCH * a_width
    gains = jnp.concatenate([jnp.tile(q_gain, heads), jnp.tile(k_gain, heads)]).reshape(1, 2 * a_width)
    dils = tuple(dil for _, dil in A_BRANCHES)
    v_list = _mm([hn], w_in, j, lambda c: v_cols // tn + c, a_width, BF16, dils=dils, seq=seq, tn=tn,
                 name="ab_in_v")
    u = _mm([hn], w_in, j, lambda c: (v_cols + a_width) // tn + c, d - a_width, F32, tn=tn, name="ab_in_u")
    slopes = _alibi_slopes(heads)
    outs, lses = [], []
    for g, dil in enumerate(dils):
        def wcol(c, g=g):
            return jnp.where(c < q_blocks, g * q_blocks + c, k_base + g * q_blocks + c - q_blocks)
        qk = _mm([hn], w_in, j, wcol, 2 * a_width, BF16, mode="headnorm", extra=gains, dils=(dil,), seq=seq,
                 tn=tn, name=f"ab_in_qk{g}")
        if dil == 1:
            qk = qk.reshape(bsz, 1, seq, 2 * a_width)
            v_g = v_list[g].reshape(bsz, 1, seq, a_width)
        else:
            v_g = v_list[g]
        o, lse = _band_attention(qk, v_g, bsz, seq, g, slopes[g])
        outs.append(o)
        lses.append(lse)
    a_out = _combine_branches(outs, lses)
    b_out = _pool_mixer(u, pool_w, pool_scale, seq)
    return _mm([a_out, b_out], w_out, j, lambda c: c, d, F32, mode="residual", extra=h, name="ab_out")


def _gla_layer(h, norm_gain, w_in, w_a2, b_a, o_gain, w_out, j, bsz, seq):
    d = h.shape[1]
    kw = w_a2.shape[2]
    vw = w_out.shape[1]
    tn = 512
    w_in_t = jnp.swapaxes(w_in, 1, 2)
    rank = w_a2.shape[1]
    w_r = jnp.pad(w_in_t[j, 2 * kw + 2 * vw:, :], ((0, LANES - rank), (0, 0)))
    w_a2p = jnp.pad(w_a2[j], ((0, LANES - rank), (0, 0)))
    hn, log_a = _gla_norm_log_decay(h, norm_gain, w_r, w_a2p, b_a[j])
    w_in_t = w_in_t.astype(BF16)
    qk = _mm([hn], w_in_t, j, lambda c: c, 2 * kw, F32, w_t=True, tn=tn, name="c_in_qk")
    v = _mm([hn], w_in_t, j, lambda c: 2 * kw // tn + c, vw, BF16, w_t=True, tn=tn, name="c_in_v")
    gate = _mm([hn], w_in_t, j, lambda c: (2 * kw + vw) // tn + c, vw, F32, w_t=True, tn=tn, name="c_in_gate")
    o = _gla(qk, v, gate, log_a, o_gain[j], bsz, seq)
    return _mm([o], w_out, j, lambda c: c, d, F32, mode="residual", extra=h, name="c_out")


def _memory_kv(mem, gain, wkv, k_gain, layer, bsz):
    xw = wkv.shape[2] // 2
    tn = 256
    mem_n = _rmsnorm(mem, gain)
    rows = mem.shape[0]
    gains = jnp.tile(k_gain, xw // HEAD_DIM).reshape(1, xw)
    k = _mm([mem_n], wkv, layer, lambda c: c, xw, BF16, mode="headnorm", extra=gains, tm=rows, tn=tn, name="mem_k")
    v = _mm([mem_n], wkv, layer, lambda c: xw // tn + c, xw, BF16, tm=rows, tn=tn, name="mem_v")
    return k.reshape(bsz, rows // bsz, xw), v.reshape(bsz, rows // bsz, xw)


def kernel(x, mem, mix_norm, ab_w_in, ab_q_norm, ab_k_norm, ab_pool_w, ab_pool_scale, ab_w_out, c_w_in, c_w_a2, c_b_a, c_o_norm, c_w_out, x_norm, x_mem_norm, x_wq, x_wkv, x_q_norm, x_k_norm, x_wo, f_norm, f_w_up, f_conv_w, f_conv_b, f_w_down):
    bsz, seq, d = x.shape
    depth = mix_norm.shape[0]
    h = x.reshape(bsz * seq, d)
    mem2 = mem.reshape(bsz * mem.shape[1], d)
    for layer in range(depth):
        j = layer // 2
        if layer % 2 == 0:
            hn = _rmsnorm(h, mix_norm[layer])
            h = _dilated_pool_layer(h, hn, ab_w_in, ab_q_norm[j], ab_k_norm[j], ab_pool_w[j], ab_pool_scale[j],
                                    ab_w_out, j, bsz, seq)
        else:
            h = _gla_layer(h, mix_norm[layer], c_w_in, c_w_a2, c_b_a, c_o_norm, c_w_out, j, bsz, seq)
        k, v = _memory_kv(mem2, x_mem_norm[layer], x_wkv, x_k_norm[layer], layer, bsz)
        h, hn = _cross_attention(h, x_norm[layer], x_wq[layer].astype(BF16), x_q_norm[layer], k, v,
                                 x_wo[layer].astype(BF16), f_norm[layer], seq)
        act, w_down = _ffn_up(hn, f_w_up, f_conv_w, f_conv_b, f_w_down, layer, seq)
        h = _mm([act], w_down, None, lambda c: c, d, F32, mode="residual", extra=h, tm=512, name="ffn_down")
    return h.reshape(bsz, seq, d)
```

```python
import functools

import numpy as np
import jax
import jax.numpy as jnp
from jax import lax
from jax.experimental import pallas as pl
from jax.experimental.pallas import tpu as pltpu

F32 = jnp.float32
BF16 = jnp.bfloat16

LANES = 128
SUBLANES = 8
VMEM_LIMIT_BYTES = 56 * 2 ** 20

EPS = 1e-6
HEAD_DIM = 128
A_BRANCHES = ((128, 1), (512, 4), (2048, 16))
N_BRANCH = len(A_BRANCHES)
BAND_BLOCK = 128
POOL_WINDOWS = (2, 4, 8, 16)
POOL_HALO = 16
C_HEADS = 8
C_GATE_RANK = 16
C_GATE_TAU = 16.0
C_CHUNK = 64
X_HEADS = 4
CONV_WIDTH = 3
CONV_HALO = SUBLANES
MM_ROWS = 1024
ATTN_STAGE_HEADS = {1: 8, 4: 16, 16: 1}
ATTN_PAIRS_PER_STEP = 64
FFN_ROWS = 2048


def _params(*semantics):
    return pltpu.CompilerParams(dimension_semantics=semantics, vmem_limit_bytes=VMEM_LIMIT_BYTES)


def _lane_groups(width):
    return [slice(c * LANES, (c + 1) * LANES) for c in range(width // LANES)]


def _rmsnorm_kernel(x_ref, g_ref, o_ref):
    x = x_ref[...].astype(F32)
    ms = jnp.mean(x * x, axis=-1, keepdims=True)
    o_ref[...] = (x * lax.rsqrt(ms + EPS) * g_ref[...]).astype(o_ref.dtype)


def _rmsnorm(x, gain, tm=512):
    t, d = x.shape
    return pl.pallas_call(
        _rmsnorm_kernel,
        grid=(t // tm,),
        in_specs=[pl.BlockSpec((tm, d), lambda i: (i, 0)),
                  pl.BlockSpec((1, d), lambda i: (0, 0))],
        out_specs=pl.BlockSpec((tm, d), lambda i: (i, 0)),
        out_shape=jax.ShapeDtypeStruct((t, d), BF16),
        compiler_params=_params("parallel"),
        name="rmsnorm",
    )(x, gain.reshape(1, d))


def _mm_kernel(*refs, n_a, mode, dils, w_t):
    a_refs = refs[:n_a]
    w_ref = refs[n_a]
    n_extra = 1 if mode in ("headnorm", "residual") else 0
    extra = refs[n_a + 1:n_a + 1 + n_extra]
    o_refs = refs[n_a + 1 + n_extra:n_a + 1 + n_extra + len(dils)]
    scratch = refs[n_a + 1 + n_extra + len(dils):]
    acc = None
    k0 = 0
    for a_ref in a_refs:
        kk = a_ref.shape[1]
        if w_t:
            part = lax.dot_general(a_ref[...], w_ref[:, k0:k0 + kk].astype(BF16), (((1,), (1,)), ((), ())),
                                   preferred_element_type=F32)
        else:
            part = jnp.dot(a_ref[...], w_ref[k0:k0 + kk, :].astype(BF16), preferred_element_type=F32)
        acc = part if acc is None else acc + part
        k0 += kk
    tm, tn = acc.shape
    for c, sl in enumerate(_lane_groups(tn)):
        blk = acc[:, sl]
        if mode == "headnorm":
            ms = jnp.mean(blk * blk, axis=-1, keepdims=True)
            blk = blk * lax.rsqrt(ms + EPS) * extra[0][:, sl]
        elif mode == "residual":
            blk = extra[0][:, sl] + blk
        if scratch:
            scratch[0][c] = blk
        for o_ref, dil in zip(o_refs, dils):
            if dil == 1:
                o_ref[:, sl] = blk.astype(o_ref.dtype)
    for o_ref, dil in zip(o_refs, dils):
        if dil > 1:
            for r in range(dil):
                for c, sl in enumerate(_lane_groups(tn)):
                    o_ref[r, :, sl] = scratch[0][c, pl.ds(r, tm // dil, stride=dil), :].astype(o_ref.dtype)


def _mm(a_list, w, layer, wcol, ncols, out_dtype, *, mode="plain", extra=None, dils=(1,), seq=None,
        w_t=False, tm=MM_ROWS, tn=512, name="mm"):
    t = a_list[0].shape[0]
    k_total = sum(a.shape[1] for a in a_list)
    assert w.shape[-1 if w_t else -2] == k_total and ncols % tn == 0 and t % tm == 0
    in_specs = [pl.BlockSpec((tm, a.shape[1]), lambda i, j: (i, 0)) for a in a_list]
    if w_t:
        in_specs.append(pl.BlockSpec((None, tn, k_total), lambda i, j: (layer, wcol(j), 0)))
    elif w.ndim == 3:
        in_specs.append(pl.BlockSpec((None, k_total, tn), lambda i, j: (layer, 0, wcol(j))))
    else:
        in_specs.append(pl.BlockSpec((k_total, tn), lambda i, j: (0, wcol(j))))
    args = list(a_list) + [w]
    if mode == "headnorm":
        in_specs.append(pl.BlockSpec((1, tn), lambda i, j: (0, j)))
        args.append(extra)
    elif mode == "residual":
        in_specs.append(pl.BlockSpec((tm, tn), lambda i, j: (i, j)))
        args.append(extra)
    out_specs, out_shapes = [], []
    for dil in dils:
        if dil == 1:
            out_specs.append(pl.BlockSpec((tm, tn), lambda i, j: (i, j)))
            out_shapes.append(jax.ShapeDtypeStruct((t, ncols), out_dtype))
        else:
            tiles = seq // tm
            assert seq % tm == 0 and tm % dil == 0
            out_specs.append(pl.BlockSpec((None, dil, tm // dil, tn), lambda i, j: (i // tiles, 0, i % tiles, j)))
            out_shapes.append(jax.ShapeDtypeStruct((t // seq, dil, seq // dil, ncols), out_dtype))
    scratch = [pltpu.VMEM((tn // LANES, tm, LANES), F32)] if any(dil > 1 for dil in dils) else []
    outs = pl.pallas_call(
        functools.partial(_mm_kernel, n_a=len(a_list), mode=mode, dils=tuple(dils), w_t=w_t),
        grid=(t // tm, ncols // tn),
        in_specs=in_specs,
        out_specs=out_specs,
        out_shape=out_shapes,
        scratch_shapes=scratch,
        compiler_params=_params("parallel", "arbitrary"),
        name=name,
    )(*args)
    return outs[0] if len(dils) == 1 else outs


def _alibi_slopes(heads):
    n = N_BRANCH * heads
    s = np.power(np.float32(2.0), -8.0 * np.arange(1, n + 1, dtype=np.float32) / np.float32(n)).astype(np.float32)
    return s.reshape(N_BRANCH, -1)


def _band_attn_kernel(q_ref, kp_ref, kc_ref, vp_ref, vc_ref, o_ref, lse_ref, *scratch, dilation, slopes,
                      stage_heads):
    blk = BAND_BLOCK
    hg = pl.program_id(2)
    has_prev = pl.program_id(1) > 0
    n_heads = len(slopes[0])
    qi = lax.broadcasted_iota(jnp.int32, (blk, 2 * blk), 0)
    ki = lax.broadcasted_iota(jnp.int32, (blk, 2 * blk), 1)
    rel = qi + blk - ki
    valid = jnp.logical_and(jnp.logical_and(rel >= 0, rel <= blk), jnp.logical_or(ki >= blk, has_prev))
    dist = (rel * dilation).astype(F32)
    lane = lax.broadcasted_iota(jnp.int32, (blk, LANES), 1)
    scale = HEAD_DIM ** -0.5
    contract_last = (((1,), (1,)), ((), ()))
    head_cols = [slice(h * HEAD_DIM, (h + 1) * HEAD_DIM) for h in range(n_heads)]
    head_slope = []
    for h in range(n_heads):
        slope = slopes[0][h]
        for g in range(1, len(slopes)):
            slope = jnp.where(hg == g, slopes[g][h], slope)
        head_slope.append(slope)
    for r in range(dilation):
        lse_tile = jnp.zeros((blk, LANES), F32)
        for h0 in range(0, n_heads, stage_heads):
            group = list(range(h0, min(h0 + stage_heads, n_heads)))
            scores = {h: lax.dot_general(q_ref[r, :, head_cols[h]],
                                         jnp.concatenate([kp_ref[r, :, head_cols[h]], kc_ref[r, :, head_cols[h]]],
                                                         axis=0),
                                         contract_last, preferred_element_type=F32) for h in group}
            probs, dens, maxes = {}, {}, {}
            for h in group:
                s = jnp.where(valid, scores[h] * scale - head_slope[h] * dist, -jnp.inf)
                maxes[h] = jnp.max(s, axis=-1, keepdims=True)
                p = jnp.exp(s - maxes[h])
                dens[h] = jnp.sum(p, axis=-1, keepdims=True)
                probs[h] = p.astype(BF16)
            outs = {h: jnp.dot(probs[h],
                               jnp.concatenate([vp_ref[r, :, head_cols[h]], vc_ref[r, :, head_cols[h]]], axis=0),
                               preferred_element_type=F32) for h in group}
            for h in group:
                o = outs[h] / dens[h]
                lse_h = maxes[h] + jnp.log(dens[h])
                for g in range(len(slopes)):
                    lse_tile = jnp.where(jnp.logical_and(lane == g * n_heads + h, hg == g), lse_h, lse_tile)
                if dilation == 1:
                    o_ref[:, head_cols[h]] = o.astype(o_ref.dtype)
                else:
                    scratch[0][h, pl.ds(r, blk, stride=dilation), :] = o
        if dilation == 1:
            lse_slab = lse_tile
        else:
            scratch[1][pl.ds(r, blk, stride=dilation), :] = lse_tile
    if dilation > 1:
        for h in range(n_heads):
            o_ref[:, h * HEAD_DIM:(h + 1) * HEAD_DIM] = scratch[0][h].astype(o_ref.dtype)
        lse_slab = scratch[1][...]

    @pl.when(hg == 0)
    def _():
        lse_ref[...] = lse_slab

    @pl.when(hg > 0)
    def _():
        lse_ref[...] += lse_slab


def _band_attention(qk, v, bsz, seq, branch, slopes):
    _, dilation = A_BRANCHES[branch]
    width = v.shape[-1]
    n_blk = seq // dilation // BAND_BLOCK
    heads = max(min(width // HEAD_DIM, ATTN_PAIRS_PER_STEP // dilation), 1)
    hw = heads * HEAD_DIM
    n_hg = width // hw
    slope_tab = tuple(tuple(float(s) for s in slopes[g * heads:(g + 1) * heads]) for g in range(n_hg))
    rows = BAND_BLOCK * dilation
    blk = (None, dilation, BAND_BLOCK, hw)

    def prev(n):
        return jnp.maximum(n - 1, 0)

    scratch = []
    if dilation > 1:
        scratch = [pltpu.VMEM((heads, rows, HEAD_DIM), F32), pltpu.VMEM((rows, LANES), F32)]
    return pl.pallas_call(
        functools.partial(_band_attn_kernel, dilation=dilation, slopes=slope_tab,
                          stage_heads=ATTN_STAGE_HEADS[dilation]),
        grid=(bsz, n_blk, n_hg),
        in_specs=[pl.BlockSpec(blk, lambda b, n, g: (b, 0, n, g)),
                  pl.BlockSpec(blk, lambda b, n, g: (b, 0, prev(n), n_hg + g)),
                  pl.BlockSpec(blk, lambda b, n, g: (b, 0, n, n_hg + g)),
                  pl.BlockSpec(blk, lambda b, n, g: (b, 0, prev(n), g)),
                  pl.BlockSpec(blk, lambda b, n, g: (b, 0, n, g))],
        out_specs=[pl.BlockSpec((rows, hw), lambda b, n, g: (b * n_blk + n, g)),
                   pl.BlockSpec((rows, LANES), lambda b, n, g: (b * n_blk + n, 0))],
        out_shape=[jax.ShapeDtypeStruct((bsz * seq, width), BF16),
                   jax.ShapeDtypeStruct((bsz * seq, LANES), F32)],
        scratch_shapes=scratch,
        compiler_params=_params("parallel", "arbitrary", "arbitrary"),
        name=f"band_attn_d{dilation}",
    )(qk, qk, qk, v, v)


def _combine_kernel(o0_ref, o1_ref, o2_ref, l0_ref, l1_ref, l2_ref, out_ref):
    l0, l1, l2 = l0_ref[...], l1_ref[...], l2_ref[...]
    m = jnp.maximum(jnp.maximum(l0, l1), l2)
    e0, e1, e2 = jnp.exp(l0 - m), jnp.exp(l1 - m), jnp.exp(l2 - m)
    tot = e0 + e1 + e2
    w0, w1, w2 = e0 / tot, e1 / tot, e2 / tot
    for h in range(out_ref.shape[1] // HEAD_DIM):
        sl = slice(h * HEAD_DIM, (h + 1) * HEAD_DIM)
        acc = (w0[:, h:h + 1] * o0_ref[:, sl].astype(F32) + w1[:, h:h + 1] * o1_ref[:, sl].astype(F32)
               + w2[:, h:h + 1] * o2_ref[:, sl].astype(F32))
        out_ref[:, sl] = acc.astype(out_ref.dtype)


def _combine_branches(outs, lses, tm=512):
    t, width = outs[0].shape
    o_spec = pl.BlockSpec((tm, width), lambda i: (i, 0))
    l_spec = pl.BlockSpec((tm, LANES), lambda i: (i, 0))
    return pl.pallas_call(
        _combine_kernel,
        grid=(t // tm,),
        in_specs=[o_spec] * 3 + [l_spec] * 3,
        out_specs=o_spec,
        out_shape=jax.ShapeDtypeStruct((t, width), BF16),
        compiler_params=_params("parallel"),
        name="combine_branches",
    )(*outs, *lses)


def _pool_kernel(u_ref, uh_ref, w_ref, sc_ref, o_ref, wb_ref, *, blocks_per_seq):
    i = pl.program_id(0)

    @pl.when(i == 0)
    def _():
        wb_ref[...] = w_ref[...].astype(BF16)

    tm = u_ref.shape[0]
    group = w_ref.shape[1]
    first = (i % blocks_per_seq) == 0
    row = lax.broadcasted_iota(jnp.int32, (tm, 1), 0)
    pos = (i % blocks_per_seq) * tm + row
    for g, win in enumerate(POOL_WINDOWS):
        sl = slice(g * group, (g + 1) * group)
        u = u_ref[:, sl]
        halo = jnp.where(first, 0.0, uh_ref[:, sl])
        s = jnp.concatenate([halo, u], axis=0)
        step = 1
        while step < win:
            s = s + pltpu.roll(s, step, axis=0)
            step *= 2
        count = jnp.minimum(pos + 1, win).astype(F32)
        pooled = s[POOL_HALO:] / count - u
        y = jnp.dot(pooled.astype(BF16), wb_ref[g], preferred_element_type=F32)
        o_ref[:, sl] = (y * sc_ref[:, sl]).astype(o_ref.dtype)


def _pool_mixer(u, pool_w, pool_scale, seq, tm=1024):
    t, width = u.shape
    n_group, group, _ = pool_w.shape
    blocks_per_seq = seq // tm
    halo_blocks = tm // POOL_HALO
    return pl.pallas_call(
        functools.partial(_pool_kernel, blocks_per_seq=blocks_per_seq),
        grid=(t // tm,),
        in_specs=[pl.BlockSpec((tm, width), lambda i: (i, 0)),
                  pl.BlockSpec((POOL_HALO, width), lambda i: (jnp.maximum(i * halo_blocks - 1, 0), 0)),
                  pl.BlockSpec((n_group, group, group), lambda i: (0, 0, 0)),
                  pl.BlockSpec((1, width), lambda i: (0, 0))],
        out_specs=pl.BlockSpec((tm, width), lambda i: (i, 0)),
        out_shape=jax.ShapeDtypeStruct((t, width), BF16),
        scratch_shapes=[pltpu.VMEM((n_group, group, group), BF16)],
        compiler_params=_params("arbitrary"),
        name="pool_mixer",
    )(u, u, pool_w, pool_scale.reshape(1, width))


def _split3(x):
    hi = x.astype(BF16)
    r1 = x - hi.astype(F32)
    mid = r1.astype(BF16)
    lo = (r1 - mid.astype(F32)).astype(BF16)
    return hi, mid, lo


def _gate_kernel(h_ref, ng_ref, wr_ref, wa2_ref, ba_ref, hn_ref, o_ref):
    x = h_ref[...]
    ms = jnp.mean(x * x, axis=-1, keepdims=True)
    hn = (x * lax.rsqrt(ms + EPS) * ng_ref[...]).astype(hn_ref.dtype)
    hn_ref[...] = hn
    r = lax.dot_general(hn, wr_ref[...].astype(BF16), (((1,), (1,)), ((), ())), preferred_element_type=F32)
    r_hi, r_mid, _ = _split3(r)
    w_hi, w_mid, _ = _split3(wa2_ref[...])
    g = (jnp.dot(r_hi, w_hi, preferred_element_type=F32)
         + (jnp.dot(r_hi, w_mid, preferred_element_type=F32) + jnp.dot(r_mid, w_hi, preferred_element_type=F32)))
    g = g + ba_ref[...]
    log_sig = jnp.minimum(g, 0.0) - jnp.log1p(jnp.exp(-jnp.abs(g)))
    o_ref[...] = log_sig / C_GATE_TAU


def _gla_norm_log_decay(h, norm_gain, w_r, w_a2, b_a, tm=512):
    t, d = h.shape
    kw = w_a2.shape[1]
    return pl.pallas_call(
        _gate_kernel,
        grid=(t // tm,),
        in_specs=[pl.BlockSpec((tm, d), lambda i: (i, 0)),
                  pl.BlockSpec((1, d), lambda i: (0, 0)),
                  pl.BlockSpec((LANES, d), lambda i: (0, 0)),
                  pl.BlockSpec((LANES, kw), lambda i: (0, 0)),
                  pl.BlockSpec((1, kw), lambda i: (0, 0))],
        out_specs=[pl.BlockSpec((tm, d), lambda i: (i, 0)),
                   pl.BlockSpec((tm, kw), lambda i: (i, 0))],
        out_shape=[jax.ShapeDtypeStruct((t, d), BF16),
                   jax.ShapeDtypeStruct((t, kw), F32)],
        compiler_params=_params("parallel"),
        name="gla_norm_log_decay",
    )(h, norm_gain.reshape(1, d), w_r, w_a2, b_a.reshape(1, kw))


def _gla_kernel(q_ref, k_ref, v_ref, la_ref, gate_ref, og_ref, o_ref, state_ref, *, chunks, heads):
    @pl.when(pl.program_id(2) == 0)
    def _():
        state_ref[...] = jnp.zeros_like(state_ref)

    c = C_CHUNK
    dk = q_ref.shape[1] // heads
    dv = v_ref.shape[1] // heads
    ri = lax.broadcasted_iota(jnp.int32, (c, c), 0)
    ci = lax.broadcasted_iota(jnp.int32, (c, c), 1)
    causal = ci <= ri
    tri = causal.astype(BF16)
    contract_last = (((1,), (1,)), ((), ()))
    contract_first = (((0,), (0,)), ((), ()))
    pairs = [(n, h) for n in range(chunks) for h in range(heads)]

    def rows(n):
        return slice(n * c, (n + 1) * c)

    def kcols(h):
        return slice(h * dk, (h + 1) * dk)

    def vcols(h):
        return slice(h * dv, (h + 1) * dv)

    bc = {}
    for n, h in pairs:
        la_hi, la_mid, la_lo = _split3(la_ref[rows(n), kcols(h)])
        bc[n, h] = (jnp.dot(tri, la_hi, preferred_element_type=F32)
                    + jnp.dot(tri, la_mid, preferred_element_type=F32)
                    + jnp.dot(tri, la_lo, preferred_element_type=F32))
    q_b, k_in, k_st, decay = {}, {}, {}, {}
    for n, h in pairs:
        b = bc[n, h]
        b_last = b[c - 1:c, :]
        q_b[n, h] = (q_ref[rows(n), kcols(h)] * (dk ** -0.5) * jnp.exp(b)).astype(BF16)
        k = k_ref[rows(n), kcols(h)]
        k_in[n, h] = (k * jnp.exp(-b)).astype(BF16)
        k_st[n, h] = (k * jnp.exp(b_last - b)).astype(BF16)
        decay[n, h] = jnp.exp(b_last)
    att = {}
    for p in pairs:
        a = lax.dot_general(q_b[p], k_in[p], contract_last, preferred_element_type=F32)
        att[p] = jnp.where(causal, a, 0.0).astype(BF16)
    o_intra, upd = {}, {}
    for n, h in pairs:
        v = v_ref[rows(n), vcols(h)]
        o_intra[n, h] = jnp.dot(att[n, h], v, preferred_element_type=F32)
        upd[n, h] = lax.dot_general(v, k_st[n, h], contract_first, preferred_element_type=F32)
    for h in range(heads):
        state = state_ref[h]
        for n in range(chunks):
            o = o_intra[n, h] + lax.dot_general(q_b[n, h], state.astype(BF16), contract_last,
                                                preferred_element_type=F32)
            state = state * decay[n, h] + upd[n, h]
            ms = jnp.mean(o * o, axis=-1, keepdims=True)
            gate = gate_ref[rows(n), vcols(h)]
            y = o * lax.rsqrt(ms + EPS) * og_ref[...] * (gate * jax.nn.sigmoid(gate))
            o_ref[rows(n), vcols(h)] = y.astype(o_ref.dtype)
        state_ref[h] = state


def _gla(qk, v, gate, log_a, o_gain, bsz, seq, rows=512, heads=2):
    t = bsz * seq
    dk = log_a.shape[1] // C_HEADS
    dv = v.shape[1] // C_HEADS
    steps = seq // rows
    groups = C_HEADS // heads

    def row_block(b, n):
        return b * steps + n

    return pl.pallas_call(
        functools.partial(_gla_kernel, chunks=rows // C_CHUNK, heads=heads),
        grid=(bsz, groups, steps),
        in_specs=[pl.BlockSpec((rows, heads * dk), lambda b, g, n: (row_block(b, n), g)),
                  pl.BlockSpec((rows, heads * dk), lambda b, g, n: (row_block(b, n), groups + g)),
                  pl.BlockSpec((rows, heads * dv), lambda b, g, n: (row_block(b, n), g)),
                  pl.BlockSpec((rows, heads * dk), lambda b, g, n: (row_block(b, n), g)),
                  pl.BlockSpec((rows, heads * dv), lambda b, g, n: (row_block(b, n), g)),
                  pl.BlockSpec((1, dv), lambda b, g, n: (0, 0))],
        out_specs=pl.BlockSpec((rows, heads * dv), lambda b, g, n: (row_block(b, n), g)),
        out_shape=jax.ShapeDtypeStruct((t, v.shape[1]), BF16),
        scratch_shapes=[pltpu.VMEM((heads, dv, dk), F32)],
        compiler_params=_params("parallel", "parallel", "arbitrary"),
        name="gla",
    )(qk, qk, v, log_a, gate, o_gain.reshape(1, dv))


def _xattn_kernel(h_ref, g_ref, wq_ref, qg_ref, k_ref, v_ref, wo_ref, fg_ref, o_ref, on_ref):
    x = h_ref[...]
    ms = jnp.mean(x * x, axis=-1, keepdims=True)
    hn = (x * lax.rsqrt(ms + EPS) * g_ref[...]).astype(BF16)
    q = jnp.dot(hn, wq_ref[...], preferred_element_type=F32)
    scale = HEAD_DIM ** -0.5
    contract_last = (((1,), (1,)), ((), ()))
    cols = [slice(hd * HEAD_DIM, (hd + 1) * HEAD_DIM) for hd in range(X_HEADS)]
    q_heads = []
    for sl in cols:
        qh = q[:, sl]
        qms = jnp.mean(qh * qh, axis=-1, keepdims=True)
        q_heads.append((qh * lax.rsqrt(qms + EPS) * qg_ref[...]).astype(BF16))
    scores = [lax.dot_general(qh, k_ref[0, :, sl], contract_last, preferred_element_type=F32) * scale
              for qh, sl in zip(q_heads, cols)]
    probs = []
    for s in scores:
        m = jnp.max(s, axis=-1, keepdims=True)
        p = jnp.exp(s - m)
        probs.append((p / jnp.sum(p, axis=-1, keepdims=True)).astype(BF16))
    heads = [jnp.dot(p, v_ref[0, :, sl], preferred_element_type=F32) for p, sl in zip(probs, cols)]
    o = jnp.concatenate(heads, axis=-1).astype(BF16)
    y = x + jnp.dot(o, wo_ref[...], preferred_element_type=F32)
    o_ref[...] = y
    yms = jnp.mean(y * y, axis=-1, keepdims=True)
    on_ref[...] = (y * lax.rsqrt(yms + EPS) * fg_ref[...]).astype(on_ref.dtype)


def _cross_attention(h, norm_gain, wq, q_gain, k, v, wo, next_gain, seq, tm=256):
    t, d = h.shape
    xw = wq.shape[1]
    mlen = k.shape[1]
    tiles_per_seq = seq // tm
    return pl.pallas_call(
        _xattn_kernel,
        grid=(t // tm,),
        in_specs=[pl.BlockSpec((tm, d), lambda i: (i, 0)),
                  pl.BlockSpec((1, d), lambda i: (0, 0)),
                  pl.BlockSpec((d, xw), lambda i: (0, 0)),
                  pl.BlockSpec((1, HEAD_DIM), lambda i: (0, 0)),
                  pl.BlockSpec((1, mlen, xw), lambda i: (i // tiles_per_seq, 0, 0)),
                  pl.BlockSpec((1, mlen, xw), lambda i: (i // tiles_per_seq, 0, 0)),
                  pl.BlockSpec((xw, d), lambda i: (0, 0)),
                  pl.BlockSpec((1, d), lambda i: (0, 0))],
        out_specs=[pl.BlockSpec((tm, d), lambda i: (i, 0)),
                   pl.BlockSpec((tm, d), lambda i: (i, 0))],
        out_shape=[jax.ShapeDtypeStruct((t, d), F32),
                   jax.ShapeDtypeStruct((t, d), BF16)],
        compiler_params=_params("parallel"),
        name="cross_attention",
    )(h, norm_gain.reshape(1, d), wq, q_gain.reshape(1, HEAD_DIM), k, v, wo, next_gain.reshape(1, d))


def _ffn_up_kernel(a_ref, ah_ref, wg_ref, wv_ref, cwg_ref, cwv_ref, cbg_ref, cbv_ref, wd_ref, o_ref, wdo_ref, *,
                   blocks_per_seq):
    wdo_ref[...] = wd_ref[...].astype(wdo_ref.dtype)
    tm = a_ref.shape[0]
    first = (pl.program_id(0) % blocks_per_seq) == 0
    row = lax.broadcasted_iota(jnp.int32, (tm, 1), 0)
    a = a_ref[...]
    ah = ah_ref[...]

    def conv_half(w_ref, cw_ref, cb_ref):
        w = w_ref[...].astype(BF16)
        u = jnp.dot(a, w, preferred_element_type=F32)
        uh = jnp.where(first, 0.0, jnp.dot(ah, w, preferred_element_type=F32))
        u1 = jnp.where(row == 0, uh[CONV_HALO - 1:CONV_HALO], pltpu.roll(u, 1, axis=0))
        u2 = jnp.where(row == 0, uh[CONV_HALO - 2:CONV_HALO - 1],
                       jnp.where(row == 1, uh[CONV_HALO - 1:CONV_HALO], pltpu.roll(u, 2, axis=0)))
        cw = cw_ref[...]
        return cb_ref[...] + cw[0:1] * u2 + cw[1:2] * u1 + cw[2:3] * u

    cg = conv_half(wg_ref, cwg_ref, cbg_ref)
    cv = conv_half(wv_ref, cwv_ref, cbv_ref)
    o_ref[...] = (cg * jax.nn.sigmoid(cg) * cv).astype(o_ref.dtype)


def _ffn_up(hn, w_up, conv_w, conv_b, w_down, layer, seq, tm=FFN_ROWS, tn=256):
    t, d = hn.shape
    d_ff = w_up.shape[2] // 2
    n_tiles = d_ff // tn
    n_steps = (t // tm) * n_tiles
    slab = d_ff // n_steps
    assert d_ff % n_steps == 0 and slab % (2 * SUBLANES) == 0
    blocks_per_seq = seq // tm
    halo_blocks = tm // CONV_HALO
    cw = conv_w[layer]
    cb = conv_b[layer].reshape(1, 2 * d_ff)
    return pl.pallas_call(
        functools.partial(_ffn_up_kernel, blocks_per_seq=blocks_per_seq),
        grid=(t // tm, n_tiles),
        in_specs=[pl.BlockSpec((tm, d), lambda i, j: (i, 0), pipeline_mode=pl.Buffered(1)),
                  pl.BlockSpec((CONV_HALO, d), lambda i, j: (jnp.maximum(i * halo_blocks - 1, 0), 0)),
                  pl.BlockSpec((None, d, tn), lambda i, j: (layer, 0, j)),
                  pl.BlockSpec((None, d, tn), lambda i, j: (layer, 0, n_tiles + j)),
                  pl.BlockSpec((CONV_WIDTH, tn), lambda i, j: (0, j)),
                  pl.BlockSpec((CONV_WIDTH, tn), lambda i, j: (0, n_tiles + j)),
                  pl.BlockSpec((1, tn), lambda i, j: (0, j)),
                  pl.BlockSpec((1, tn), lambda i, j: (0, n_tiles + j)),
                  pl.BlockSpec((None, slab, d), lambda i, j: (layer, i * n_tiles + j, 0))],
        out_specs=[pl.BlockSpec((tm, tn), lambda i, j: (i, j)),
                   pl.BlockSpec((slab, d), lambda i, j: (i * n_tiles + j, 0))],
        out_shape=[jax.ShapeDtypeStruct((t, d_ff), BF16),
                   jax.ShapeDtypeStruct((d_ff, d), BF16)],
        compiler_params=_params("parallel", "arbitrary"),
        name="ffn_up",
    )(hn, hn, w_up, w_up, cw, cw, cb, cb, w_down)


def _dilated_pool_layer(h, hn, w_in, q_gain, k_gain, pool_w, pool_scale, w_out, j, bsz, seq):
    d = h.shape[1]
    a_width = d // 2
    heads = a_width // HEAD_DIM
    tn = 512
    q_blocks = a_width // tn
    k_base = N_BRANCH * q_blocks
    v_cols = 2 * N_BRANCH * a_width
    gains = jnp.concatenate([jnp.tile(q_gain, heads), jnp.tile(k_gain, heads)]).reshape(1, 2 * a_width)
    dils = tuple(dil for _, dil in A_BRANCHES)
    v_list = _mm([hn], w_in, j, lambda c: v_cols // tn + c, a_width, BF16, dils=dils, seq=seq, tn=tn,
                 name="ab_in_v")
    u = _mm([hn], w_in, j, lambda c: (v_cols + a_width) // tn + c, d - a_width, F32, tn=tn, name="ab_in_u")
    slopes = _alibi_slopes(heads)
    outs, lses = [], []
    for g, dil in enumerate(dils):
        def wcol(c, g=g):
            return jnp.where(c < q_blocks, g * q_blocks + c, k_base + g * q_blocks + c - q_blocks)
        qk = _mm([hn], w_in, j, wcol, 2 * a_width, BF16, mode="headnorm", extra=gains, dils=(dil,), seq=seq,
                 tn=tn, name=f"ab_in_qk{g}")
        if dil == 1:
            qk = qk.reshape(bsz, 1, seq, 2 * a_width)
            v_g = v_list[g].reshape(bsz, 1, seq, a_width)
        else:
            v_g = v_list[g]
        o, lse = _band_attention(qk, v_g, bsz, seq, g, slopes[g])
        outs.append(o)
        lses.append(lse)
    a_out = _combine_branches(outs, lses)
    b_out = _pool_mixer(u, pool_w, pool_scale, seq)
    return _mm([a_out, b_out], w_out, j, lambda c: c, d, F32, mode="residual", extra=h, name="ab_out")


def _gla_layer(h, norm_gain, w_in, w_a2, b_a, o_gain, w_out, j, bsz, seq):
    d = h.shape[1]
    kw = w_a2.shape[2]
    vw = w_out.shape[1]
    tn = 512
    w_in_t = jnp.swapaxes(w_in, 1, 2)
    rank = w_a2.shape[1]
    w_r = jnp.pad(w_in_t[j, 2 * kw + 2 * vw:, :], ((0, LANES - rank), (0, 0)))
    w_a2p = jnp.pad(w_a2[j], ((0, LANES - rank), (0, 0)))
    hn, log_a = _gla_norm_log_decay(h, norm_gain, w_r, w_a2p, b_a[j])
    qk = _mm([hn], w_in_t, j, lambda c: c, 2 * kw, F32, w_t=True, tn=tn, name="c_in_qk")
    v = _mm([hn], w_in_t, j, lambda c: 2 * kw // tn + c, vw, BF16, w_t=True, tn=tn, name="c_in_v")
    gate = _mm([hn], w_in_t, j, lambda c: (2 * kw + vw) // tn + c, vw, F32, w_t=True, tn=tn, name="c_in_gate")
    o = _gla(qk, v, gate, log_a, o_gain[j], bsz, seq)
    return _mm([o], w_out, j, lambda c: c, d, F32, mode="residual", extra=h, name="c_out")


def _memory_kv(mem, gain, wkv, k_gain, layer, bsz):
    xw = wkv.shape[2] // 2
    tn = 256
    mem_n = _rmsnorm(mem, gain)
    rows = mem.shape[0]
    gains = jnp.tile(k_gain, xw // HEAD_DIM).reshape(1, xw)
    k = _mm([mem_n], wkv, layer, lambda c: c, xw, BF16, mode="headnorm", extra=gains, tm=rows, tn=tn, name="mem_k")
    v = _mm([mem_n], wkv, layer, lambda c: xw // tn + c, xw, BF16, tm=rows, tn=tn, name="mem_v")
    return k.reshape(bsz, rows // bsz, xw), v.reshape(bsz, rows // bsz, xw)


def kernel(x, mem, mix_norm, ab_w_in, ab_q_norm, ab_k_norm, ab_pool_w, ab_pool_scale, ab_w_out, c_w_in, c_w_a2, c_b_a, c_o_norm, c_w_out, x_norm, x_mem_norm, x_wq, x_wkv, x_q_norm, x_k_norm, x_wo, f_norm, f_w_up, f_conv_w, f_conv_b, f_w_down):
    bsz, seq, d = x.shape
    depth = mix_norm.shape[0]
    h = x.reshape(bsz * seq, d)
    mem2 = mem.reshape(bsz * mem.shape[1], d)
    for layer in range(depth):
        j = layer // 2
        if layer % 2 == 0:
            hn = _rmsnorm(h, mix_norm[layer])
            h = _dilated_pool_layer(h, hn, ab_w_in, ab_q_norm[j], ab_k_norm[j], ab_pool_w[j], ab_pool_scale[j],
                                    ab_w_out, j, bsz, seq)
        else:
            h = _gla_layer(h, mix_norm[layer], c_w_in, c_w_a2, c_b_a, c_o_norm, c_w_out, j, bsz, seq)
        k, v = _memory_kv(mem2, x_mem_norm[layer], x_wkv, x_k_norm[layer], layer, bsz)
        h, hn = _cross_attention(h, x_norm[layer], x_wq[layer].astype(BF16), x_q_norm[layer], k, v,
                                 x_wo[layer].astype(BF16), f_norm[layer], seq)
        act, w_down = _ffn_up(hn, f_w_up, f_conv_w, f_conv_b, f_w_down, layer, seq)
        h = _mm([act], w_down, None, lambda c: c, d, F32, mode="residual", extra=h, tm=512, name="ffn_down")
    return h.reshape(bsz, seq, d)
```

```python
import functools

import numpy as np
import jax
import jax.numpy as jnp
from jax import lax
from jax.experimental import pallas as pl
from jax.experimental.pallas import tpu as pltpu

F32 = jnp.float32
BF16 = jnp.bfloat16

LANES = 128
SUBLANES = 8
VMEM_LIMIT_BYTES = 56 * 2 ** 20

EPS = 1e-6
HEAD_DIM = 128
A_BRANCHES = ((128, 1), (512, 4), (2048, 16))
N_BRANCH = len(A_BRANCHES)
BAND_BLOCK = 128
POOL_WINDOWS = (2, 4, 8, 16)
POOL_HALO = 16
C_HEADS = 8
C_GATE_RANK = 16
C_GATE_TAU = 16.0
C_CHUNK = 64
X_HEADS = 4
CONV_WIDTH = 3
CONV_HALO = SUBLANES
MM_ROWS = 1024
FAST_ROW_STRIDE = 4
ATTN_STAGE_HEADS = {1: 8, 4: 16, 16: 1}
ATTN_PAIRS_PER_STEP = 64
FFN_ROWS = 2048


def _params(*semantics):
    return pltpu.CompilerParams(dimension_semantics=semantics, vmem_limit_bytes=VMEM_LIMIT_BYTES)


def _lane_groups(width):
    return [slice(c * LANES, (c + 1) * LANES) for c in range(width // LANES)]


def _rmsnorm_kernel(x_ref, g_ref, o_ref):
    x = x_ref[...].astype(F32)
    ms = jnp.mean(x * x, axis=-1, keepdims=True)
    o_ref[...] = (x * lax.rsqrt(ms + EPS) * g_ref[...]).astype(o_ref.dtype)


def _rmsnorm(x, gain, tm=512):
    t, d = x.shape
    return pl.pallas_call(
        _rmsnorm_kernel,
        grid=(t // tm,),
        in_specs=[pl.BlockSpec((tm, d), lambda i: (i, 0)),
                  pl.BlockSpec((1, d), lambda i: (0, 0))],
        out_specs=pl.BlockSpec((tm, d), lambda i: (i, 0)),
        out_shape=jax.ShapeDtypeStruct((t, d), BF16),
        compiler_params=_params("parallel"),
        name="rmsnorm",
    )(x, gain.reshape(1, d))


def _mm_kernel(*refs, n_a, mode, dils, w_t):
    a_refs = refs[:n_a]
    w_ref = refs[n_a]
    n_extra = 1 if mode in ("headnorm", "residual") else 0
    extra = refs[n_a + 1:n_a + 1 + n_extra]
    o_refs = refs[n_a + 1 + n_extra:n_a + 1 + n_extra + len(dils)]
    scratch = refs[n_a + 1 + n_extra + len(dils):]
    acc = None
    k0 = 0
    for a_ref in a_refs:
        kk = a_ref.shape[1]
        if w_t:
            part = lax.dot_general(a_ref[...], w_ref[:, k0:k0 + kk].astype(BF16), (((1,), (1,)), ((), ())),
                                   preferred_element_type=F32)
        else:
            part = jnp.dot(a_ref[...], w_ref[k0:k0 + kk, :].astype(BF16), preferred_element_type=F32)
        acc = part if acc is None else acc + part
        k0 += kk
    tm, tn = acc.shape
    for c, sl in enumerate(_lane_groups(tn)):
        blk = acc[:, sl]
        if mode == "headnorm":
            ms = jnp.mean(blk * blk, axis=-1, keepdims=True)
            blk = blk * lax.rsqrt(ms + EPS) * extra[0][:, sl]
        elif mode == "residual":
            blk = extra[0][:, sl] + blk
        if scratch:
            scratch[0][c] = blk
        for o_ref, dil in zip(o_refs, dils):
            if dil == 1:
                o_ref[:, sl] = blk.astype(o_ref.dtype)
    for o_ref, dil in zip(o_refs, dils):
        if 1 < dil <= FAST_ROW_STRIDE:
            for r in range(dil):
                for c, sl in enumerate(_lane_groups(tn)):
                    o_ref[r, :, sl] = scratch[0][c, pl.ds(r, tm // dil, stride=dil), :].astype(o_ref.dtype)
        elif dil > FAST_ROW_STRIDE:
            s1, s2 = FAST_ROW_STRIDE, dil // FAST_ROW_STRIDE
            part = tm // s1
            for q in range(s1):
                for c in range(tn // LANES):
                    scratch[1][c, q * part:(q + 1) * part, :] = scratch[0][c, pl.ds(q, part, stride=s1), :]
            for q in range(s1):
                for p in range(s2):
                    for c, sl in enumerate(_lane_groups(tn)):
                        o_ref[s1 * p + q, :, sl] = scratch[1][c, pl.ds(q * part + p, tm // dil, stride=s2),
                                                              :].astype(o_ref.dtype)


def _mm(a_list, w, layer, wcol, ncols, out_dtype, *, mode="plain", extra=None, dils=(1,), seq=None,
        w_t=False, tm=MM_ROWS, tn=512, name="mm"):
    t = a_list[0].shape[0]
    k_total = sum(a.shape[1] for a in a_list)
    assert w.shape[-1 if w_t else -2] == k_total and ncols % tn == 0 and t % tm == 0
    in_specs = [pl.BlockSpec((tm, a.shape[1]), lambda i, j: (i, 0)) for a in a_list]
    if w_t:
        in_specs.append(pl.BlockSpec((None, tn, k_total), lambda i, j: (layer, wcol(j), 0)))
    elif w.ndim == 3:
        in_specs.append(pl.BlockSpec((None, k_total, tn), lambda i, j: (layer, 0, wcol(j))))
    else:
        in_specs.append(pl.BlockSpec((k_total, tn), lambda i, j: (0, wcol(j))))
    args = list(a_list) + [w]
    if mode == "headnorm":
        in_specs.append(pl.BlockSpec((1, tn), lambda i, j: (0, j)))
        args.append(extra)
    elif mode == "residual":
        in_specs.append(pl.BlockSpec((tm, tn), lambda i, j: (i, j)))
        args.append(extra)
    out_specs, out_shapes = [], []
    for dil in dils:
        if dil == 1:
            out_specs.append(pl.BlockSpec((tm, tn), lambda i, j: (i, j)))
            out_shapes.append(jax.ShapeDtypeStruct((t, ncols), out_dtype))
        else:
            tiles = seq // tm
            assert seq % tm == 0 and tm % dil == 0
            out_specs.append(pl.BlockSpec((None, dil, tm // dil, tn), lambda i, j: (i // tiles, 0, i % tiles, j)))
            out_shapes.append(jax.ShapeDtypeStruct((t // seq, dil, seq // dil, ncols), out_dtype))
    assert all(dil <= FAST_ROW_STRIDE ** 2 and (dil <= FAST_ROW_STRIDE or dil % FAST_ROW_STRIDE == 0) for dil in dils)
    slab = pltpu.VMEM((tn // LANES, tm, LANES), F32)
    n_slabs = 2 if max(dils) > FAST_ROW_STRIDE else 1 if max(dils) > 1 else 0
    scratch = [slab] * n_slabs
    outs = pl.pallas_call(
        functools.partial(_mm_kernel, n_a=len(a_list), mode=mode, dils=tuple(dils), w_t=w_t),
        grid=(t // tm, ncols // tn),
        in_specs=in_specs,
        out_specs=out_specs,
        out_shape=out_shapes,
        scratch_shapes=scratch,
        compiler_params=_params("parallel", "arbitrary"),
        name=name,
    )(*args)
    return outs[0] if len(dils) == 1 else outs


def _alibi_slopes(heads):
    n = N_BRANCH * heads
    s = np.power(np.float32(2.0), -8.0 * np.arange(1, n + 1, dtype=np.float32) / np.float32(n)).astype(np.float32)
    return s.reshape(N_BRANCH, -1)


def _band_attn_kernel(q_ref, kp_ref, kc_ref, vp_ref, vc_ref, o_ref, lse_ref, *scratch, dilation, slopes,
                      stage_heads):
    blk = BAND_BLOCK
    hg = pl.program_id(2)
    has_prev = pl.program_id(1) > 0
    n_heads = len(slopes[0])
    qi = lax.broadcasted_iota(jnp.int32, (blk, 2 * blk), 0)
    ki = lax.broadcasted_iota(jnp.int32, (blk, 2 * blk), 1)
    rel = qi + blk - ki
    valid = jnp.logical_and(jnp.logical_and(rel >= 0, rel <= blk), jnp.logical_or(ki >= blk, has_prev))
    dist = (rel * dilation).astype(F32)
    lane = lax.broadcasted_iota(jnp.int32, (blk, LANES), 1)
    scale = HEAD_DIM ** -0.5
    contract_last = (((1,), (1,)), ((), ()))
    head_cols = [slice(h * HEAD_DIM, (h + 1) * HEAD_DIM) for h in range(n_heads)]
    head_slope = []
    for h in range(n_heads):
        slope = slopes[0][h]
        for g in range(1, len(slopes)):
            slope = jnp.where(hg == g, slopes[g][h], slope)
        head_slope.append(slope)
    for r in range(dilation):
        lse_tile = jnp.zeros((blk, LANES), F32)
        for h0 in range(0, n_heads, stage_heads):
            group = list(range(h0, min(h0 + stage_heads, n_heads)))
            scores = {h: lax.dot_general(q_ref[r, :, head_cols[h]],
                                         jnp.concatenate([kp_ref[r, :, head_cols[h]], kc_ref[r, :, head_cols[h]]],
                                                         axis=0),
                                         contract_last, preferred_element_type=F32) for h in group}
            probs, dens, maxes = {}, {}, {}
            for h in group:
                s = jnp.where(valid, scores[h] * scale - head_slope[h] * dist, -jnp.inf)
                maxes[h] = jnp.max(s, axis=-1, keepdims=True)
                p = jnp.exp(s - maxes[h])
                dens[h] = jnp.sum(p, axis=-1, keepdims=True)
                probs[h] = p.astype(BF16)
            outs = {h: jnp.dot(probs[h],
                               jnp.concatenate([vp_ref[r, :, head_cols[h]], vc_ref[r, :, head_cols[h]]], axis=0),
                               preferred_element_type=F32) for h in group}
            for h in group:
                o = outs[h] / dens[h]
                lse_h = maxes[h] + jnp.log(dens[h])
                for g in range(len(slopes)):
                    lse_tile = jnp.where(jnp.logical_and(lane == g * n_heads + h, hg == g), lse_h, lse_tile)
                if dilation == 1:
                    o_ref[:, head_cols[h]] = o.astype(o_ref.dtype)
                else:
                    scratch[0][h, pl.ds(r, blk, stride=dilation), :] = o
        if dilation == 1:
            lse_slab = lse_tile
        else:
            scratch[1][pl.ds(r, blk, stride=dilation), :] = lse_tile
    if dilation > 1:
        for h in range(n_heads):
            o_ref[:, h * HEAD_DIM:(h + 1) * HEAD_DIM] = scratch[0][h].astype(o_ref.dtype)
        lse_slab = scratch[1][...]

    @pl.when(hg == 0)
    def _():
        lse_ref[...] = lse_slab

    @pl.when(hg > 0)
    def _():
        lse_ref[...] += lse_slab


def _band_attention(qk, v, bsz, seq, branch, slopes):
    _, dilation = A_BRANCHES[branch]
    width = v.shape[-1]
    n_blk = seq // dilation // BAND_BLOCK
    heads = max(min(width // HEAD_DIM, ATTN_PAIRS_PER_STEP // dilation), 1)
    hw = heads * HEAD_DIM
    n_hg = width // hw
    slope_tab = tuple(tuple(float(s) for s in slopes[g * heads:(g + 1) * heads]) for g in range(n_hg))
    rows = BAND_BLOCK * dilation
    blk = (None, dilation, BAND_BLOCK, hw)

    def prev(n):
        return jnp.maximum(n - 1, 0)

    scratch = []
    if dilation > 1:
        scratch = [pltpu.VMEM((heads, rows, HEAD_DIM), F32), pltpu.VMEM((rows, LANES), F32)]
    return pl.pallas_call(
        functools.partial(_band_attn_kernel, dilation=dilation, slopes=slope_tab,
                          stage_heads=ATTN_STAGE_HEADS[dilation]),
        grid=(bsz, n_blk, n_hg),
        in_specs=[pl.BlockSpec(blk, lambda b, n, g: (b, 0, n, g)),
                  pl.BlockSpec(blk, lambda b, n, g: (b, 0, prev(n), n_hg + g)),
                  pl.BlockSpec(blk, lambda b, n, g: (b, 0, n, n_hg + g)),
                  pl.BlockSpec(blk, lambda b, n, g: (b, 0, prev(n), g)),
                  pl.BlockSpec(blk, lambda b, n, g: (b, 0, n, g))],
        out_specs=[pl.BlockSpec((rows, hw), lambda b, n, g: (b * n_blk + n, g)),
                   pl.BlockSpec((rows, LANES), lambda b, n, g: (b * n_blk + n, 0))],
        out_shape=[jax.ShapeDtypeStruct((bsz * seq, width), BF16),
                   jax.ShapeDtypeStruct((bsz * seq, LANES), F32)],
        scratch_shapes=scratch,
        compiler_params=_params("parallel", "arbitrary", "arbitrary"),
        name=f"band_attn_d{dilation}",
    )(qk, qk, qk, v, v)


def _combine_kernel(o0_ref, o1_ref, o2_ref, l0_ref, l1_ref, l2_ref, out_ref):
    l0, l1, l2 = l0_ref[...], l1_ref[...], l2_ref[...]
    m = jnp.maximum(jnp.maximum(l0, l1), l2)
    e0, e1, e2 = jnp.exp(l0 - m), jnp.exp(l1 - m), jnp.exp(l2 - m)
    tot = e0 + e1 + e2
    w0, w1, w2 = e0 / tot, e1 / tot, e2 / tot
    for h in range(out_ref.shape[1] // HEAD_DIM):
        sl = slice(h * HEAD_DIM, (h + 1) * HEAD_DIM)
        acc = (w0[:, h:h + 1] * o0_ref[:, sl].astype(F32) + w1[:, h:h + 1] * o1_ref[:, sl].astype(F32)
               + w2[:, h:h + 1] * o2_ref[:, sl].astype(F32))
        out_ref[:, sl] = acc.astype(out_ref.dtype)


def _combine_branches(outs, lses, tm=512):
    t, width = outs[0].shape
    o_spec = pl.BlockSpec((tm, width), lambda i: (i, 0))
    l_spec = pl.BlockSpec((tm, LANES), lambda i: (i, 0))
    return pl.pallas_call(
        _combine_kernel,
        grid=(t // tm,),
        in_specs=[o_spec] * 3 + [l_spec] * 3,
        out_specs=o_spec,
        out_shape=jax.ShapeDtypeStruct((t, width), BF16),
        compiler_params=_params("parallel"),
        name="combine_branches",
    )(*outs, *lses)


def _pool_kernel(u_ref, uh_ref, w_ref, sc_ref, o_ref, wb_ref, *, blocks_per_seq):
    i = pl.program_id(0)

    @pl.when(i == 0)
    def _():
        wb_ref[...] = w_ref[...].astype(BF16)

    tm = u_ref.shape[0]
    group = w_ref.shape[1]
    first = (i % blocks_per_seq) == 0
    row = lax.broadcasted_iota(jnp.int32, (tm, 1), 0)
    pos = (i % blocks_per_seq) * tm + row
    for g, win in enumerate(POOL_WINDOWS):
        sl = slice(g * group, (g + 1) * group)
        u = u_ref[:, sl]
        halo = jnp.where(first, 0.0, uh_ref[:, sl])
        s = jnp.concatenate([halo, u], axis=0)
        step = 1
        while step < win:
            s = s + pltpu.roll(s, step, axis=0)
            step *= 2
        count = jnp.minimum(pos + 1, win).astype(F32)
        pooled = s[POOL_HALO:] / count - u
        y = jnp.dot(pooled.astype(BF16), wb_ref[g], preferred_element_type=F32)
        o_ref[:, sl] = (y * sc_ref[:, sl]).astype(o_ref.dtype)


def _pool_mixer(u, pool_w, pool_scale, seq, tm=1024):
    t, width = u.shape
    n_group, group, _ = pool_w.shape
    blocks_per_seq = seq // tm
    halo_blocks = tm // POOL_HALO
    return pl.pallas_call(
        functools.partial(_pool_kernel, blocks_per_seq=blocks_per_seq),
        grid=(t // tm,),
        in_specs=[pl.BlockSpec((tm, width), lambda i: (i, 0)),
                  pl.BlockSpec((POOL_HALO, width), lambda i: (jnp.maximum(i * halo_blocks - 1, 0), 0)),
                  pl.BlockSpec((n_group, group, group), lambda i: (0, 0, 0)),
                  pl.BlockSpec((1, width), lambda i: (0, 0))],
        out_specs=pl.BlockSpec((tm, width), lambda i: (i, 0)),
        out_shape=jax.ShapeDtypeStruct((t, width), BF16),
        scratch_shapes=[pltpu.VMEM((n_group, group, group), BF16)],
        compiler_params=_params("arbitrary"),
        name="pool_mixer",
    )(u, u, pool_w, pool_scale.reshape(1, width))


def _split3(x):
    hi = x.astype(BF16)
    r1 = x - hi.astype(F32)
    mid = r1.astype(BF16)
    lo = (r1 - mid.astype(F32)).astype(BF16)
    return hi, mid, lo


def _gate_kernel(h_ref, ng_ref, wr_ref, wa2_ref, ba_ref, hn_ref, o_ref):
    x = h_ref[...]
    ms = jnp.mean(x * x, axis=-1, keepdims=True)
    hn = (x * lax.rsqrt(ms + EPS) * ng_ref[...]).astype(hn_ref.dtype)
    hn_ref[...] = hn
    r = lax.dot_general(hn, wr_ref[...].astype(BF16), (((1,), (1,)), ((), ())), preferred_element_type=F32)
    r_hi, r_mid, _ = _split3(r)
    w_hi, w_mid, _ = _split3(wa2_ref[...])
    g = (jnp.dot(r_hi, w_hi, preferred_element_type=F32)
         + (jnp.dot(r_hi, w_mid, preferred_element_type=F32) + jnp.dot(r_mid, w_hi, preferred_element_type=F32)))
    g = g + ba_ref[...]
    log_sig = jnp.minimum(g, 0.0) - jnp.log1p(jnp.exp(-jnp.abs(g)))
    o_ref[...] = log_sig / C_GATE_TAU


def _gla_norm_log_decay(h, norm_gain, w_r, w_a2, b_a, tm=512):
    t, d = h.shape
    kw = w_a2.shape[1]
    return pl.pallas_call(
        _gate_kernel,
        grid=(t // tm,),
        in_specs=[pl.BlockSpec((tm, d), lambda i: (i, 0)),
                  pl.BlockSpec((1, d), lambda i: (0, 0)),
                  pl.BlockSpec((LANES, d), lambda i: (0, 0)),
                  pl.BlockSpec((LANES, kw), lambda i: (0, 0)),
                  pl.BlockSpec((1, kw), lambda i: (0, 0))],
        out_specs=[pl.BlockSpec((tm, d), lambda i: (i, 0)),
                   pl.BlockSpec((tm, kw), lambda i: (i, 0))],
        out_shape=[jax.ShapeDtypeStruct((t, d), BF16),
                   jax.ShapeDtypeStruct((t, kw), F32)],
        compiler_params=_params("parallel"),
        name="gla_norm_log_decay",
    )(h, norm_gain.reshape(1, d), w_r, w_a2, b_a.reshape(1, kw))


def _gla_kernel(q_ref, k_ref, v_ref, la_ref, gate_ref, og_ref, o_ref, state_ref, *, chunks, heads):
    @pl.when(pl.program_id(2) == 0)
    def _():
        state_ref[...] = jnp.zeros_like(state_ref)

    c = C_CHUNK
    dk = q_ref.shape[1] // heads
    dv = v_ref.shape[1] // heads
    ri = lax.broadcasted_iota(jnp.int32, (c, c), 0)
    ci = lax.broadcasted_iota(jnp.int32, (c, c), 1)
    causal = ci <= ri
    tri = causal.astype(BF16)
    contract_last = (((1,), (1,)), ((), ()))
    contract_first = (((0,), (0,)), ((), ()))
    pairs = [(n, h) for n in range(chunks) for h in range(heads)]

    def rows(n):
        return slice(n * c, (n + 1) * c)

    def kcols(h):
        return slice(h * dk, (h + 1) * dk)

    def vcols(h):
        return slice(h * dv, (h + 1) * dv)

    bc = {}
    for n, h in pairs:
        la_hi, la_mid, la_lo = _split3(la_ref[rows(n), kcols(h)])
        bc[n, h] = (jnp.dot(tri, la_hi, preferred_element_type=F32)
                    + jnp.dot(tri, la_mid, preferred_element_type=F32)
                    + jnp.dot(tri, la_lo, preferred_element_type=F32))
    q_b, k_in, k_st, decay = {}, {}, {}, {}
    for n, h in pairs:
        b = bc[n, h]
        b_last = b[c - 1:c, :]
        q_b[n, h] = (q_ref[rows(n), kcols(h)] * (dk ** -0.5) * jnp.exp(b)).astype(BF16)
        k = k_ref[rows(n), kcols(h)]
        k_in[n, h] = (k * jnp.exp(-b)).astype(BF16)
        k_st[n, h] = (k * jnp.exp(b_last - b)).astype(BF16)
        decay[n, h] = jnp.exp(b_last)
    att = {}
    for p in pairs:
        a = lax.dot_general(q_b[p], k_in[p], contract_last, preferred_element_type=F32)
        att[p] = jnp.where(causal, a, 0.0).astype(BF16)
    o_intra, upd = {}, {}
    for n, h in pairs:
        v = v_ref[rows(n), vcols(h)]
        o_intra[n, h] = jnp.dot(att[n, h], v, preferred_element_type=F32)
        upd[n, h] = lax.dot_general(v, k_st[n, h], contract_first, preferred_element_type=F32)
    for h in range(heads):
        state = state_ref[h]
        for n in range(chunks):
            o = o_intra[n, h] + lax.dot_general(q_b[n, h], state.astype(BF16), contract_last,
                                                preferred_element_type=F32)
            state = state * decay[n, h] + upd[n, h]
            ms = jnp.mean(o * o, axis=-1, keepdims=True)
            gate = gate_ref[rows(n), vcols(h)]
            y = o * lax.rsqrt(ms + EPS) * og_ref[...] * (gate * jax.nn.sigmoid(gate))
            o_ref[rows(n), vcols(h)] = y.astype(o_ref.dtype)
        state_ref[h] = state


def _gla(qk, v, gate, log_a, o_gain, bsz, seq, rows=512, heads=2):
    t = bsz * seq
    dk = log_a.shape[1] // C_HEADS
    dv = v.shape[1] // C_HEADS
    steps = seq // rows
    groups = C_HEADS // heads

    def row_block(b, n):
        return b * steps + n

    return pl.pallas_call(
        functools.partial(_gla_kernel, chunks=rows // C_CHUNK, heads=heads),
        grid=(bsz, groups, steps),
        in_specs=[pl.BlockSpec((rows, heads * dk), lambda b, g, n: (row_block(b, n), g)),
                  pl.BlockSpec((rows, heads * dk), lambda b, g, n: (row_block(b, n), groups + g)),
                  pl.BlockSpec((rows, heads * dv), lambda b, g, n: (row_block(b, n), g)),
                  pl.BlockSpec((rows, heads * dk), lambda b, g, n: (row_block(b, n), g)),
                  pl.BlockSpec((rows, heads * dv), lambda b, g, n: (row_block(b, n), g)),
                  pl.BlockSpec((1, dv), lambda b, g, n: (0, 0))],
        out_specs=pl.BlockSpec((rows, heads * dv), lambda b, g, n: (row_block(b, n), g)),
        out_shape=jax.ShapeDtypeStruct((t, v.shape[1]), BF16),
        scratch_shapes=[pltpu.VMEM((heads, dv, dk), F32)],
        compiler_params=_params("parallel", "parallel", "arbitrary"),
        name="gla",
    )(qk, qk, v, log_a, gate, o_gain.reshape(1, dv))


def _xattn_kernel(h_ref, g_ref, wq_ref, qg_ref, k_ref, v_ref, wo_ref, fg_ref, o_ref, on_ref):
    x = h_ref[...]
    ms = jnp.mean(x * x, axis=-1, keepdims=True)
    hn = (x * lax.rsqrt(ms + EPS) * g_ref[...]).astype(BF16)
    q = jnp.dot(hn, wq_ref[...], preferred_element_type=F32)
    scale = HEAD_DIM ** -0.5
    contract_last = (((1,), (1,)), ((), ()))
    cols = [slice(hd * HEAD_DIM, (hd + 1) * HEAD_DIM) for hd in range(X_HEADS)]
    q_heads = []
    for sl in cols:
        qh = q[:, sl]
        qms = jnp.mean(qh * qh, axis=-1, keepdims=True)
        q_heads.append((qh * lax.rsqrt(qms + EPS) * qg_ref[...]).astype(BF16))
    scores = [lax.dot_general(qh, k_ref[0, :, sl], contract_last, preferred_element_type=F32) * scale
              for qh, sl in zip(q_heads, cols)]
    probs = []
    for s in scores:
        m = jnp.max(s, axis=-1, keepdims=True)
        p = jnp.exp(s - m)
        probs.append((p / jnp.sum(p, axis=-1, keepdims=True)).astype(BF16))
    heads = [jnp.dot(p, v_ref[0, :, sl], preferred_element_type=F32) for p, sl in zip(probs, cols)]
    o = jnp.concatenate(heads, axis=-1).astype(BF16)
    y = x + jnp.dot(o, wo_ref[...], preferred_element_type=F32)
    o_ref[...] = y
    yms = jnp.mean(y * y, axis=-1, keepdims=True)
    on_ref[...] = (y * lax.rsqrt(yms + EPS) * fg_ref[...]).astype(on_ref.dtype)


def _cross_attention(h, norm_gain, wq, q_gain, k, v, wo, next_gain, seq, tm=256):
    t, d = h.shape
    xw = wq.shape[1]
    mlen = k.shape[1]
    tiles_per_seq = seq // tm
    return pl.pallas_call(
        _xattn_kernel,
        grid=(t // tm,),
        in_specs=[pl.BlockSpec((tm, d), lambda i: (i, 0)),
                  pl.BlockSpec((1, d), lambda i: (0, 0)),
                  pl.BlockSpec((d, xw), lambda i: (0, 0)),
                  pl.BlockSpec((1, HEAD_DIM), lambda i: (0, 0)),
                  pl.BlockSpec((1, mlen, xw), lambda i: (i // tiles_per_seq, 0, 0)),
                  pl.BlockSpec((1, mlen, xw), lambda i: (i // tiles_per_seq, 0, 0)),
                  pl.BlockSpec((xw, d), lambda i: (0, 0)),
                  pl.BlockSpec((1, d), lambda i: (0, 0))],
        out_specs=[pl.BlockSpec((tm, d), lambda i: (i, 0)),
                   pl.BlockSpec((tm, d), lambda i: (i, 0))],
        out_shape=[jax.ShapeDtypeStruct((t, d), F32),
                   jax.ShapeDtypeStruct((t, d), BF16)],
        compiler_params=_params("parallel"),
        name="cross_attention",
    )(h, norm_gain.reshape(1, d), wq, q_gain.reshape(1, HEAD_DIM), k, v, wo, next_gain.reshape(1, d))


def _ffn_up_kernel(a_ref, ah_ref, wg_ref, wv_ref, cwg_ref, cwv_ref, cbg_ref, cbv_ref, wd_ref, o_ref, wdo_ref, *,
                   blocks_per_seq):
    wdo_ref[...] = wd_ref[...].astype(wdo_ref.dtype)
    tm = a_ref.shape[0]
    first = (pl.program_id(0) % blocks_per_seq) == 0
    row = lax.broadcasted_iota(jnp.int32, (tm, 1), 0)
    a = a_ref[...]
    ah = ah_ref[...]

    def conv_half(w_ref, cw_ref, cb_ref):
        w = w_ref[...].astype(BF16)
        u = jnp.dot(a, w, preferred_element_type=F32)
        uh = jnp.where(first, 0.0, jnp.dot(ah, w, preferred_element_type=F32))
        u1 = jnp.where(row == 0, uh[CONV_HALO - 1:CONV_HALO], pltpu.roll(u, 1, axis=0))
        u2 = jnp.where(row == 0, uh[CONV_HALO - 2:CONV_HALO - 1],
                       jnp.where(row == 1, uh[CONV_HALO - 1:CONV_HALO], pltpu.roll(u, 2, axis=0)))
        cw = cw_ref[...]
        return cb_ref[...] + cw[0:1] * u2 + cw[1:2] * u1 + cw[2:3] * u

    cg = conv_half(wg_ref, cwg_ref, cbg_ref)
    cv = conv_half(wv_ref, cwv_ref, cbv_ref)
    o_ref[...] = (cg * jax.nn.sigmoid(cg) * cv).astype(o_ref.dtype)


def _ffn_up(hn, w_up, conv_w, conv_b, w_down, layer, seq, tm=FFN_ROWS, tn=256):
    t, d = hn.shape
    d_ff = w_up.shape[2] // 2
    n_tiles = d_ff // tn
    n_steps = (t // tm) * n_tiles
    slab = d_ff // n_steps
    assert d_ff % n_steps == 0 and slab % (2 * SUBLANES) == 0
    blocks_per_seq = seq // tm
    halo_blocks = tm // CONV_HALO
    cw = conv_w[layer]
    cb = conv_b[layer].reshape(1, 2 * d_ff)
    return pl.pallas_call(
        functools.partial(_ffn_up_kernel, blocks_per_seq=blocks_per_seq),
        grid=(t // tm, n_tiles),
        in_specs=[pl.BlockSpec((tm, d), lambda i, j: (i, 0), pipeline_mode=pl.Buffered(1)),
                  pl.BlockSpec((CONV_HALO, d), lambda i, j: (jnp.maximum(i * halo_blocks - 1, 0), 0)),
                  pl.BlockSpec((None, d, tn), lambda i, j: (layer, 0, j)),
                  pl.BlockSpec((None, d, tn), lambda i, j: (layer, 0, n_tiles + j)),
                  pl.BlockSpec((CONV_WIDTH, tn), lambda i, j: (0, j)),
                  pl.BlockSpec((CONV_WIDTH, tn), lambda i, j: (0, n_tiles + j)),
                  pl.BlockSpec((1, tn), lambda i, j: (0, j)),
                  pl.BlockSpec((1, tn), lambda i, j: (0, n_tiles + j)),
                  pl.BlockSpec((None, slab, d), lambda i, j: (layer, i * n_tiles + j, 0))],
        out_specs=[pl.BlockSpec((tm, tn), lambda i, j: (i, j)),
                   pl.BlockSpec((slab, d), lambda i, j: (i * n_tiles + j, 0))],
        out_shape=[jax.ShapeDtypeStruct((t, d_ff), BF16),
                   jax.ShapeDtypeStruct((d_ff, d), BF16)],
        compiler_params=_params("parallel", "arbitrary"),
        name="ffn_up",
    )(hn, hn, w_up, w_up, cw, cw, cb, cb, w_down)


def _dilated_pool_layer(h, hn, w_in, q_gain, k_gain, pool_w, pool_scale, w_out, j, bsz, seq):
    d = h.shape[1]
    a_width = d // 2
    heads = a_width // HEAD_DIM
    tn = 512
    q_blocks = a_width // tn
    k_base = N_BRANCH * q_blocks
    v_cols = 2 * N_BRANCH * a_width
    gains = jnp.concatenate([jnp.tile(q_gain, heads), jnp.tile(k_gain, heads)]).reshape(1, 2 * a_width)
    dils = tuple(dil for _, dil in A_BRANCHES)
    v_list = _mm([hn], w_in, j, lambda c: v_cols // tn + c, a_width, BF16, dils=dils, seq=seq, tn=tn,
                 name="ab_in_v")
    u = _mm([hn], w_in, j, lambda c: (v_cols + a_width) // tn + c, d - a_width, F32, tn=tn, name="ab_in_u")
    slopes = _alibi_slopes(heads)
    outs, lses = [], []
    for g, dil in enumerate(dils):
        def wcol(c, g=g):
            return jnp.where(c < q_blocks, g * q_blocks + c, k_base + g * q_blocks + c - q_blocks)
        qk = _mm([hn], w_in, j, wcol, 2 * a_width, BF16, mode="headnorm", extra=gains, dils=(dil,), seq=seq,
                 tn=tn, name=f"ab_in_qk{g}")
        if dil == 1:
            qk = qk.reshape(bsz, 1, seq, 2 * a_width)
            v_g = v_list[g].reshape(bsz, 1, seq, a_width)
        else:
            v_g = v_list[g]
        o, lse = _band_attention(qk, v_g, bsz, seq, g, slopes[g])
        outs.append(o)
        lses.append(lse)
    a_out = _combine_branches(outs, lses)
    b_out = _pool_mixer(u, pool_w, pool_scale, seq)
    return _mm([a_out, b_out], w_out, j, lambda c: c, d, F32, mode="residual", extra=h, name="ab_out")


def _gla_layer(h, norm_gain, w_in, w_a2, b_a, o_gain, w_out, j, bsz, seq):
    d = h.shape[1]
    kw = w_a2.shape[2]
    vw = w_out.shape[1]
    tn = 512
    w_in_t = jnp.swapaxes(w_in, 1, 2)
    rank = w_a2.shape[1]
    w_r = jnp.pad(w_in_t[j, 2 * kw + 2 * vw:, :], ((0, LANES - rank), (0, 0)))
    w_a2p = jnp.pad(w_a2[j], ((0, LANES - rank), (0, 0)))
    hn, log_a = _gla_norm_log_decay(h, norm_gain, w_r, w_a2p, b_a[j])
    qk = _mm([hn], w_in_t, j, lambda c: c, 2 * kw, F32, w_t=True, tn=tn, name="c_in_qk")
    v = _mm([hn], w_in_t, j, lambda c: 2 * kw // tn + c, vw, BF16, w_t=True, tn=tn, name="c_in_v")
    gate = _mm([hn], w_in_t, j, lambda c: (2 * kw + vw) // tn + c, vw, F32, w_t=True, tn=tn, name="c_in_gate")
    o = _gla(qk, v, gate, log_a, o_gain[j], bsz, seq)
    return _mm([o], w_out, j, lambda c: c, d, F32, mode="residual", extra=h, name="c_out")


def _memory_kv(mem, gain, wkv, k_gain, layer, bsz):
    xw = wkv.shape[2] // 2
    tn = 256
    mem_n = _rmsnorm(mem, gain)
    rows = mem.shape[0]
    gains = jnp.tile(k_gain, xw // HEAD_DIM).reshape(1, xw)
    k = _mm([mem_n], wkv, layer, lambda c: c, xw, BF16, mode="headnorm", extra=gains, tm=rows, tn=tn, name="mem_k")
    v = _mm([mem_n], wkv, layer, lambda c: xw // tn + c, xw, BF16, tm=rows, tn=tn, name="mem_v")
    return k.reshape(bsz, rows // bsz, xw), v.reshape(bsz, rows // bsz, xw)


def kernel(x, mem, mix_norm, ab_w_in, ab_q_norm, ab_k_norm, ab_pool_w, ab_pool_scale, ab_w_out, c_w_in, c_w_a2, c_b_a, c_o_norm, c_w_out, x_norm, x_mem_norm, x_wq, x_wkv, x_q_norm, x_k_norm, x_wo, f_norm, f_w_up, f_conv_w, f_conv_b, f_w_down):
    bsz, seq, d = x.shape
    depth = mix_norm.shape[0]
    h = x.reshape(bsz * seq, d)
    mem2 = mem.reshape(bsz * mem.shape[1], d)
    for layer in range(depth):
        j = layer // 2
        if layer % 2 == 0:
            hn = _rmsnorm(h, mix_norm[layer])
            h = _dilated_pool_layer(h, hn, ab_w_in, ab_q_norm[j], ab_k_norm[j], ab_pool_w[j], ab_pool_scale[j],
                                    ab_w_out, j, bsz, seq)
        else:
            h = _gla_layer(h, mix_norm[layer], c_w_in, c_w_a2, c_b_a, c_o_norm, c_w_out, j, bsz, seq)
        k, v = _memory_kv(mem2, x_mem_norm[layer], x_wkv, x_k_norm[layer], layer, bsz)
        h, hn = _cross_attention(h, x_norm[layer], x_wq[layer].astype(BF16), x_q_norm[layer], k, v,
                                 x_wo[layer].astype(BF16), f_norm[layer], seq)
        act, w_down = _ffn_up(hn, f_w_up, f_conv_w, f_conv_b, f_w_down, layer, seq)
        h = _mm([act], w_down, None, lambda c: c, d, F32, mode="residual", extra=h, tm=512, name="ffn_down")
    return h.reshape(bsz, seq, d)
```

```python
import functools

import numpy as np
import jax
import jax.numpy as jnp
from jax import lax
from jax.experimental import pallas as pl
from jax.experimental.pallas import tpu as pltpu

F32 = jnp.float32
BF16 = jnp.bfloat16

LANES = 128
SUBLANES = 8
VMEM_LIMIT_BYTES = 56 * 2 ** 20

EPS = 1e-6
HEAD_DIM = 128
A_BRANCHES = ((128, 1), (512, 4), (2048, 16))
N_BRANCH = len(A_BRANCHES)
BAND_BLOCK = 128
POOL_WINDOWS = (2, 4, 8, 16)
POOL_HALO = 16
C_HEADS = 8
C_GATE_RANK = 16
C_GATE_TAU = 16.0
C_CHUNK = 64
X_HEADS = 4
CONV_WIDTH = 3
CONV_HALO = SUBLANES
MM_ROWS = 1024
FAST_ROW_STRIDE = 4
ATTN_STAGE_HEADS = {1: 8, 4: 16, 16: 1}
ATTN_PAIRS_PER_STEP = 64
FFN_ROWS = 2048


def _params(*semantics):
    return pltpu.CompilerParams(dimension_semantics=semantics, vmem_limit_bytes=VMEM_LIMIT_BYTES)


def _lane_groups(width):
    return [slice(c * LANES, (c + 1) * LANES) for c in range(width // LANES)]


def _rmsnorm_kernel(x_ref, g_ref, o_ref):
    x = x_ref[...].astype(F32)
    ms = jnp.mean(x * x, axis=-1, keepdims=True)
    o_ref[...] = (x * lax.rsqrt(ms + EPS) * g_ref[...]).astype(o_ref.dtype)


def _rmsnorm(x, gain, tm=512):
    t, d = x.shape
    return pl.pallas_call(
        _rmsnorm_kernel,
        grid=(t // tm,),
        in_specs=[pl.BlockSpec((tm, d), lambda i: (i, 0)),
                  pl.BlockSpec((1, d), lambda i: (0, 0))],
        out_specs=pl.BlockSpec((tm, d), lambda i: (i, 0)),
        out_shape=jax.ShapeDtypeStruct((t, d), BF16),
        compiler_params=_params("parallel"),
        name="rmsnorm",
    )(x, gain.reshape(1, d))


def _mm_kernel(*refs, n_a, mode, dils, w_t):
    a_refs = refs[:n_a]
    w_ref = refs[n_a]
    n_extra = 1 if mode in ("headnorm", "residual") else 0
    extra = refs[n_a + 1:n_a + 1 + n_extra]
    o_refs = refs[n_a + 1 + n_extra:n_a + 1 + n_extra + len(dils)]
    scratch = refs[n_a + 1 + n_extra + len(dils):]
    acc = None
    k0 = 0
    for a_ref in a_refs:
        kk = a_ref.shape[1]
        if w_t:
            part = lax.dot_general(a_ref[...], w_ref[:, k0:k0 + kk].astype(BF16), (((1,), (1,)), ((), ())),
                                   preferred_element_type=F32)
        else:
            part = jnp.dot(a_ref[...], w_ref[k0:k0 + kk, :].astype(BF16), preferred_element_type=F32)
        acc = part if acc is None else acc + part
        k0 += kk
    tm, tn = acc.shape
    for c, sl in enumerate(_lane_groups(tn)):
        blk = acc[:, sl]
        if mode == "headnorm":
            ms = jnp.mean(blk * blk, axis=-1, keepdims=True)
            blk = blk * lax.rsqrt(ms + EPS) * extra[0][:, sl]
        elif mode == "residual":
            blk = extra[0][:, sl] + blk
        if scratch:
            scratch[0][c] = blk
        for o_ref, dil in zip(o_refs, dils):
            if dil == 1:
                o_ref[:, sl] = blk.astype(o_ref.dtype)
    for o_ref, dil in zip(o_refs, dils):
        if 1 < dil <= FAST_ROW_STRIDE:
            for r in range(dil):
                for c, sl in enumerate(_lane_groups(tn)):
                    o_ref[r, :, sl] = scratch[0][c, pl.ds(r, tm // dil, stride=dil), :].astype(o_ref.dtype)
        elif dil > FAST_ROW_STRIDE:
            s1, s2 = FAST_ROW_STRIDE, dil // FAST_ROW_STRIDE
            part = tm // s1
            for q in range(s1):
                for c in range(tn // LANES):
                    scratch[1][c, q * part:(q + 1) * part, :] = scratch[0][c, pl.ds(q, part, stride=s1), :]
            for q in range(s1):
                for p in range(s2):
                    for c, sl in enumerate(_lane_groups(tn)):
                        o_ref[s1 * p + q, :, sl] = scratch[1][c, pl.ds(q * part + p, tm // dil, stride=s2),
                                                              :].astype(o_ref.dtype)


def _mm(a_list, w, layer, wcol, ncols, out_dtype, *, mode="plain", extra=None, dils=(1,), seq=None,
        w_t=False, tm=MM_ROWS, tn=512, name="mm"):
    t = a_list[0].shape[0]
    k_total = sum(a.shape[1] for a in a_list)
    assert w.shape[-1 if w_t else -2] == k_total and ncols % tn == 0 and t % tm == 0
    in_specs = [pl.BlockSpec((tm, a.shape[1]), lambda i, j: (i, 0)) for a in a_list]
    if w_t:
        in_specs.append(pl.BlockSpec((None, tn, k_total), lambda i, j: (layer, wcol(j), 0)))
    elif w.ndim == 3:
        in_specs.append(pl.BlockSpec((None, k_total, tn), lambda i, j: (layer, 0, wcol(j))))
    else:
        in_specs.append(pl.BlockSpec((k_total, tn), lambda i, j: (0, wcol(j))))
    args = list(a_list) + [w]
    if mode == "headnorm":
        in_specs.append(pl.BlockSpec((1, tn), lambda i, j: (0, j)))
        args.append(extra)
    elif mode == "residual":
        in_specs.append(pl.BlockSpec((tm, tn), lambda i, j: (i, j)))
        args.append(extra)
    out_specs, out_shapes = [], []
    for dil in dils:
        if dil == 1:
            out_specs.append(pl.BlockSpec((tm, tn), lambda i, j: (i, j)))
            out_shapes.append(jax.ShapeDtypeStruct((t, ncols), out_dtype))
        else:
            tiles = seq // tm
            assert seq % tm == 0 and tm % dil == 0
            out_specs.append(pl.BlockSpec((None, dil, tm // dil, tn), lambda i, j: (i // tiles, 0, i % tiles, j)))
            out_shapes.append(jax.ShapeDtypeStruct((t // seq, dil, seq // dil, ncols), out_dtype))
    assert all(dil <= FAST_ROW_STRIDE ** 2 and (dil <= FAST_ROW_STRIDE or dil % FAST_ROW_STRIDE == 0) for dil in dils)
    slab = pltpu.VMEM((tn // LANES, tm, LANES), F32)
    n_slabs = 2 if max(dils) > FAST_ROW_STRIDE else 1 if max(dils) > 1 else 0
    scratch = [slab] * n_slabs
    outs = pl.pallas_call(
        functools.partial(_mm_kernel, n_a=len(a_list), mode=mode, dils=tuple(dils), w_t=w_t),
        grid=(t // tm, ncols // tn),
        in_specs=in_specs,
        out_specs=out_specs,
        out_shape=out_shapes,
        scratch_shapes=scratch,
        compiler_params=_params("parallel", "arbitrary"),
        name=name,
    )(*args)
    return outs[0] if len(dils) == 1 else outs


def _alibi_slopes(heads):
    n = N_BRANCH * heads
    s = np.power(np.float32(2.0), -8.0 * np.arange(1, n + 1, dtype=np.float32) / np.float32(n)).astype(np.float32)
    return s.reshape(N_BRANCH, -1)


def _band_attn_kernel(q_ref, kp_ref, kc_ref, vp_ref, vc_ref, o_ref, lse_ref, *scratch, dilation, slopes,
                      stage_heads, sub):
    blk = BAND_BLOCK
    hg = pl.program_id(2)
    has_prev = pl.program_id(1) > 0
    n_heads = len(slopes[0])
    qi = lax.broadcasted_iota(jnp.int32, (blk, 2 * blk), 0)
    ki = lax.broadcasted_iota(jnp.int32, (blk, 2 * blk), 1)
    rel = qi + blk - ki
    in_band = jnp.logical_and(rel >= 0, rel <= blk)
    valid_first = jnp.logical_and(in_band, jnp.logical_or(ki >= blk, has_prev))
    dist = (rel * dilation).astype(F32)
    lane = lax.broadcasted_iota(jnp.int32, (blk, LANES), 1)
    scale = HEAD_DIM ** -0.5
    contract_last = (((1,), (1,)), ((), ()))
    head_cols = [slice(h * HEAD_DIM, (h + 1) * HEAD_DIM) for h in range(n_heads)]
    head_slope = []
    for h in range(n_heads):
        slope = slopes[0][h]
        for g in range(1, len(slopes)):
            slope = jnp.where(hg == g, slopes[g][h], slope)
        head_slope.append(slope)
    lse_tiles = []
    for r, m in [(r, m) for r in range(dilation) for m in range(sub)]:
        rows = slice(m * blk, (m + 1) * blk)
        prev_rows = slice((m - 1) * blk, m * blk)
        valid = valid_first if m == 0 else in_band

        def keys(prev_ref, cur_ref, cols):
            prev = prev_ref[r, :, cols] if m == 0 else cur_ref[r, prev_rows, cols]
            return jnp.concatenate([prev, cur_ref[r, rows, cols]], axis=0)

        lse_tile = jnp.zeros((blk, LANES), F32)
        for h0 in range(0, n_heads, stage_heads):
            group = list(range(h0, min(h0 + stage_heads, n_heads)))
            scores = {h: lax.dot_general(q_ref[r, rows, head_cols[h]], keys(kp_ref, kc_ref, head_cols[h]),
                                         contract_last, preferred_element_type=F32) for h in group}
            probs, dens, maxes = {}, {}, {}
            for h in group:
                s = jnp.where(valid, scores[h] * scale - head_slope[h] * dist, -jnp.inf)
                maxes[h] = jnp.max(s, axis=-1, keepdims=True)
                p = jnp.exp(s - maxes[h])
                dens[h] = jnp.sum(p, axis=-1, keepdims=True)
                probs[h] = p.astype(BF16)
            outs = {h: jnp.dot(probs[h], keys(vp_ref, vc_ref, head_cols[h]), preferred_element_type=F32)
                    for h in group}
            for h in group:
                o = outs[h] / dens[h]
                lse_h = maxes[h] + jnp.log(dens[h])
                for g in range(len(slopes)):
                    lse_tile = jnp.where(jnp.logical_and(lane == g * n_heads + h, hg == g), lse_h, lse_tile)
                if dilation == 1:
                    o_ref[rows, head_cols[h]] = o.astype(o_ref.dtype)
                else:
                    scratch[0][h, pl.ds(r, blk, stride=dilation), :] = o
        if dilation == 1:
            lse_tiles.append(lse_tile)
        else:
            scratch[1][pl.ds(r, blk, stride=dilation), :] = lse_tile
    if dilation == 1:
        lse_slab = jnp.concatenate(lse_tiles, axis=0)
    else:
        for h in range(n_heads):
            o_ref[:, h * HEAD_DIM:(h + 1) * HEAD_DIM] = scratch[0][h].astype(o_ref.dtype)
        lse_slab = scratch[1][...]

    @pl.when(hg == 0)
    def _():
        lse_ref[...] = lse_slab

    @pl.when(hg > 0)
    def _():
        lse_ref[...] += lse_slab


def _band_attention(qk, v, bsz, seq, branch, slopes):
    _, dilation = A_BRANCHES[branch]
    width = v.shape[-1]
    n_blk = seq // dilation // BAND_BLOCK
    heads = max(min(width // HEAD_DIM, ATTN_PAIRS_PER_STEP // dilation), 1)
    hw = heads * HEAD_DIM
    n_hg = width // hw
    sub = ATTN_PAIRS_PER_STEP // heads if dilation == 1 else 1
    assert n_blk % sub == 0
    n_steps = n_blk // sub
    slope_tab = tuple(tuple(float(s) for s in slopes[g * heads:(g + 1) * heads]) for g in range(n_hg))
    rows = BAND_BLOCK * dilation * sub
    blk = (None, dilation, sub * BAND_BLOCK, hw)
    prev_blk = (None, dilation, BAND_BLOCK, hw)

    def prev(n):
        return jnp.maximum(n * sub - 1, 0)

    scratch = []
    if dilation > 1:
        scratch = [pltpu.VMEM((heads, rows, HEAD_DIM), F32), pltpu.VMEM((rows, LANES), F32)]
    return pl.pallas_call(
        functools.partial(_band_attn_kernel, dilation=dilation, slopes=slope_tab,
                          stage_heads=ATTN_STAGE_HEADS[dilation], sub=sub),
        grid=(bsz, n_steps, n_hg),
        in_specs=[pl.BlockSpec(blk, lambda b, n, g: (b, 0, n, g)),
                  pl.BlockSpec(prev_blk, lambda b, n, g: (b, 0, prev(n), n_hg + g)),
                  pl.BlockSpec(blk, lambda b, n, g: (b, 0, n, n_hg + g)),
                  pl.BlockSpec(prev_blk, lambda b, n, g: (b, 0, prev(n), g)),
                  pl.BlockSpec(blk, lambda b, n, g: (b, 0, n, g))],
        out_specs=[pl.BlockSpec((rows, hw), lambda b, n, g: (b * n_steps + n, g)),
                   pl.BlockSpec((rows, LANES), lambda b, n, g: (b * n_steps + n, 0))],
        out_shape=[jax.ShapeDtypeStruct((bsz * seq, width), BF16),
                   jax.ShapeDtypeStruct((bsz * seq, LANES), F32)],
        scratch_shapes=scratch,
        compiler_params=_params("parallel", "arbitrary", "arbitrary"),
        name=f"band_attn_d{dilation}",
    )(qk, qk, qk, v, v)


def _combine_kernel(o0_ref, o1_ref, o2_ref, l0_ref, l1_ref, l2_ref, out_ref):
    l0, l1, l2 = l0_ref[...], l1_ref[...], l2_ref[...]
    m = jnp.maximum(jnp.maximum(l0, l1), l2)
    e0, e1, e2 = jnp.exp(l0 - m), jnp.exp(l1 - m), jnp.exp(l2 - m)
    tot = e0 + e1 + e2
    w0, w1, w2 = e0 / tot, e1 / tot, e2 / tot
    for h in range(out_ref.shape[1] // HEAD_DIM):
        sl = slice(h * HEAD_DIM, (h + 1) * HEAD_DIM)
        acc = (w0[:, h:h + 1] * o0_ref[:, sl].astype(F32) + w1[:, h:h + 1] * o1_ref[:, sl].astype(F32)
               + w2[:, h:h + 1] * o2_ref[:, sl].astype(F32))
        out_ref[:, sl] = acc.astype(out_ref.dtype)


def _combine_branches(outs, lses, tm=512):
    t, width = outs[0].shape
    o_spec = pl.BlockSpec((tm, width), lambda i: (i, 0))
    l_spec = pl.BlockSpec((tm, LANES), lambda i: (i, 0))
    return pl.pallas_call(
        _combine_kernel,
        grid=(t // tm,),
        in_specs=[o_spec] * 3 + [l_spec] * 3,
        out_specs=o_spec,
        out_shape=jax.ShapeDtypeStruct((t, width), BF16),
        compiler_params=_params("parallel"),
        name="combine_branches",
    )(*outs, *lses)


def _pool_kernel(u_ref, uh_ref, w_ref, sc_ref, o_ref, wb_ref, *, blocks_per_seq):
    i = pl.program_id(0)

    @pl.when(i == 0)
    def _():
        wb_ref[...] = w_ref[...].astype(BF16)

    tm = u_ref.shape[0]
    group = w_ref.shape[1]
    first = (i % blocks_per_seq) == 0
    row = lax.broadcasted_iota(jnp.int32, (tm, 1), 0)
    pos = (i % blocks_per_seq) * tm + row
    for g, win in enumerate(POOL_WINDOWS):
        sl = slice(g * group, (g + 1) * group)
        u = u_ref[:, sl]
        halo = jnp.where(first, 0.0, uh_ref[:, sl])
        s = jnp.concatenate([halo, u], axis=0)
        step = 1
        while step < win:
            s = s + pltpu.roll(s, step, axis=0)
            step *= 2
        count = jnp.minimum(pos + 1, win).astype(F32)
        pooled = s[POOL_HALO:] / count - u
        y = jnp.dot(pooled.astype(BF16), wb_ref[g], preferred_element_type=F32)
        o_ref[:, sl] = (y * sc_ref[:, sl]).astype(o_ref.dtype)


def _pool_mixer(u, pool_w, pool_scale, seq, tm=1024):
    t, width = u.shape
    n_group, group, _ = pool_w.shape
    blocks_per_seq = seq // tm
    halo_blocks = tm // POOL_HALO
    return pl.pallas_call(
        functools.partial(_pool_kernel, blocks_per_seq=blocks_per_seq),
        grid=(t // tm,),
        in_specs=[pl.BlockSpec((tm, width), lambda i: (i, 0)),
                  pl.BlockSpec((POOL_HALO, width), lambda i: (jnp.maximum(i * halo_blocks - 1, 0), 0)),
                  pl.BlockSpec((n_group, group, group), lambda i: (0, 0, 0)),
                  pl.BlockSpec((1, width), lambda i: (0, 0))],
        out_specs=pl.BlockSpec((tm, width), lambda i: (i, 0)),
        out_shape=jax.ShapeDtypeStruct((t, width), BF16),
        scratch_shapes=[pltpu.VMEM((n_group, group, group), BF16)],
        compiler_params=_params("arbitrary"),
        name="pool_mixer",
    )(u, u, pool_w, pool_scale.reshape(1, width))


def _split3(x):
    hi = x.astype(BF16)
    r1 = x - hi.astype(F32)
    mid = r1.astype(BF16)
    lo = (r1 - mid.astype(F32)).astype(BF16)
    return hi, mid, lo


def _gate_kernel(h_ref, ng_ref, wr_ref, wa2_ref, ba_ref, hn_ref, o_ref):
    x = h_ref[...]
    ms = jnp.mean(x * x, axis=-1, keepdims=True)
    hn = (x * lax.rsqrt(ms + EPS) * ng_ref[...]).astype(hn_ref.dtype)
    hn_ref[...] = hn
    r = lax.dot_general(hn, wr_ref[...].astype(BF16), (((1,), (1,)), ((), ())), preferred_element_type=F32)
    r_hi, r_mid, _ = _split3(r)
    w_hi, w_mid, _ = _split3(wa2_ref[...])
    g = (jnp.dot(r_hi, w_hi, preferred_element_type=F32)
         + (jnp.dot(r_hi, w_mid, preferred_element_type=F32) + jnp.dot(r_mid, w_hi, preferred_element_type=F32)))
    g = g + ba_ref[...]
    log_sig = jnp.minimum(g, 0.0) - jnp.log1p(jnp.exp(-jnp.abs(g)))
    o_ref[...] = log_sig / C_GATE_TAU


def _gla_norm_log_decay(h, norm_gain, w_r, w_a2, b_a, tm=512):
    t, d = h.shape
    kw = w_a2.shape[1]
    return pl.pallas_call(
        _gate_kernel,
        grid=(t // tm,),
        in_specs=[pl.BlockSpec((tm, d), lambda i: (i, 0)),
                  pl.BlockSpec((1, d), lambda i: (0, 0)),
                  pl.BlockSpec((LANES, d), lambda i: (0, 0)),
                  pl.BlockSpec((LANES, kw), lambda i: (0, 0)),
                  pl.BlockSpec((1, kw), lambda i: (0, 0))],
        out_specs=[pl.BlockSpec((tm, d), lambda i: (i, 0)),
                   pl.BlockSpec((tm, kw), lambda i: (i, 0))],
        out_shape=[jax.ShapeDtypeStruct((t, d), BF16),
                   jax.ShapeDtypeStruct((t, kw), F32)],
        compiler_params=_params("parallel"),
        name="gla_norm_log_decay",
    )(h, norm_gain.reshape(1, d), w_r, w_a2, b_a.reshape(1, kw))


def _gla_kernel(q_ref, k_ref, v_ref, la_ref, gate_ref, og_ref, o_ref, state_ref, *, chunks, heads):
    @pl.when(pl.program_id(2) == 0)
    def _():
        state_ref[...] = jnp.zeros_like(state_ref)

    c = C_CHUNK
    dk = q_ref.shape[1] // heads
    dv = v_ref.shape[1] // heads
    ri = lax.broadcasted_iota(jnp.int32, (c, c), 0)
    ci = lax.broadcasted_iota(jnp.int32, (c, c), 1)
    causal = ci <= ri
    tri = causal.astype(BF16)
    contract_last = (((1,), (1,)), ((), ()))
    contract_first = (((0,), (0,)), ((), ()))
    pairs = [(n, h) for n in range(chunks) for h in range(heads)]

    def rows(n):
        return slice(n * c, (n + 1) * c)

    def kcols(h):
        return slice(h * dk, (h + 1) * dk)

    def vcols(h):
        return slice(h * dv, (h + 1) * dv)

    bc = {}
    for n, h in pairs:
        la_hi, la_mid, la_lo = _split3(la_ref[rows(n), kcols(h)])
        bc[n, h] = (jnp.dot(tri, la_hi, preferred_element_type=F32)
                    + jnp.dot(tri, la_mid, preferred_element_type=F32)
                    + jnp.dot(tri, la_lo, preferred_element_type=F32))
    q_b, k_in, k_st, decay = {}, {}, {}, {}
    for n, h in pairs:
        b = bc[n, h]
        b_last = b[c - 1:c, :]
        q_b[n, h] = (q_ref[rows(n), kcols(h)] * (dk ** -0.5) * jnp.exp(b)).astype(BF16)
        k = k_ref[rows(n), kcols(h)]
        k_in[n, h] = (k * jnp.exp(-b)).astype(BF16)
        k_st[n, h] = (k * jnp.exp(b_last - b)).astype(BF16)
        decay[n, h] = jnp.exp(b_last)
    att = {}
    for p in pairs:
        a = lax.dot_general(q_b[p], k_in[p], contract_last, preferred_element_type=F32)
        att[p] = jnp.where(causal, a, 0.0).astype(BF16)
    o_intra, upd = {}, {}
    for n, h in pairs:
        v = v_ref[rows(n), vcols(h)]
        o_intra[n, h] = jnp.dot(att[n, h], v, preferred_element_type=F32)
        upd[n, h] = lax.dot_general(v, k_st[n, h], contract_first, preferred_element_type=F32)
    for h in range(heads):
        state = state_ref[h]
        for n in range(chunks):
            o = o_intra[n, h] + lax.dot_general(q_b[n, h], state.astype(BF16), contract_last,
                                                preferred_element_type=F32)
            state = state * decay[n, h] + upd[n, h]
            ms = jnp.mean(o * o, axis=-1, keepdims=True)
            gate = gate_ref[rows(n), vcols(h)]
            y = o * lax.rsqrt(ms + EPS) * og_ref[...] * (gate * jax.nn.sigmoid(gate))
            o_ref[rows(n), vcols(h)] = y.astype(o_ref.dtype)
        state_ref[h] = state


def _gla(qk, v, gate, log_a, o_gain, bsz, seq, rows=512, heads=2):
    t = bsz * seq
    dk = log_a.shape[1] // C_HEADS
    dv = v.shape[1] // C_HEADS
    steps = seq // rows
    groups = C_HEADS // heads

    def row_block(b, n):
        return b * steps + n

    return pl.pallas_call(
        functools.partial(_gla_kernel, chunks=rows // C_CHUNK, heads=heads),
        grid=(bsz, groups, steps),
        in_specs=[pl.BlockSpec((rows, heads * dk), lambda b, g, n: (row_block(b, n), g)),
                  pl.BlockSpec((rows, heads * dk), lambda b, g, n: (row_block(b, n), groups + g)),
                  pl.BlockSpec((rows, heads * dv), lambda b, g, n: (row_block(b, n), g)),
                  pl.BlockSpec((rows, heads * dk), lambda b, g, n: (row_block(b, n), g)),
                  pl.BlockSpec((rows, heads * dv), lambda b, g, n: (row_block(b, n), g)),
                  pl.BlockSpec((1, dv), lambda b, g, n: (0, 0))],
        out_specs=pl.BlockSpec((rows, heads * dv), lambda b, g, n: (row_block(b, n), g)),
        out_shape=jax.ShapeDtypeStruct((t, v.shape[1]), BF16),
        scratch_shapes=[pltpu.VMEM((heads, dv, dk), F32)],
        compiler_params=_params("parallel", "parallel", "arbitrary"),
        name="gla",
    )(qk, qk, v, log_a, gate, o_gain.reshape(1, dv))


def _xattn_kernel(h_ref, g_ref, wq_ref, qg_ref, k_ref, v_ref, wo_ref, fg_ref, o_ref, on_ref):
    x = h_ref[...]
    ms = jnp.mean(x * x, axis=-1, keepdims=True)
    hn = (x * lax.rsqrt(ms + EPS) * g_ref[...]).astype(BF16)
    q = jnp.dot(hn, wq_ref[...], preferred_element_type=F32)
    scale = HEAD_DIM ** -0.5
    contract_last = (((1,), (1,)), ((), ()))
    cols = [slice(hd * HEAD_DIM, (hd + 1) * HEAD_DIM) for hd in range(X_HEADS)]
    q_heads = []
    for sl in cols:
        qh = q[:, sl]
        qms = jnp.mean(qh * qh, axis=-1, keepdims=True)
        q_heads.append((qh * lax.rsqrt(qms + EPS) * qg_ref[...]).astype(BF16))
    scores = [lax.dot_general(qh, k_ref[0, :, sl], contract_last, preferred_element_type=F32) * scale
              for qh, sl in zip(q_heads, cols)]
    probs = []
    for s in scores:
        m = jnp.max(s, axis=-1, keepdims=True)
        p = jnp.exp(s - m)
        probs.append((p / jnp.sum(p, axis=-1, keepdims=True)).astype(BF16))
    heads = [jnp.dot(p, v_ref[0, :, sl], preferred_element_type=F32) for p, sl in zip(probs, cols)]
    o = jnp.concatenate(heads, axis=-1).astype(BF16)
    y = x + jnp.dot(o, wo_ref[...], preferred_element_type=F32)
    o_ref[...] = y
    yms = jnp.mean(y * y, axis=-1, keepdims=True)
    on_ref[...] = (y * lax.rsqrt(yms + EPS) * fg_ref[...]).astype(on_ref.dtype)


def _cross_attention(h, norm_gain, wq, q_gain, k, v, wo, next_gain, seq, tm=512):
    t, d = h.shape
    xw = wq.shape[1]
    mlen = k.shape[1]
    tiles_per_seq = seq // tm
    return pl.pallas_call(
        _xattn_kernel,
        grid=(t // tm,),
        in_specs=[pl.BlockSpec((tm, d), lambda i: (i, 0)),
                  pl.BlockSpec((1, d), lambda i: (0, 0)),
                  pl.BlockSpec((d, xw), lambda i: (0, 0)),
                  pl.BlockSpec((1, HEAD_DIM), lambda i: (0, 0)),
                  pl.BlockSpec((1, mlen, xw), lambda i: (i // tiles_per_seq, 0, 0)),
                  pl.BlockSpec((1, mlen, xw), lambda i: (i // tiles_per_seq, 0, 0)),
                  pl.BlockSpec((xw, d), lambda i: (0, 0)),
                  pl.BlockSpec((1, d), lambda i: (0, 0))],
        out_specs=[pl.BlockSpec((tm, d), lambda i: (i, 0)),
                   pl.BlockSpec((tm, d), lambda i: (i, 0))],
        out_shape=[jax.ShapeDtypeStruct((t, d), F32),
                   jax.ShapeDtypeStruct((t, d), BF16)],
        compiler_params=_params("parallel"),
        name="cross_attention",
    )(h, norm_gain.reshape(1, d), wq, q_gain.reshape(1, HEAD_DIM), k, v, wo, next_gain.reshape(1, d))


def _ffn_up_kernel(a_ref, ah_ref, wg_ref, wv_ref, cwg_ref, cwv_ref, cbg_ref, cbv_ref, wd_ref, o_ref, wdo_ref, *,
                   blocks_per_seq):
    wdo_ref[...] = wd_ref[...].astype(wdo_ref.dtype)
    tm = a_ref.shape[0]
    first = (pl.program_id(0) % blocks_per_seq) == 0
    row = lax.broadcasted_iota(jnp.int32, (tm, 1), 0)
    a = a_ref[...]
    ah = ah_ref[...]

    def conv_half(w_ref, cw_ref, cb_ref):
        w = w_ref[...].astype(BF16)
        u = jnp.dot(a, w, preferred_element_type=F32)
        uh = jnp.where(first, 0.0, jnp.dot(ah, w, preferred_element_type=F32))
        u1 = jnp.where(row == 0, uh[CONV_HALO - 1:CONV_HALO], pltpu.roll(u, 1, axis=0))
        u2 = jnp.where(row == 0, uh[CONV_HALO - 2:CONV_HALO - 1],
                       jnp.where(row == 1, uh[CONV_HALO - 1:CONV_HALO], pltpu.roll(u, 2, axis=0)))
        cw = cw_ref[...]
        return cb_ref[...] + cw[0:1] * u2 + cw[1:2] * u1 + cw[2:3] * u

    cg = conv_half(wg_ref, cwg_ref, cbg_ref)
    cv = conv_half(wv_ref, cwv_ref, cbv_ref)
    o_ref[...] = (cg * jax.nn.sigmoid(cg) * cv).astype(o_ref.dtype)


def _ffn_up(hn, w_up, conv_w, conv_b, w_down, layer, seq, tm=FFN_ROWS, tn=256):
    t, d = hn.shape
    d_ff = w_up.shape[2] // 2
    n_tiles = d_ff // tn
    n_steps = (t // tm) * n_tiles
    slab = d_ff // n_steps
    assert d_ff % n_steps == 0 and slab % (2 * SUBLANES) == 0
    blocks_per_seq = seq // tm
    halo_blocks = tm // CONV_HALO
    cw = conv_w[layer]
    cb = conv_b[layer].reshape(1, 2 * d_ff)
    return pl.pallas_call(
        functools.partial(_ffn_up_kernel, blocks_per_seq=blocks_per_seq),
        grid=(t // tm, n_tiles),
        in_specs=[pl.BlockSpec((tm, d), lambda i, j: (i, 0), pipeline_mode=pl.Buffered(1)),
                  pl.BlockSpec((CONV_HALO, d), lambda i, j: (jnp.maximum(i * halo_blocks - 1, 0), 0)),
                  pl.BlockSpec((None, d, tn), lambda i, j: (layer, 0, j)),
                  pl.BlockSpec((None, d, tn), lambda i, j: (layer, 0, n_tiles + j)),
                  pl.BlockSpec((CONV_WIDTH, tn), lambda i, j: (0, j)),
                  pl.BlockSpec((CONV_WIDTH, tn), lambda i, j: (0, n_tiles + j)),
                  pl.BlockSpec((1, tn), lambda i, j: (0, j)),
                  pl.BlockSpec((1, tn), lambda i, j: (0, n_tiles + j)),
                  pl.BlockSpec((None, slab, d), lambda i, j: (layer, i * n_tiles + j, 0))],
        out_specs=[pl.BlockSpec((tm, tn), lambda i, j: (i, j)),
                   pl.BlockSpec((slab, d), lambda i, j: (i * n_tiles + j, 0))],
        out_shape=[jax.ShapeDtypeStruct((t, d_ff), BF16),
                   jax.ShapeDtypeStruct((d_ff, d), BF16)],
        compiler_params=_params("parallel", "arbitrary"),
        name="ffn_up",
    )(hn, hn, w_up, w_up, cw, cw, cb, cb, w_down)


def _dilated_pool_layer(h, hn, w_in, q_gain, k_gain, pool_w, pool_scale, w_out, j, bsz, seq):
    d = h.shape[1]
    a_width = d // 2
    heads = a_width // HEAD_DIM
    tn = 512
    q_blocks = a_width // tn
    k_base = N_BRANCH * q_blocks
    v_cols = 2 * N_BRANCH * a_width
    gains = jnp.concatenate([jnp.tile(q_gain, heads), jnp.tile(k_gain, heads)]).reshape(1, 2 * a_width)
    dils = tuple(dil for _, dil in A_BRANCHES)
    v_list = _mm([hn], w_in, j, lambda c: v_cols // tn + c, a_width, BF16, dils=dils, seq=seq, tn=tn,
                 name="ab_in_v")
    u = _mm([hn], w_in, j, lambda c: (v_cols + a_width) // tn + c, d - a_width, F32, tn=tn, name="ab_in_u")
    slopes = _alibi_slopes(heads)
    outs, lses = [], []
    for g, dil in enumerate(dils):
        def wcol(c, g=g):
            return jnp.where(c < q_blocks, g * q_blocks + c, k_base + g * q_blocks + c - q_blocks)
        qk = _mm([hn], w_in, j, wcol, 2 * a_width, BF16, mode="headnorm", extra=gains, dils=(dil,), seq=seq,
                 tn=tn, name=f"ab_in_qk{g}")
        if dil == 1:
            qk = qk.reshape(bsz, 1, seq, 2 * a_width)
            v_g = v_list[g].reshape(bsz, 1, seq, a_width)
        else:
            v_g = v_list[g]
        o, lse = _band_attention(qk, v_g, bsz, seq, g, slopes[g])
        outs.append(o)
        lses.append(lse)
    a_out = _combine_branches(outs, lses)
    b_out = _pool_mixer(u, pool_w, pool_scale, seq)
    return _mm([a_out, b_out], w_out, j, lambda c: c, d, F32, mode="residual", extra=h, name="ab_out")


def _gla_layer(h, norm_gain, w_in, w_a2, b_a, o_gain, w_out, j, bsz, seq):
    d = h.shape[1]
    kw = w_a2.shape[2]
    vw = w_out.shape[1]
    tn = 512
    w_in_t = jnp.swapaxes(w_in, 1, 2)
    rank = w_a2.shape[1]
    w_r = jnp.pad(w_in_t[j, 2 * kw + 2 * vw:, :], ((0, LANES - rank), (0, 0)))
    w_a2p = jnp.pad(w_a2[j], ((0, LANES - rank), (0, 0)))
    hn, log_a = _gla_norm_log_decay(h, norm_gain, w_r, w_a2p, b_a[j])
    qk = _mm([hn], w_in_t, j, lambda c: c, 2 * kw, F32, w_t=True, tn=tn, name="c_in_qk")
    v = _mm([hn], w_in_t, j, lambda c: 2 * kw // tn + c, vw, BF16, w_t=True, tn=tn, name="c_in_v")
    gate = _mm([hn], w_in_t, j, lambda c: (2 * kw + vw) // tn + c, vw, F32, w_t=True, tn=tn, name="c_in_gate")
    o = _gla(qk, v, gate, log_a, o_gain[j], bsz, seq)
    return _mm([o], w_out, j, lambda c: c, d, F32, mode="residual", extra=h, name="c_out")


def _memory_kv(mem, gain, wkv, k_gain, layer, bsz):
    xw = wkv.shape[2] // 2
    tn = 256
    mem_n = _rmsnorm(mem, gain)
    rows = mem.shape[0]
    gains = jnp.tile(k_gain, xw // HEAD_DIM).reshape(1, xw)
    k = _mm([mem_n], wkv, layer, lambda c: c, xw, BF16, mode="headnorm", extra=gains, tm=rows, tn=tn, name="mem_k")
    v = _mm([mem_n], wkv, layer, lambda c: xw // tn + c, xw, BF16, tm=rows, tn=tn, name="mem_v")
    return k.reshape(bsz, rows // bsz, xw), v.reshape(bsz, rows // bsz, xw)


def kernel(x, mem, mix_norm, ab_w_in, ab_q_norm, ab_k_norm, ab_pool_w, ab_pool_scale, ab_w_out, c_w_in, c_w_a2, c_b_a, c_o_norm, c_w_out, x_norm, x_mem_norm, x_wq, x_wkv, x_q_norm, x_k_norm, x_wo, f_norm, f_w_up, f_conv_w, f_conv_b, f_w_down):
    bsz, seq, d = x.shape
    depth = mix_norm.shape[0]
    h = x.reshape(bsz * seq, d)
    mem2 = mem.reshape(bsz * mem.shape[1], d)
    for layer in range(depth):
        j = layer // 2
        if layer % 2 == 0:
            hn = _rmsnorm(h, mix_norm[layer])
            h = _dilated_pool_layer(h, hn, ab_w_in, ab_q_norm[j], ab_k_norm[j], ab_pool_w[j], ab_pool_scale[j],
                                    ab_w_out, j, bsz, seq)
        else:
            h = _gla_layer(h, mix_norm[layer], c_w_in, c_w_a2, c_b_a, c_o_norm, c_w_out, j, bsz, seq)
        k, v = _memory_kv(mem2, x_mem_norm[layer], x_wkv, x_k_norm[layer], layer, bsz)
        h, hn = _cross_attention(h, x_norm[layer], x_wq[layer].astype(BF16), x_q_norm[layer], k, v,
                                 x_wo[layer].astype(BF16), f_norm[layer], seq)
        act, w_down = _ffn_up(hn, f_w_up, f_conv_w, f_conv_b, f_w_down, layer, seq)
        h = _mm([act], w_down, None, lambda c: c, d, F32, mode="residual", extra=h, tm=512, name="ffn_down")
    return h.reshape(bsz, seq, d)
```

```python
import functools

import numpy as np
import jax
import jax.numpy as jnp
from jax import lax
from jax.experimental import pallas as pl
from jax.experimental.pallas import tpu as pltpu

F32 = jnp.float32
BF16 = jnp.bfloat16

LANES = 128
SUBLANES = 8
VMEM_LIMIT_BYTES = 56 * 2 ** 20

EPS = 1e-6
HEAD_DIM = 128
A_BRANCHES = ((128, 1), (512, 4), (2048, 16))
N_BRANCH = len(A_BRANCHES)
BAND_BLOCK = 128
POOL_WINDOWS = (2, 4, 8, 16)
POOL_HALO = 16
C_HEADS = 8
C_GATE_RANK = 16
C_GATE_TAU = 16.0
C_CHUNK = 64
X_HEADS = 4
CONV_WIDTH = 3
CONV_HALO = SUBLANES
MM_ROWS = 1024
FAST_ROW_STRIDE = 4
ATTN_STAGE_HEADS = {1: 8, 4: 16, 16: 1}
ATTN_PAIRS_PER_STEP = 64
FFN_ROWS = 2048


def _params(*semantics):
    return pltpu.CompilerParams(dimension_semantics=semantics, vmem_limit_bytes=VMEM_LIMIT_BYTES)


def _lane_groups(width):
    return [slice(c * LANES, (c + 1) * LANES) for c in range(width // LANES)]


def _rmsnorm_kernel(x_ref, g_ref, o_ref):
    x = x_ref[...].astype(F32)
    ms = jnp.mean(x * x, axis=-1, keepdims=True)
    o_ref[...] = (x * lax.rsqrt(ms + EPS) * g_ref[...]).astype(o_ref.dtype)


def _rmsnorm(x, gain, tm=512):
    t, d = x.shape
    return pl.pallas_call(
        _rmsnorm_kernel,
        grid=(t // tm,),
        in_specs=[pl.BlockSpec((tm, d), lambda i: (i, 0)),
                  pl.BlockSpec((1, d), lambda i: (0, 0))],
        out_specs=pl.BlockSpec((tm, d), lambda i: (i, 0)),
        out_shape=jax.ShapeDtypeStruct((t, d), BF16),
        compiler_params=_params("parallel"),
        name="rmsnorm",
    )(x, gain.reshape(1, d))


def _mm_kernel(*refs, n_a, mode, dils, w_t):
    a_refs = refs[:n_a]
    w_ref = refs[n_a]
    n_extra = 1 if mode in ("headnorm", "residual") else 0
    extra = refs[n_a + 1:n_a + 1 + n_extra]
    o_refs = refs[n_a + 1 + n_extra:n_a + 1 + n_extra + len(dils)]
    scratch = refs[n_a + 1 + n_extra + len(dils):]
    acc = None
    k0 = 0
    for a_ref in a_refs:
        kk = a_ref.shape[1]
        if w_t:
            part = lax.dot_general(a_ref[...], w_ref[:, k0:k0 + kk].astype(BF16), (((1,), (1,)), ((), ())),
                                   preferred_element_type=F32)
        else:
            part = jnp.dot(a_ref[...], w_ref[k0:k0 + kk, :].astype(BF16), preferred_element_type=F32)
        acc = part if acc is None else acc + part
        k0 += kk
    tm, tn = acc.shape
    for c, sl in enumerate(_lane_groups(tn)):
        blk = acc[:, sl]
        if mode == "headnorm":
            ms = jnp.mean(blk * blk, axis=-1, keepdims=True)
            blk = blk * lax.rsqrt(ms + EPS) * extra[0][:, sl]
        elif mode == "residual":
            blk = extra[0][:, sl] + blk
        if scratch:
            scratch[0][c] = blk
        for o_ref, dil in zip(o_refs, dils):
            if dil == 1:
                o_ref[:, sl] = blk.astype(o_ref.dtype)
    for o_ref, dil in zip(o_refs, dils):
        if 1 < dil <= FAST_ROW_STRIDE:
            for r in range(dil):
                for c, sl in enumerate(_lane_groups(tn)):
                    o_ref[r, :, sl] = scratch[0][c, pl.ds(r, tm // dil, stride=dil), :].astype(o_ref.dtype)
        elif dil > FAST_ROW_STRIDE:
            s1, s2 = FAST_ROW_STRIDE, dil // FAST_ROW_STRIDE
            part = tm // s1
            for q in range(s1):
                for c in range(tn // LANES):
                    scratch[1][c, q * part:(q + 1) * part, :] = scratch[0][c, pl.ds(q, part, stride=s1), :]
            for q in range(s1):
                for p in range(s2):
                    for c, sl in enumerate(_lane_groups(tn)):
                        o_ref[s1 * p + q, :, sl] = scratch[1][c, pl.ds(q * part + p, tm // dil, stride=s2),
                                                              :].astype(o_ref.dtype)


def _mm(a_list, w, layer, wcol, ncols, out_dtype, *, mode="plain", extra=None, dils=(1,), seq=None,
        w_t=False, tm=MM_ROWS, tn=512, name="mm"):
    t = a_list[0].shape[0]
    k_total = sum(a.shape[1] for a in a_list)
    assert w.shape[-1 if w_t else -2] == k_total and ncols % tn == 0 and t % tm == 0
    in_specs = [pl.BlockSpec((tm, a.shape[1]), lambda i, j: (i, 0)) for a in a_list]
    if w_t:
        in_specs.append(pl.BlockSpec((None, tn, k_total), lambda i, j: (layer, wcol(j), 0)))
    elif w.ndim == 3:
        in_specs.append(pl.BlockSpec((None, k_total, tn), lambda i, j: (layer, 0, wcol(j))))
    else:
        in_specs.append(pl.BlockSpec((k_total, tn), lambda i, j: (0, wcol(j))))
    args = list(a_list) + [w]
    if mode == "headnorm":
        in_specs.append(pl.BlockSpec((1, tn), lambda i, j: (0, j)))
        args.append(extra)
    elif mode == "residual":
        in_specs.append(pl.BlockSpec((tm, tn), lambda i, j: (i, j)))
        args.append(extra)
    out_specs, out_shapes = [], []
    for dil in dils:
        if dil == 1:
            out_specs.append(pl.BlockSpec((tm, tn), lambda i, j: (i, j)))
            out_shapes.append(jax.ShapeDtypeStruct((t, ncols), out_dtype))
        else:
            tiles = seq // tm
            assert seq % tm == 0 and tm % dil == 0
            out_specs.append(pl.BlockSpec((None, dil, tm // dil, tn), lambda i, j: (i // tiles, 0, i % tiles, j)))
            out_shapes.append(jax.ShapeDtypeStruct((t // seq, dil, seq // dil, ncols), out_dtype))
    assert all(dil <= FAST_ROW_STRIDE ** 2 and (dil <= FAST_ROW_STRIDE or dil % FAST_ROW_STRIDE == 0) for dil in dils)
    slab = pltpu.VMEM((tn // LANES, tm, LANES), F32)
    n_slabs = 2 if max(dils) > FAST_ROW_STRIDE else 1 if max(dils) > 1 else 0
    scratch = [slab] * n_slabs
    outs = pl.pallas_call(
        functools.partial(_mm_kernel, n_a=len(a_list), mode=mode, dils=tuple(dils), w_t=w_t),
        grid=(t // tm, ncols // tn),
        in_specs=in_specs,
        out_specs=out_specs,
        out_shape=out_shapes,
        scratch_shapes=scratch,
        compiler_params=_params("parallel", "arbitrary"),
        name=name,
    )(*args)
    return outs[0] if len(dils) == 1 else outs


def _alibi_slopes(heads):
    n = N_BRANCH * heads
    s = np.power(np.float32(2.0), -8.0 * np.arange(1, n + 1, dtype=np.float32) / np.float32(n)).astype(np.float32)
    return s.reshape(N_BRANCH, -1)


def _band_attn_kernel(q_ref, kp_ref, kc_ref, vp_ref, vc_ref, o_ref, lse_ref, *scratch, dilation, slopes,
                      stage_heads, sub):
    blk = BAND_BLOCK
    hg = pl.program_id(2)
    has_prev = pl.program_id(1) > 0
    n_heads = len(slopes[0])
    qi = lax.broadcasted_iota(jnp.int32, (blk, 2 * blk), 0)
    ki = lax.broadcasted_iota(jnp.int32, (blk, 2 * blk), 1)
    rel = qi + blk - ki
    in_band = jnp.logical_and(rel >= 0, rel <= blk)
    valid_first = jnp.logical_and(in_band, jnp.logical_or(ki >= blk, has_prev))
    dist = (rel * dilation).astype(F32)
    lane = lax.broadcasted_iota(jnp.int32, (blk, LANES), 1)
    scale = HEAD_DIM ** -0.5
    contract_last = (((1,), (1,)), ((), ()))
    head_cols = [slice(h * HEAD_DIM, (h + 1) * HEAD_DIM) for h in range(n_heads)]
    two_pass = dilation > FAST_ROW_STRIDE
    s1, s2 = FAST_ROW_STRIDE, dilation // FAST_ROW_STRIDE
    part = blk * dilation // s1
    head_slope = []
    for h in range(n_heads):
        slope = slopes[0][h]
        for g in range(1, len(slopes)):
            slope = jnp.where(hg == g, slopes[g][h], slope)
        head_slope.append(slope)
    lse_tiles = []
    for r, m in [(r, m) for r in range(dilation) for m in range(sub)]:
        rows = slice(m * blk, (m + 1) * blk)
        prev_rows = slice((m - 1) * blk, m * blk)
        valid = valid_first if m == 0 else in_band

        def keys(prev_ref, cur_ref, cols):
            prev = prev_ref[r, :, cols] if m == 0 else cur_ref[r, prev_rows, cols]
            return jnp.concatenate([prev, cur_ref[r, rows, cols]], axis=0)

        lse_tile = jnp.zeros((blk, LANES), F32)
        for h0 in range(0, n_heads, stage_heads):
            group = list(range(h0, min(h0 + stage_heads, n_heads)))
            scores = {h: lax.dot_general(q_ref[r, rows, head_cols[h]], keys(kp_ref, kc_ref, head_cols[h]),
                                         contract_last, preferred_element_type=F32) for h in group}
            probs, dens, maxes = {}, {}, {}
            for h in group:
                s = jnp.where(valid, scores[h] * scale - head_slope[h] * dist, -jnp.inf)
                maxes[h] = jnp.max(s, axis=-1, keepdims=True)
                p = jnp.exp(s - maxes[h])
                dens[h] = jnp.sum(p, axis=-1, keepdims=True)
                probs[h] = p.astype(BF16)
            outs = {h: jnp.dot(probs[h], keys(vp_ref, vc_ref, head_cols[h]), preferred_element_type=F32)
                    for h in group}
            for h in group:
                o = outs[h] / dens[h]
                lse_h = maxes[h] + jnp.log(dens[h])
                for g in range(len(slopes)):
                    lse_tile = jnp.where(jnp.logical_and(lane == g * n_heads + h, hg == g), lse_h, lse_tile)
                if dilation == 1:
                    o_ref[rows, head_cols[h]] = o.astype(o_ref.dtype)
                elif two_pass:
                    scratch[0][h, pl.ds((r % s1) * part + r // s1, blk, stride=s2), :] = o
                else:
                    scratch[0][h, pl.ds(r, blk, stride=dilation), :] = o
        if dilation == 1:
            lse_tiles.append(lse_tile)
        elif two_pass:
            scratch[1][pl.ds((r % s1) * part + r // s1, blk, stride=s2), :] = lse_tile
        else:
            scratch[1][pl.ds(r, blk, stride=dilation), :] = lse_tile
    if dilation == 1:
        lse_slab = jnp.concatenate(lse_tiles, axis=0)
    elif two_pass:
        for q in range(s1):
            for h in range(n_heads):
                scratch[2][h, pl.ds(q, part, stride=s1), :] = scratch[0][h, q * part:(q + 1) * part, :]
            scratch[3][pl.ds(q, part, stride=s1), :] = scratch[1][q * part:(q + 1) * part, :]
        for h in range(n_heads):
            o_ref[:, h * HEAD_DIM:(h + 1) * HEAD_DIM] = scratch[2][h].astype(o_ref.dtype)
        lse_slab = scratch[3][...]
    else:
        for h in range(n_heads):
            o_ref[:, h * HEAD_DIM:(h + 1) * HEAD_DIM] = scratch[0][h].astype(o_ref.dtype)
        lse_slab = scratch[1][...]

    @pl.when(hg == 0)
    def _():
        lse_ref[...] = lse_slab

    @pl.when(hg > 0)
    def _():
        lse_ref[...] += lse_slab


def _band_attention(qk, v, bsz, seq, branch, slopes):
    _, dilation = A_BRANCHES[branch]
    width = v.shape[-1]
    n_blk = seq // dilation // BAND_BLOCK
    heads = max(min(width // HEAD_DIM, ATTN_PAIRS_PER_STEP // dilation), 1)
    hw = heads * HEAD_DIM
    n_hg = width // hw
    sub = ATTN_PAIRS_PER_STEP // heads if dilation == 1 else 1
    assert n_blk % sub == 0
    n_steps = n_blk // sub
    slope_tab = tuple(tuple(float(s) for s in slopes[g * heads:(g + 1) * heads]) for g in range(n_hg))
    rows = BAND_BLOCK * dilation * sub
    blk = (None, dilation, sub * BAND_BLOCK, hw)
    prev_blk = (None, dilation, BAND_BLOCK, hw)

    def prev(n):
        return jnp.maximum(n * sub - 1, 0)

    scratch = []
    if dilation > 1:
        scratch = [pltpu.VMEM((heads, rows, HEAD_DIM), F32), pltpu.VMEM((rows, LANES), F32)]
        if dilation > FAST_ROW_STRIDE:
            scratch = scratch * 2
    return pl.pallas_call(
        functools.partial(_band_attn_kernel, dilation=dilation, slopes=slope_tab,
                          stage_heads=ATTN_STAGE_HEADS[dilation], sub=sub),
        grid=(bsz, n_steps, n_hg),
        in_specs=[pl.BlockSpec(blk, lambda b, n, g: (b, 0, n, g)),
                  pl.BlockSpec(prev_blk, lambda b, n, g: (b, 0, prev(n), n_hg + g)),
                  pl.BlockSpec(blk, lambda b, n, g: (b, 0, n, n_hg + g)),
                  pl.BlockSpec(prev_blk, lambda b, n, g: (b, 0, prev(n), g)),
                  pl.BlockSpec(blk, lambda b, n, g: (b, 0, n, g))],
        out_specs=[pl.BlockSpec((rows, hw), lambda b, n, g: (b * n_steps + n, g)),
                   pl.BlockSpec((rows, LANES), lambda b, n, g: (b * n_steps + n, 0))],
        out_shape=[jax.ShapeDtypeStruct((bsz * seq, width), BF16),
                   jax.ShapeDtypeStruct((bsz * seq, LANES), F32)],
        scratch_shapes=scratch,
        compiler_params=_params("parallel", "arbitrary", "arbitrary"),
        name=f"band_attn_d{dilation}",
    )(qk, qk, qk, v, v)


def _combine_kernel(o0_ref, o1_ref, o2_ref, l0_ref, l1_ref, l2_ref, out_ref):
    l0, l1, l2 = l0_ref[...], l1_ref[...], l2_ref[...]
    m = jnp.maximum(jnp.maximum(l0, l1), l2)
    e0, e1, e2 = jnp.exp(l0 - m), jnp.exp(l1 - m), jnp.exp(l2 - m)
    tot = e0 + e1 + e2
    w0, w1, w2 = e0 / tot, e1 / tot, e2 / tot
    for h in range(out_ref.shape[1] // HEAD_DIM):
        sl = slice(h * HEAD_DIM, (h + 1) * HEAD_DIM)
        acc = (w0[:, h:h + 1] * o0_ref[:, sl].astype(F32) + w1[:, h:h + 1] * o1_ref[:, sl].astype(F32)
               + w2[:, h:h + 1] * o2_ref[:, sl].astype(F32))
        out_ref[:, sl] = acc.astype(out_ref.dtype)


def _combine_branches(outs, lses, tm=512):
    t, width = outs[0].shape
    o_spec = pl.BlockSpec((tm, width), lambda i: (i, 0))
    l_spec = pl.BlockSpec((tm, LANES), lambda i: (i, 0))
    return pl.pallas_call(
        _combine_kernel,
        grid=(t // tm,),
        in_specs=[o_spec] * 3 + [l_spec] * 3,
        out_specs=o_spec,
        out_shape=jax.ShapeDtypeStruct((t, width), BF16),
        compiler_params=_params("parallel"),
        name="combine_branches",
    )(*outs, *lses)


def _pool_kernel(u_ref, uh_ref, w_ref, sc_ref, o_ref, wb_ref, *, blocks_per_seq):
    i = pl.program_id(0)

    @pl.when(i == 0)
    def _():
        wb_ref[...] = w_ref[...].astype(BF16)

    tm = u_ref.shape[0]
    group = w_ref.shape[1]
    first = (i % blocks_per_seq) == 0
    row = lax.broadcasted_iota(jnp.int32, (tm, 1), 0)
    pos = (i % blocks_per_seq) * tm + row
    for g, win in enumerate(POOL_WINDOWS):
        sl = slice(g * group, (g + 1) * group)
        u = u_ref[:, sl]
        halo = jnp.where(first, 0.0, uh_ref[:, sl])
        s = jnp.concatenate([halo, u], axis=0)
        step = 1
        while step < win:
            s = s + pltpu.roll(s, step, axis=0)
            step *= 2
        count = jnp.minimum(pos + 1, win).astype(F32)
        pooled = s[POOL_HALO:] / count - u
        y = jnp.dot(pooled.astype(BF16), wb_ref[g], preferred_element_type=F32)
        o_ref[:, sl] = (y * sc_ref[:, sl]).astype(o_ref.dtype)


def _pool_mixer(u, pool_w, pool_scale, seq, tm=1024):
    t, width = u.shape
    n_group, group, _ = pool_w.shape
    blocks_per_seq = seq // tm
    halo_blocks = tm // POOL_HALO
    return pl.pallas_call(
        functools.partial(_pool_kernel, blocks_per_seq=blocks_per_seq),
        grid=(t // tm,),
        in_specs=[pl.BlockSpec((tm, width), lambda i: (i, 0)),
                  pl.BlockSpec((POOL_HALO, width), lambda i: (jnp.maximum(i * halo_blocks - 1, 0), 0)),
                  pl.BlockSpec((n_group, group, group), lambda i: (0, 0, 0)),
                  pl.BlockSpec((1, width), lambda i: (0, 0))],
        out_specs=pl.BlockSpec((tm, width), lambda i: (i, 0)),
        out_shape=jax.ShapeDtypeStruct((t, width), BF16),
        scratch_shapes=[pltpu.VMEM((n_group, group, group), BF16)],
        compiler_params=_params("arbitrary"),
        name="pool_mixer",
    )(u, u, pool_w, pool_scale.reshape(1, width))


def _split3(x):
    hi = x.astype(BF16)
    r1 = x - hi.astype(F32)
    mid = r1.astype(BF16)
    lo = (r1 - mid.astype(F32)).astype(BF16)
    return hi, mid, lo


def _gate_kernel(h_ref, ng_ref, wr_ref, wa2_ref, ba_ref, hn_ref, o_ref):
    x = h_ref[...]
    ms = jnp.mean(x * x, axis=-1, keepdims=True)
    hn = (x * lax.rsqrt(ms + EPS) * ng_ref[...]).astype(hn_ref.dtype)
    hn_ref[...] = hn
    r = lax.dot_general(hn, wr_ref[...].astype(BF16), (((1,), (1,)), ((), ())), preferred_element_type=F32)
    r_hi, r_mid, _ = _split3(r)
    w_hi, w_mid, _ = _split3(wa2_ref[...])
    g = (jnp.dot(r_hi, w_hi, preferred_element_type=F32)
         + (jnp.dot(r_hi, w_mid, preferred_element_type=F32) + jnp.dot(r_mid, w_hi, preferred_element_type=F32)))
    g = g + ba_ref[...]
    log_sig = jnp.minimum(g, 0.0) - jnp.log1p(jnp.exp(-jnp.abs(g)))
    o_ref[...] = log_sig / C_GATE_TAU


def _gla_norm_log_decay(h, norm_gain, w_r, w_a2, b_a, tm=512):
    t, d = h.shape
    kw = w_a2.shape[1]
    return pl.pallas_call(
        _gate_kernel,
        grid=(t // tm,),
        in_specs=[pl.BlockSpec((tm, d), lambda i: (i, 0)),
                  pl.BlockSpec((1, d), lambda i: (0, 0)),
                  pl.BlockSpec((LANES, d), lambda i: (0, 0)),
                  pl.BlockSpec((LANES, kw), lambda i: (0, 0)),
                  pl.BlockSpec((1, kw), lambda i: (0, 0))],
        out_specs=[pl.BlockSpec((tm, d), lambda i: (i, 0)),
                   pl.BlockSpec((tm, kw), lambda i: (i, 0))],
        out_shape=[jax.ShapeDtypeStruct((t, d), BF16),
                   jax.ShapeDtypeStruct((t, kw), F32)],
        compiler_params=_params("parallel"),
        name="gla_norm_log_decay",
    )(h, norm_gain.reshape(1, d), w_r, w_a2, b_a.reshape(1, kw))


def _gla_kernel(q_ref, k_ref, v_ref, la_ref, gate_ref, og_ref, o_ref, state_ref, *, chunks, heads):
    @pl.when(pl.program_id(2) == 0)
    def _():
        state_ref[...] = jnp.zeros_like(state_ref)

    c = C_CHUNK
    dk = q_ref.shape[1] // heads
    dv = v_ref.shape[1] // heads
    ri = lax.broadcasted_iota(jnp.int32, (c, c), 0)
    ci = lax.broadcasted_iota(jnp.int32, (c, c), 1)
    causal = ci <= ri
    tri = causal.astype(BF16)
    contract_last = (((1,), (1,)), ((), ()))
    contract_first = (((0,), (0,)), ((), ()))
    pairs = [(n, h) for n in range(chunks) for h in range(heads)]

    def rows(n):
        return slice(n * c, (n + 1) * c)

    def kcols(h):
        return slice(h * dk, (h + 1) * dk)

    def vcols(h):
        return slice(h * dv, (h + 1) * dv)

    bc = {}
    for n, h in pairs:
        la_hi, la_mid, la_lo = _split3(la_ref[rows(n), kcols(h)])
        bc[n, h] = (jnp.dot(tri, la_hi, preferred_element_type=F32)
                    + jnp.dot(tri, la_mid, preferred_element_type=F32)
                    + jnp.dot(tri, la_lo, preferred_element_type=F32))
    q_b, k_in, k_st, decay = {}, {}, {}, {}
    for n, h in pairs:
        b = bc[n, h]
        b_last = b[c - 1:c, :]
        q_b[n, h] = (q_ref[rows(n), kcols(h)] * (dk ** -0.5) * jnp.exp(b)).astype(BF16)
        k = k_ref[rows(n), kcols(h)]
        k_in[n, h] = (k * jnp.exp(-b)).astype(BF16)
        k_st[n, h] = (k * jnp.exp(b_last - b)).astype(BF16)
        decay[n, h] = jnp.exp(b_last)
    att = {}
    for p in pairs:
        a = lax.dot_general(q_b[p], k_in[p], contract_last, preferred_element_type=F32)
        att[p] = jnp.where(causal, a, 0.0).astype(BF16)
    o_intra, upd = {}, {}
    for n, h in pairs:
        v = v_ref[rows(n), vcols(h)]
        o_intra[n, h] = jnp.dot(att[n, h], v, preferred_element_type=F32)
        upd[n, h] = lax.dot_general(v, k_st[n, h], contract_first, preferred_element_type=F32)
    for h in range(heads):
        state = state_ref[h]
        for n in range(chunks):
            o = o_intra[n, h] + lax.dot_general(q_b[n, h], state.astype(BF16), contract_last,
                                                preferred_element_type=F32)
            state = state * decay[n, h] + upd[n, h]
            ms = jnp.mean(o * o, axis=-1, keepdims=True)
            gate = gate_ref[rows(n), vcols(h)]
            y = o * lax.rsqrt(ms + EPS) * og_ref[...] * (gate * jax.nn.sigmoid(gate))
            o_ref[rows(n), vcols(h)] = y.astype(o_ref.dtype)
        state_ref[h] = state


def _gla(qk, v, gate, log_a, o_gain, bsz, seq, rows=1024, heads=2):
    t = bsz * seq
    dk = log_a.shape[1] // C_HEADS
    dv = v.shape[1] // C_HEADS
    steps = seq // rows
    groups = C_HEADS // heads

    def row_block(b, n):
        return b * steps + n

    return pl.pallas_call(
        functools.partial(_gla_kernel, chunks=rows // C_CHUNK, heads=heads),
        grid=(bsz, groups, steps),
        in_specs=[pl.BlockSpec((rows, heads * dk), lambda b, g, n: (row_block(b, n), g)),
                  pl.BlockSpec((rows, heads * dk), lambda b, g, n: (row_block(b, n), groups + g)),
                  pl.BlockSpec((rows, heads * dv), lambda b, g, n: (row_block(b, n), g)),
                  pl.BlockSpec((rows, heads * dk), lambda b, g, n: (row_block(b, n), g)),
                  pl.BlockSpec((rows, heads * dv), lambda b, g, n: (row_block(b, n), g)),
                  pl.BlockSpec((1, dv), lambda b, g, n: (0, 0))],
        out_specs=pl.BlockSpec((rows, heads * dv), lambda b, g, n: (row_block(b, n), g)),
        out_shape=jax.ShapeDtypeStruct((t, v.shape[1]), BF16),
        scratch_shapes=[pltpu.VMEM((heads, dv, dk), F32)],
        compiler_params=_params("parallel", "parallel", "arbitrary"),
        name="gla",
    )(qk, qk, v, log_a, gate, o_gain.reshape(1, dv))


def _xattn_kernel(h_ref, g_ref, wq_ref, qg_ref, k_ref, v_ref, wo_ref, fg_ref, o_ref, on_ref):
    x = h_ref[...]
    ms = jnp.mean(x * x, axis=-1, keepdims=True)
    hn = (x * lax.rsqrt(ms + EPS) * g_ref[...]).astype(BF16)
    q = jnp.dot(hn, wq_ref[...], preferred_element_type=F32)
    scale = HEAD_DIM ** -0.5
    contract_last = (((1,), (1,)), ((), ()))
    cols = [slice(hd * HEAD_DIM, (hd + 1) * HEAD_DIM) for hd in range(X_HEADS)]
    q_heads = []
    for sl in cols:
        qh = q[:, sl]
        qms = jnp.mean(qh * qh, axis=-1, keepdims=True)
        q_heads.append((qh * lax.rsqrt(qms + EPS) * qg_ref[...]).astype(BF16))
    scores = [lax.dot_general(qh, k_ref[0, :, sl], contract_last, preferred_element_type=F32) * scale
              for qh, sl in zip(q_heads, cols)]
    probs = []
    for s in scores:
        m = jnp.max(s, axis=-1, keepdims=True)
        p = jnp.exp(s - m)
        probs.append((p / jnp.sum(p, axis=-1, keepdims=True)).astype(BF16))
    heads = [jnp.dot(p, v_ref[0, :, sl], preferred_element_type=F32) for p, sl in zip(probs, cols)]
    o = jnp.concatenate(heads, axis=-1).astype(BF16)
    y = x + jnp.dot(o, wo_ref[...], preferred_element_type=F32)
    o_ref[...] = y
    yms = jnp.mean(y * y, axis=-1, keepdims=True)
    on_ref[...] = (y * lax.rsqrt(yms + EPS) * fg_ref[...]).astype(on_ref.dtype)


def _cross_attention(h, norm_gain, wq, q_gain, k, v, wo, next_gain, seq, tm=512):
    t, d = h.shape
    xw = wq.shape[1]
    mlen = k.shape[1]
    tiles_per_seq = seq // tm
    return pl.pallas_call(
        _xattn_kernel,
        grid=(t // tm,),
        in_specs=[pl.BlockSpec((tm, d), lambda i: (i, 0)),
                  pl.BlockSpec((1, d), lambda i: (0, 0)),
                  pl.BlockSpec((d, xw), lambda i: (0, 0)),
                  pl.BlockSpec((1, HEAD_DIM), lambda i: (0, 0)),
                  pl.BlockSpec((1, mlen, xw), lambda i: (i // tiles_per_seq, 0, 0)),
                  pl.BlockSpec((1, mlen, xw), lambda i: (i // tiles_per_seq, 0, 0)),
                  pl.BlockSpec((xw, d), lambda i: (0, 0)),
                  pl.BlockSpec((1, d), lambda i: (0, 0))],
        out_specs=[pl.BlockSpec((tm, d), lambda i: (i, 0)),
                   pl.BlockSpec((tm, d), lambda i: (i, 0))],
        out_shape=[jax.ShapeDtypeStruct((t, d), F32),
                   jax.ShapeDtypeStruct((t, d), BF16)],
        compiler_params=_params("parallel"),
        name="cross_attention",
    )(h, norm_gain.reshape(1, d), wq, q_gain.reshape(1, HEAD_DIM), k, v, wo, next_gain.reshape(1, d))


def _ffn_up_kernel(a_ref, ah_ref, wg_ref, wv_ref, cwg_ref, cwv_ref, cbg_ref, cbv_ref, wd_ref, o_ref, wdo_ref, *,
                   blocks_per_seq):
    wdo_ref[...] = wd_ref[...].astype(wdo_ref.dtype)
    tm = a_ref.shape[0]
    first = (pl.program_id(0) % blocks_per_seq) == 0
    row = lax.broadcasted_iota(jnp.int32, (tm, 1), 0)
    a = a_ref[...]
    ah = ah_ref[...]

    def conv_half(w_ref, cw_ref, cb_ref):
        w = w_ref[...].astype(BF16)
        u = jnp.dot(a, w, preferred_element_type=F32)
        uh = jnp.where(first, 0.0, jnp.dot(ah, w, preferred_element_type=F32))
        u1 = jnp.where(row == 0, uh[CONV_HALO - 1:CONV_HALO], pltpu.roll(u, 1, axis=0))
        u2 = jnp.where(row == 0, uh[CONV_HALO - 2:CONV_HALO - 1],
                       jnp.where(row == 1, uh[CONV_HALO - 1:CONV_HALO], pltpu.roll(u, 2, axis=0)))
        cw = cw_ref[...]
        return cb_ref[...] + cw[0:1] * u2 + cw[1:2] * u1 + cw[2:3] * u

    cg = conv_half(wg_ref, cwg_ref, cbg_ref)
    cv = conv_half(wv_ref, cwv_ref, cbv_ref)
    o_ref[...] = (cg * jax.nn.sigmoid(cg) * cv).astype(o_ref.dtype)


def _ffn_up(hn, w_up, conv_w, conv_b, w_down, layer, seq, tm=FFN_ROWS, tn=256):
    t, d = hn.shape
    d_ff = w_up.shape[2] // 2
    n_tiles = d_ff // tn
    n_steps = (t // tm) * n_tiles
    slab = d_ff // n_steps
    assert d_ff % n_steps == 0 and slab % (2 * SUBLANES) == 0
    blocks_per_seq = seq // tm
    halo_blocks = tm // CONV_HALO
    cw = conv_w[layer]
    cb = conv_b[layer].reshape(1, 2 * d_ff)
    return pl.pallas_call(
        functools.partial(_ffn_up_kernel, blocks_per_seq=blocks_per_seq),
        grid=(t // tm, n_tiles),
        in_specs=[pl.BlockSpec((tm, d), lambda i, j: (i, 0), pipeline_mode=pl.Buffered(1)),
                  pl.BlockSpec((CONV_HALO, d), lambda i, j: (jnp.maximum(i * halo_blocks - 1, 0), 0)),
                  pl.BlockSpec((None, d, tn), lambda i, j: (layer, 0, j)),
                  pl.BlockSpec((None, d, tn), lambda i, j: (layer, 0, n_tiles + j)),
                  pl.BlockSpec((CONV_WIDTH, tn), lambda i, j: (0, j)),
                  pl.BlockSpec((CONV_WIDTH, tn), lambda i, j: (0, n_tiles + j)),
                  pl.BlockSpec((1, tn), lambda i, j: (0, j)),
                  pl.BlockSpec((1, tn), lambda i, j: (0, n_tiles + j)),
                  pl.BlockSpec((None, slab, d), lambda i, j: (layer, i * n_tiles + j, 0))],
        out_specs=[pl.BlockSpec((tm, tn), lambda i, j: (i, j)),
                   pl.BlockSpec((slab, d), lambda i, j: (i * n_tiles + j, 0))],
        out_shape=[jax.ShapeDtypeStruct((t, d_ff), BF16),
                   jax.ShapeDtypeStruct((d_ff, d), BF16)],
        compiler_params=_params("parallel", "arbitrary"),
        name="ffn_up",
    )(hn, hn, w_up, w_up, cw, cw, cb, cb, w_down)


def _dilated_pool_layer(h, hn, w_in, q_gain, k_gain, pool_w, pool_scale, w_out, j, bsz, seq):
    d = h.shape[1]
    a_width = d // 2
    heads = a_width // HEAD_DIM
    tn = 512
    q_blocks = a_width // tn
    k_base = N_BRANCH * q_blocks
    v_cols = 2 * N_BRANCH * a_width
    gains = jnp.concatenate([jnp.tile(q_gain, heads), jnp.tile(k_gain, heads)]).reshape(1, 2 * a_width)
    dils = tuple(dil for _, dil in A_BRANCHES)
    v_list = _mm([hn], w_in, j, lambda c: v_cols // tn + c, a_width, BF16, dils=dils, seq=seq, tn=tn,
                 name="ab_in_v")
    u = _mm([hn], w_in, j, lambda c: (v_cols + a_width) // tn + c, d - a_width, F32, tn=tn, name="ab_in_u")
    slopes = _alibi_slopes(heads)
    outs, lses = [], []
    for g, dil in enumerate(dils):
        def wcol(c, g=g):
            return jnp.where(c < q_blocks, g * q_blocks + c, k_base + g * q_blocks + c - q_blocks)
        qk = _mm([hn], w_in, j, wcol, 2 * a_width, BF16, mode="headnorm", extra=gains, dils=(dil,), seq=seq,
                 tn=tn, name=f"ab_in_qk{g}")
        if dil == 1:
            qk = qk.reshape(bsz, 1, seq, 2 * a_width)
            v_g = v_list[g].reshape(bsz, 1, seq, a_width)
        else:
            v_g = v_list[g]
        o, lse = _band_attention(qk, v_g, bsz, seq, g, slopes[g])
        outs.append(o)
        lses.append(lse)
    a_out = _combine_branches(outs, lses)
    b_out = _pool_mixer(u, pool_w, pool_scale, seq)
    return _mm([a_out, b_out], w_out, j, lambda c: c, d, F32, mode="residual", extra=h, name="ab_out")


def _gla_layer(h, norm_gain, w_in, w_a2, b_a, o_gain, w_out, j, bsz, seq):
    d = h.shape[1]
    kw = w_a2.shape[2]
    vw = w_out.shape[1]
    tn = 512
    w_in_t = jnp.swapaxes(w_in, 1, 2)
    rank = w_a2.shape[1]
    w_r = jnp.pad(w_in_t[j, 2 * kw + 2 * vw:, :], ((0, LANES - rank), (0, 0)))
    w_a2p = jnp.pad(w_a2[j], ((0, LANES - rank), (0, 0)))
    hn, log_a = _gla_norm_log_decay(h, norm_gain, w_r, w_a2p, b_a[j])
    qk = _mm([hn], w_in_t, j, lambda c: c, 2 * kw, F32, w_t=True, tn=tn, name="c_in_qk")
    v = _mm([hn], w_in_t, j, lambda c: 2 * kw // tn + c, vw, BF16, w_t=True, tn=tn, name="c_in_v")
    gate = _mm([hn], w_in_t, j, lambda c: (2 * kw + vw) // tn + c, vw, F32, w_t=True, tn=tn, name="c_in_gate")
    o = _gla(qk, v, gate, log_a, o_gain[j], bsz, seq)
    return _mm([o], w_out, j, lambda c: c, d, F32, mode="residual", extra=h, name="c_out")


def _memory_kv(mem, gain, wkv, k_gain, layer, bsz):
    xw = wkv.shape[2] // 2
    tn = 256
    mem_n = _rmsnorm(mem, gain)
    rows = mem.shape[0]
    gains = jnp.tile(k_gain, xw // HEAD_DIM).reshape(1, xw)
    k = _mm([mem_n], wkv, layer, lambda c: c, xw, BF16, mode="headnorm", extra=gains, tm=rows, tn=tn, name="mem_k")
    v = _mm([mem_n], wkv, layer, lambda c: xw // tn + c, xw, BF16, tm=rows, tn=tn, name="mem_v")
    return k.reshape(bsz, rows // bsz, xw), v.reshape(bsz, rows // bsz, xw)


def kernel(x, mem, mix_norm, ab_w_in, ab_q_norm, ab_k_norm, ab_pool_w, ab_pool_scale, ab_w_out, c_w_in, c_w_a2, c_b_a, c_o_norm, c_w_out, x_norm, x_mem_norm, x_wq, x_wkv, x_q_norm, x_k_norm, x_wo, f_norm, f_w_up, f_conv_w, f_conv_b, f_w_down):
    bsz, seq, d = x.shape
    depth = mix_norm.shape[0]
    h = x.reshape(bsz * seq, d)
    mem2 = mem.reshape(bsz * mem.shape[1], d)
    for layer in range(depth):
        j = layer // 2
        if layer % 2 == 0:
            hn = _rmsnorm(h, mix_norm[layer])
            h = _dilated_pool_layer(h, hn, ab_w_in, ab_q_norm[j], ab_k_norm[j], ab_pool_w[j], ab_pool_scale[j],
                                    ab_w_out, j, bsz, seq)
        else:
            h = _gla_layer(h, mix_norm[layer], c_w_in, c_w_a2, c_b_a, c_o_norm, c_w_out, j, bsz, seq)
        k, v = _memory_kv(mem2, x_mem_norm[layer], x_wkv, x_k_norm[layer], layer, bsz)
        h, hn = _cross_attention(h, x_norm[layer], x_wq[layer].astype(BF16), x_q_norm[layer], k, v,
                                 x_wo[layer].astype(BF16), f_norm[layer], seq)
        act, w_down = _ffn_up(hn, f_w_up, f_conv_w, f_conv_b, f_w_down, layer, seq)
        h = _mm([act], w_down, None, lambda c: c, d, F32, mode="residual", extra=h, tm=512, name="ffn_down")
    return h.reshape(bsz, seq, d)
```

```python
import functools

import numpy as np
import jax
import jax.numpy as jnp
from jax import lax
from jax.experimental import pallas as pl
from jax.experimental.pallas import tpu as pltpu

F32 = jnp.float32
BF16 = jnp.bfloat16

LANES = 128
SUBLANES = 8
VMEM_LIMIT_BYTES = 56 * 2 ** 20

EPS = 1e-6
HEAD_DIM = 128
A_BRANCHES = ((128, 1), (512, 4), (2048, 16))
N_BRANCH = len(A_BRANCHES)
BAND_BLOCK = 128
POOL_WINDOWS = (2, 4, 8, 16)
POOL_HALO = 16
C_HEADS = 8
C_GATE_RANK = 16
C_GATE_TAU = 16.0
C_CHUNK = 64
X_HEADS = 4
CONV_WIDTH = 3
CONV_HALO = SUBLANES
MM_ROWS = 1024
FAST_ROW_STRIDE = 4
ATTN_STAGE_HEADS = {1: 8, 4: 16, 16: 1}
ATTN_PAIRS_PER_STEP = 64
FFN_ROWS = 2048


def _params(*semantics):
    return pltpu.CompilerParams(dimension_semantics=semantics, vmem_limit_bytes=VMEM_LIMIT_BYTES)


def _lane_groups(width):
    return [slice(c * LANES, (c + 1) * LANES) for c in range(width // LANES)]


def _rmsnorm_kernel(x_ref, g_ref, o_ref):
    x = x_ref[...].astype(F32)
    ms = jnp.mean(x * x, axis=-1, keepdims=True)
    o_ref[...] = (x * lax.rsqrt(ms + EPS) * g_ref[...]).astype(o_ref.dtype)


def _rmsnorm(x, gain, tm=512):
    t, d = x.shape
    return pl.pallas_call(
        _rmsnorm_kernel,
        grid=(t // tm,),
        in_specs=[pl.BlockSpec((tm, d), lambda i: (i, 0)),
                  pl.BlockSpec((1, d), lambda i: (0, 0))],
        out_specs=pl.BlockSpec((tm, d), lambda i: (i, 0)),
        out_shape=jax.ShapeDtypeStruct((t, d), BF16),
        compiler_params=_params("parallel"),
        name="rmsnorm",
    )(x, gain.reshape(1, d))


def _mm_kernel(*refs, n_a, mode, dils, w_t):
    a_refs = refs[:n_a]
    w_ref = refs[n_a]
    n_extra = 1 if mode in ("headnorm", "residual") else 0
    extra = refs[n_a + 1:n_a + 1 + n_extra]
    o_refs = refs[n_a + 1 + n_extra:n_a + 1 + n_extra + len(dils)]
    scratch = refs[n_a + 1 + n_extra + len(dils):]
    acc = None
    k0 = 0
    for a_ref in a_refs:
        kk = a_ref.shape[1]
        if w_t:
            part = lax.dot_general(a_ref[...], w_ref[:, k0:k0 + kk].astype(BF16), (((1,), (1,)), ((), ())),
                                   preferred_element_type=F32)
        else:
            part = jnp.dot(a_ref[...], w_ref[k0:k0 + kk, :].astype(BF16), preferred_element_type=F32)
        acc = part if acc is None else acc + part
        k0 += kk
    tm, tn = acc.shape
    for c, sl in enumerate(_lane_groups(tn)):
        blk = acc[:, sl]
        if mode == "headnorm":
            ms = jnp.mean(blk * blk, axis=-1, keepdims=True)
            blk = blk * lax.rsqrt(ms + EPS) * extra[0][:, sl]
        elif mode == "residual":
            blk = extra[0][:, sl] + blk
        if scratch:
            scratch[0][c] = blk
        for o_ref, dil in zip(o_refs, dils):
            if dil == 1:
                o_ref[:, sl] = blk.astype(o_ref.dtype)
    for o_ref, dil in zip(o_refs, dils):
        if 1 < dil <= FAST_ROW_STRIDE:
            for r in range(dil):
                for c, sl in enumerate(_lane_groups(tn)):
                    o_ref[r, :, sl] = scratch[0][c, pl.ds(r, tm // dil, stride=dil), :].astype(o_ref.dtype)
        elif dil > FAST_ROW_STRIDE:
            s1, s2 = FAST_ROW_STRIDE, dil // FAST_ROW_STRIDE
            part = tm // s1
            for q in range(s1):
                for c in range(tn // LANES):
                    scratch[1][c, q * part:(q + 1) * part, :] = scratch[0][c, pl.ds(q, part, stride=s1), :]
            for q in range(s1):
                for p in range(s2):
                    for c, sl in enumerate(_lane_groups(tn)):
                        o_ref[s1 * p + q, :, sl] = scratch[1][c, pl.ds(q * part + p, tm // dil, stride=s2),
                                                              :].astype(o_ref.dtype)


def _mm(a_list, w, layer, wcol, ncols, out_dtype, *, mode="plain", extra=None, dils=(1,), seq=None,
        w_t=False, tm=MM_ROWS, tn=512, name="mm"):
    t = a_list[0].shape[0]
    k_total = sum(a.shape[1] for a in a_list)
    assert w.shape[-1 if w_t else -2] == k_total and ncols % tn == 0 and t % tm == 0
    in_specs = [pl.BlockSpec((tm, a.shape[1]), lambda i, j: (i, 0)) for a in a_list]
    if w_t:
        in_specs.append(pl.BlockSpec((None, tn, k_total), lambda i, j: (layer, wcol(j), 0)))
    elif w.ndim == 3:
        in_specs.append(pl.BlockSpec((None, k_total, tn), lambda i, j: (layer, 0, wcol(j))))
    else:
        in_specs.append(pl.BlockSpec((k_total, tn), lambda i, j: (0, wcol(j))))
    args = list(a_list) + [w]
    if mode == "headnorm":
        in_specs.append(pl.BlockSpec((1, tn), lambda i, j: (0, j)))
        args.append(extra)
    elif mode == "residual":
        in_specs.append(pl.BlockSpec((tm, tn), lambda i, j: (i, j)))
        args.append(extra)
    out_specs, out_shapes = [], []
    for dil in dils:
        if dil == 1:
            out_specs.append(pl.BlockSpec((tm, tn), lambda i, j: (i, j)))
            out_shapes.append(jax.ShapeDtypeStruct((t, ncols), out_dtype))
        else:
            tiles = seq // tm
            assert seq % tm == 0 and tm % dil == 0
            out_specs.append(pl.BlockSpec((None, dil, tm // dil, tn), lambda i, j: (i // tiles, 0, i % tiles, j)))
            out_shapes.append(jax.ShapeDtypeStruct((t // seq, dil, seq // dil, ncols), out_dtype))
    assert all(dil <= FAST_ROW_STRIDE ** 2 and (dil <= FAST_ROW_STRIDE or dil % FAST_ROW_STRIDE == 0) for dil in dils)
    slab = pltpu.VMEM((tn // LANES, tm, LANES), F32)
    n_slabs = 2 if max(dils) > FAST_ROW_STRIDE else 1 if max(dils) > 1 else 0
    scratch = [slab] * n_slabs
    outs = pl.pallas_call(
        functools.partial(_mm_kernel, n_a=len(a_list), mode=mode, dils=tuple(dils), w_t=w_t),
        grid=(t // tm, ncols // tn),
        in_specs=in_specs,
        out_specs=out_specs,
        out_shape=out_shapes,
        scratch_shapes=scratch,
        compiler_params=_params("parallel", "arbitrary"),
        name=name,
    )(*args)
    return outs[0] if len(dils) == 1 else outs


def _alibi_slopes(heads):
    n = N_BRANCH * heads
    s = np.power(np.float32(2.0), -8.0 * np.arange(1, n + 1, dtype=np.float32) / np.float32(n)).astype(np.float32)
    return s.reshape(N_BRANCH, -1)


def _band_attn_kernel(q_ref, kp_ref, kc_ref, vp_ref, vc_ref, o_ref, lse_ref, *scratch, dilation, slopes,
                      stage_heads, sub):
    blk = BAND_BLOCK
    hg = pl.program_id(2)
    has_prev = pl.program_id(1) > 0
    n_heads = len(slopes[0])
    qi = lax.broadcasted_iota(jnp.int32, (blk, 2 * blk), 0)
    ki = lax.broadcasted_iota(jnp.int32, (blk, 2 * blk), 1)
    rel = qi + blk - ki
    in_band = jnp.logical_and(rel >= 0, rel <= blk)
    valid_first = jnp.logical_and(in_band, jnp.logical_or(ki >= blk, has_prev))
    dist = (rel * dilation).astype(F32)
    lane = lax.broadcasted_iota(jnp.int32, (blk, LANES), 1)
    scale = HEAD_DIM ** -0.5
    contract_last = (((1,), (1,)), ((), ()))
    head_cols = [slice(h * HEAD_DIM, (h + 1) * HEAD_DIM) for h in range(n_heads)]
    two_pass = dilation > FAST_ROW_STRIDE
    s1, s2 = FAST_ROW_STRIDE, dilation // FAST_ROW_STRIDE
    part = blk * dilation // s1
    head_slope = []
    for h in range(n_heads):
        slope = slopes[0][h]
        for g in range(1, len(slopes)):
            slope = jnp.where(hg == g, slopes[g][h], slope)
        head_slope.append(slope)
    lse_tiles = []
    for r, m in [(r, m) for r in range(dilation) for m in range(sub)]:
        rows = slice(m * blk, (m + 1) * blk)
        prev_rows = slice((m - 1) * blk, m * blk)
        valid = valid_first if m == 0 else in_band

        def keys(prev_ref, cur_ref, cols):
            prev = prev_ref[r, :, cols] if m == 0 else cur_ref[r, prev_rows, cols]
            return jnp.concatenate([prev, cur_ref[r, rows, cols]], axis=0)

        lse_tile = jnp.zeros((blk, LANES), F32)
        for h0 in range(0, n_heads, stage_heads):
            group = list(range(h0, min(h0 + stage_heads, n_heads)))
            scores = {h: lax.dot_general(q_ref[r, rows, head_cols[h]], keys(kp_ref, kc_ref, head_cols[h]),
                                         contract_last, preferred_element_type=F32) for h in group}
            probs, dens, maxes = {}, {}, {}
            for h in group:
                s = jnp.where(valid, scores[h] * scale - head_slope[h] * dist, -jnp.inf)
                maxes[h] = jnp.max(s, axis=-1, keepdims=True)
                p = jnp.exp(s - maxes[h])
                dens[h] = jnp.sum(p, axis=-1, keepdims=True)
                probs[h] = p.astype(BF16)
            outs = {h: jnp.dot(probs[h], keys(vp_ref, vc_ref, head_cols[h]), preferred_element_type=F32)
                    for h in group}
            for h in group:
                o = outs[h] / dens[h]
                lse_h = maxes[h] + jnp.log(dens[h])
                for g in range(len(slopes)):
                    lse_tile = jnp.where(jnp.logical_and(lane == g * n_heads + h, hg == g), lse_h, lse_tile)
                if dilation == 1:
                    o_ref[rows, head_cols[h]] = o.astype(o_ref.dtype)
                elif two_pass:
                    scratch[0][h, pl.ds((r % s1) * part + r // s1, blk, stride=s2), :] = o
                else:
                    scratch[0][h, pl.ds(r, blk, stride=dilation), :] = o
        if dilation == 1:
            lse_tiles.append(lse_tile)
        elif two_pass:
            scratch[1][pl.ds((r % s1) * part + r // s1, blk, stride=s2), :] = lse_tile
        else:
            scratch[1][pl.ds(r, blk, stride=dilation), :] = lse_tile
    if dilation == 1:
        lse_slab = jnp.concatenate(lse_tiles, axis=0)
    elif two_pass:
        for q in range(s1):
            for h in range(n_heads):
                scratch[2][h, pl.ds(q, part, stride=s1), :] = scratch[0][h, q * part:(q + 1) * part, :]
            scratch[3][pl.ds(q, part, stride=s1), :] = scratch[1][q * part:(q + 1) * part, :]
        for h in range(n_heads):
            o_ref[:, h * HEAD_DIM:(h + 1) * HEAD_DIM] = scratch[2][h].astype(o_ref.dtype)
        lse_slab = scratch[3][...]
    else:
        for h in range(n_heads):
            o_ref[:, h * HEAD_DIM:(h + 1) * HEAD_DIM] = scratch[0][h].astype(o_ref.dtype)
        lse_slab = scratch[1][...]

    @pl.when(hg == 0)
    def _():
        lse_ref[...] = lse_slab

    @pl.when(hg > 0)
    def _():
        lse_ref[...] += lse_slab


def _band_attention(qk, v, bsz, seq, branch, slopes):
    _, dilation = A_BRANCHES[branch]
    width = v.shape[-1]
    n_blk = seq // dilation // BAND_BLOCK
    heads = max(min(width // HEAD_DIM, ATTN_PAIRS_PER_STEP // dilation), 1)
    hw = heads * HEAD_DIM
    n_hg = width // hw
    sub = ATTN_PAIRS_PER_STEP // heads if dilation == 1 else 1
    assert n_blk % sub == 0
    n_steps = n_blk // sub
    slope_tab = tuple(tuple(float(s) for s in slopes[g * heads:(g + 1) * heads]) for g in range(n_hg))
    rows = BAND_BLOCK * dilation * sub
    blk = (None, dilation, sub * BAND_BLOCK, hw)
    prev_blk = (None, dilation, BAND_BLOCK, hw)

    def prev(n):
        return jnp.maximum(n * sub - 1, 0)

    scratch = []
    if dilation > 1:
        scratch = [pltpu.VMEM((heads, rows, HEAD_DIM), F32), pltpu.VMEM((rows, LANES), F32)]
        if dilation > FAST_ROW_STRIDE:
            scratch = scratch * 2
    return pl.pallas_call(
        functools.partial(_band_attn_kernel, dilation=dilation, slopes=slope_tab,
                          stage_heads=ATTN_STAGE_HEADS[dilation], sub=sub),
        grid=(bsz, n_steps, n_hg),
        in_specs=[pl.BlockSpec(blk, lambda b, n, g: (b, 0, n, g)),
                  pl.BlockSpec(prev_blk, lambda b, n, g: (b, 0, prev(n), n_hg + g)),
                  pl.BlockSpec(blk, lambda b, n, g: (b, 0, n, n_hg + g)),
                  pl.BlockSpec(prev_blk, lambda b, n, g: (b, 0, prev(n), g)),
                  pl.BlockSpec(blk, lambda b, n, g: (b, 0, n, g))],
        out_specs=[pl.BlockSpec((rows, hw), lambda b, n, g: (b * n_steps + n, g)),
                   pl.BlockSpec((rows, LANES), lambda b, n, g: (b * n_steps + n, 0))],
        out_shape=[jax.ShapeDtypeStruct((bsz * seq, width), BF16),
                   jax.ShapeDtypeStruct((bsz * seq, LANES), F32)],
        scratch_shapes=scratch,
        compiler_params=_params("parallel", "arbitrary", "arbitrary"),
        name=f"band_attn_d{dilation}",
    )(qk, qk, qk, v, v)


def _combine_kernel(o0_ref, o1_ref, o2_ref, l0_ref, l1_ref, l2_ref, out_ref):
    l0, l1, l2 = l0_ref[...], l1_ref[...], l2_ref[...]
    m = jnp.maximum(jnp.maximum(l0, l1), l2)
    e0, e1, e2 = jnp.exp(l0 - m), jnp.exp(l1 - m), jnp.exp(l2 - m)
    tot = e0 + e1 + e2
    w1, w2 = e1 / tot, e2 / tot
    for h in range(out_ref.shape[1] // HEAD_DIM):
        sl = slice(h * HEAD_DIM, (h + 1) * HEAD_DIM)
        o0 = o0_ref[:, sl].astype(F32)
        acc = o0 + w1[:, h:h + 1] * (o1_ref[:, sl].astype(F32) - o0) + w2[:, h:h + 1] * (o2_ref[:, sl].astype(F32) - o0)
        out_ref[:, sl] = acc.astype(out_ref.dtype)


def _combine_branches(outs, lses, tm=512):
    t, width = outs[0].shape
    o_spec = pl.BlockSpec((tm, width), lambda i: (i, 0))
    l_spec = pl.BlockSpec((tm, LANES), lambda i: (i, 0))
    return pl.pallas_call(
        _combine_kernel,
        grid=(t // tm,),
        in_specs=[o_spec] * 3 + [l_spec] * 3,
        out_specs=o_spec,
        out_shape=jax.ShapeDtypeStruct((t, width), BF16),
        compiler_params=_params("parallel"),
        name="combine_branches",
    )(*outs, *lses)


def _pool_kernel(u_ref, uh_ref, w_ref, sc_ref, o_ref, wb_ref, *, blocks_per_seq):
    i = pl.program_id(0)

    @pl.when(i == 0)
    def _():
        wb_ref[...] = w_ref[...].astype(BF16)

    tm = u_ref.shape[0]
    group = w_ref.shape[1]
    first = (i % blocks_per_seq) == 0
    row = lax.broadcasted_iota(jnp.int32, (tm, 1), 0)
    pos = (i % blocks_per_seq) * tm + row
    for g, win in enumerate(POOL_WINDOWS):
        sl = slice(g * group, (g + 1) * group)
        u = u_ref[:, sl]
        halo = jnp.where(first, 0.0, uh_ref[:, sl])
        s = jnp.concatenate([halo, u], axis=0)
        step = 1
        while step < win:
            s = s + pltpu.roll(s, step, axis=0)
            step *= 2
        count = jnp.minimum(pos + 1, win).astype(F32)
        pooled = s[POOL_HALO:] / count - u
        y = jnp.dot(pooled.astype(BF16), wb_ref[g], preferred_element_type=F32)
        o_ref[:, sl] = (y * sc_ref[:, sl]).astype(o_ref.dtype)


def _pool_mixer(u, pool_w, pool_scale, seq, tm=1024):
    t, width = u.shape
    n_group, group, _ = pool_w.shape
    blocks_per_seq = seq // tm
    halo_blocks = tm // POOL_HALO
    return pl.pallas_call(
        functools.partial(_pool_kernel, blocks_per_seq=blocks_per_seq),
        grid=(t // tm,),
        in_specs=[pl.BlockSpec((tm, width), lambda i: (i, 0)),
                  pl.BlockSpec((POOL_HALO, width), lambda i: (jnp.maximum(i * halo_blocks - 1, 0), 0)),
                  pl.BlockSpec((n_group, group, group), lambda i: (0, 0, 0)),
                  pl.BlockSpec((1, width), lambda i: (0, 0))],
        out_specs=pl.BlockSpec((tm, width), lambda i: (i, 0)),
        out_shape=jax.ShapeDtypeStruct((t, width), BF16),
        scratch_shapes=[pltpu.VMEM((n_group, group, group), BF16)],
        compiler_params=_params("arbitrary"),
        name="pool_mixer",
    )(u, u, pool_w, pool_scale.reshape(1, width))


def _split3(x):
    hi = x.astype(BF16)
    r1 = x - hi.astype(F32)
    mid = r1.astype(BF16)
    lo = (r1 - mid.astype(F32)).astype(BF16)
    return hi, mid, lo


def _gate_kernel(h_ref, ng_ref, wr_ref, wa2_ref, ba_ref, hn_ref, o_ref):
    x = h_ref[...]
    ms = jnp.mean(x * x, axis=-1, keepdims=True)
    hn = (x * lax.rsqrt(ms + EPS) * ng_ref[...]).astype(hn_ref.dtype)
    hn_ref[...] = hn
    r = lax.dot_general(hn, wr_ref[...].astype(BF16), (((1,), (1,)), ((), ())), preferred_element_type=F32)
    r_hi, r_mid, _ = _split3(r)
    w_hi, w_mid, _ = _split3(wa2_ref[...])
    g = (jnp.dot(r_hi, w_hi, preferred_element_type=F32)
         + (jnp.dot(r_hi, w_mid, preferred_element_type=F32) + jnp.dot(r_mid, w_hi, preferred_element_type=F32)))
    g = g + ba_ref[...]
    log_sig = jnp.minimum(g, 0.0) - jnp.log1p(jnp.exp(-jnp.abs(g)))
    o_ref[...] = log_sig / C_GATE_TAU


def _gla_norm_log_decay(h, norm_gain, w_r, w_a2, b_a, tm=512):
    t, d = h.shape
    kw = w_a2.shape[1]
    return pl.pallas_call(
        _gate_kernel,
        grid=(t // tm,),
        in_specs=[pl.BlockSpec((tm, d), lambda i: (i, 0)),
                  pl.BlockSpec((1, d), lambda i: (0, 0)),
                  pl.BlockSpec((LANES, d), lambda i: (0, 0)),
                  pl.BlockSpec((LANES, kw), lambda i: (0, 0)),
                  pl.BlockSpec((1, kw), lambda i: (0, 0))],
        out_specs=[pl.BlockSpec((tm, d), lambda i: (i, 0)),
                   pl.BlockSpec((tm, kw), lambda i: (i, 0))],
        out_shape=[jax.ShapeDtypeStruct((t, d), BF16),
                   jax.ShapeDtypeStruct((t, kw), F32)],
        compiler_params=_params("parallel"),
        name="gla_norm_log_decay",
    )(h, norm_gain.reshape(1, d), w_r, w_a2, b_a.reshape(1, kw))


def _gla_kernel(q_ref, k_ref, v_ref, la_ref, gate_ref, og_ref, o_ref, state_ref, *, chunks, heads):
    @pl.when(pl.program_id(2) == 0)
    def _():
        state_ref[...] = jnp.zeros_like(state_ref)

    c = C_CHUNK
    dk = q_ref.shape[1] // heads
    dv = v_ref.shape[1] // heads
    ri = lax.broadcasted_iota(jnp.int32, (c, c), 0)
    ci = lax.broadcasted_iota(jnp.int32, (c, c), 1)
    causal = ci <= ri
    tri = causal.astype(BF16)
    contract_last = (((1,), (1,)), ((), ()))
    contract_first = (((0,), (0,)), ((), ()))
    pairs = [(n, h) for n in range(chunks) for h in range(heads)]

    def rows(n):
        return slice(n * c, (n + 1) * c)

    def kcols(h):
        return slice(h * dk, (h + 1) * dk)

    def vcols(h):
        return slice(h * dv, (h + 1) * dv)

    bc = {}
    for n, h in pairs:
        la_hi, la_mid, la_lo = _split3(la_ref[rows(n), kcols(h)])
        bc[n, h] = (jnp.dot(tri, la_hi, preferred_element_type=F32)
                    + jnp.dot(tri, la_mid, preferred_element_type=F32)
                    + jnp.dot(tri, la_lo, preferred_element_type=F32))
    q_b, k_in, k_st, decay = {}, {}, {}, {}
    for n, h in pairs:
        b = bc[n, h]
        b_last = b[c - 1:c, :]
        q_b[n, h] = (q_ref[rows(n), kcols(h)] * (dk ** -0.5) * jnp.exp(b)).astype(BF16)
        k = k_ref[rows(n), kcols(h)]
        k_in[n, h] = (k * jnp.exp(-b)).astype(BF16)
        k_st[n, h] = (k * jnp.exp(b_last - b)).astype(BF16)
        decay[n, h] = jnp.exp(b_last)
    att = {}
    for p in pairs:
        a = lax.dot_general(q_b[p], k_in[p], contract_last, preferred_element_type=F32)
        att[p] = jnp.where(causal, a, 0.0).astype(BF16)
    o_intra, upd = {}, {}
    for n, h in pairs:
        v = v_ref[rows(n), vcols(h)]
        o_intra[n, h] = jnp.dot(att[n, h], v, preferred_element_type=F32)
        upd[n, h] = lax.dot_general(v, k_st[n, h], contract_first, preferred_element_type=F32)
    for h in range(heads):
        state = state_ref[h]
        for n in range(chunks):
            o = o_intra[n, h] + lax.dot_general(q_b[n, h], state.astype(BF16), contract_last,
                                                preferred_element_type=F32)
            state = state * decay[n, h] + upd[n, h]
            ms = jnp.mean(o * o, axis=-1, keepdims=True)
            gate = gate_ref[rows(n), vcols(h)]
            y = o * lax.rsqrt(ms + EPS) * og_ref[...] * (gate * jax.nn.sigmoid(gate))
            o_ref[rows(n), vcols(h)] = y.astype(o_ref.dtype)
        state_ref[h] = state


def _gla(qk, v, gate, log_a, o_gain, bsz, seq, rows=512, heads=2):
    t = bsz * seq
    dk = log_a.shape[1] // C_HEADS
    dv = v.shape[1] // C_HEADS
    steps = seq // rows
    groups = C_HEADS // heads

    def row_block(b, n):
        return b * steps + n

    return pl.pallas_call(
        functools.partial(_gla_kernel, chunks=rows // C_CHUNK, heads=heads),
        grid=(bsz, groups, steps),
        in_specs=[pl.BlockSpec((rows, heads * dk), lambda b, g, n: (row_block(b, n), g)),
                  pl.BlockSpec((rows, heads * dk), lambda b, g, n: (row_block(b, n), groups + g)),
                  pl.BlockSpec((rows, heads * dv), lambda b, g, n: (row_block(b, n), g)),
                  pl.BlockSpec((rows, heads * dk), lambda b, g, n: (row_block(b, n), g)),
                  pl.BlockSpec((rows, heads * dv), lambda b, g, n: (row_block(b, n), g)),
                  pl.BlockSpec((1, dv), lambda b, g, n: (0, 0))],
        out_specs=pl.BlockSpec((rows, heads * dv), lambda b, g, n: (row_block(b, n), g)),
        out_shape=jax.ShapeDtypeStruct((t, v.shape[1]), BF16),
        scratch_shapes=[pltpu.VMEM((heads, dv, dk), F32)],
        compiler_params=_params("parallel", "parallel", "arbitrary"),
        name="gla",
    )(qk, qk, v, log_a, gate, o_gain.reshape(1, dv))


def _xattn_kernel(h_ref, g_ref, wq_ref, qg_ref, k_ref, v_ref, wo_ref, fg_ref, o_ref, on_ref):
    x = h_ref[...]
    ms = jnp.mean(x * x, axis=-1, keepdims=True)
    hn = (x * lax.rsqrt(ms + EPS) * g_ref[...]).astype(BF16)
    q = jnp.dot(hn, wq_ref[...], preferred_element_type=F32)
    scale = HEAD_DIM ** -0.5
    contract_last = (((1,), (1,)), ((), ()))
    cols = [slice(hd * HEAD_DIM, (hd + 1) * HEAD_DIM) for hd in range(X_HEADS)]
    q_heads = []
    for sl in cols:
        qh = q[:, sl]
        qms = jnp.mean(qh * qh, axis=-1, keepdims=True)
        q_heads.append((qh * lax.rsqrt(qms + EPS) * qg_ref[...]).astype(BF16))
    scores = [lax.dot_general(qh, k_ref[0, :, sl], contract_last, preferred_element_type=F32) * scale
              for qh, sl in zip(q_heads, cols)]
    probs = []
    for s in scores:
        m = jnp.max(s, axis=-1, keepdims=True)
        p = jnp.exp(s - m)
        probs.append((p / jnp.sum(p, axis=-1, keepdims=True)).astype(BF16))
    heads = [jnp.dot(p, v_ref[0, :, sl], preferred_element_type=F32) for p, sl in zip(probs, cols)]
    o = jnp.concatenate(heads, axis=-1).astype(BF16)
    y = x + jnp.dot(o, wo_ref[...], preferred_element_type=F32)
    o_ref[...] = y
    yms = jnp.mean(y * y, axis=-1, keepdims=True)
    on_ref[...] = (y * lax.rsqrt(yms + EPS) * fg_ref[...]).astype(on_ref.dtype)


def _cross_attention(h, norm_gain, wq, q_gain, k, v, wo, next_gain, seq, tm=512):
    t, d = h.shape
    xw = wq.shape[1]
    mlen = k.shape[1]
    tiles_per_seq = seq // tm
    return pl.pallas_call(
        _xattn_kernel,
        grid=(t // tm,),
        in_specs=[pl.BlockSpec((tm, d), lambda i: (i, 0)),
                  pl.BlockSpec((1, d), lambda i: (0, 0)),
                  pl.BlockSpec((d, xw), lambda i: (0, 0)),
                  pl.BlockSpec((1, HEAD_DIM), lambda i: (0, 0)),
                  pl.BlockSpec((1, mlen, xw), lambda i: (i // tiles_per_seq, 0, 0)),
                  pl.BlockSpec((1, mlen, xw), lambda i: (i // tiles_per_seq, 0, 0)),
                  pl.BlockSpec((xw, d), lambda i: (0, 0)),
                  pl.BlockSpec((1, d), lambda i: (0, 0))],
        out_specs=[pl.BlockSpec((tm, d), lambda i: (i, 0)),
                   pl.BlockSpec((tm, d), lambda i: (i, 0))],
        out_shape=[jax.ShapeDtypeStruct((t, d), F32),
                   jax.ShapeDtypeStruct((t, d), BF16)],
        compiler_params=_params("parallel"),
        name="cross_attention",
    )(h, norm_gain.reshape(1, d), wq, q_gain.reshape(1, HEAD_DIM), k, v, wo, next_gain.reshape(1, d))


def _ffn_up_kernel(a_ref, ah_ref, wg_ref, wv_ref, cwg_ref, cwv_ref, cbg_ref, cbv_ref, wd_ref, o_ref, wdo_ref, *,
                   blocks_per_seq):
    wdo_ref[...] = wd_ref[...].astype(wdo_ref.dtype)
    tm = a_ref.shape[0]
    first = (pl.program_id(0) % blocks_per_seq) == 0
    row = lax.broadcasted_iota(jnp.int32, (tm, 1), 0)
    a = a_ref[...]
    ah = ah_ref[...]

    def conv_half(w_ref, cw_ref, cb_ref):
        w = w_ref[...].astype(BF16)
        u = jnp.dot(a, w, preferred_element_type=F32)
        uh = jnp.where(first, 0.0, jnp.dot(ah, w, preferred_element_type=F32))
        u1 = jnp.where(row == 0, uh[CONV_HALO - 1:CONV_HALO], pltpu.roll(u, 1, axis=0))
        u2 = jnp.where(row == 0, uh[CONV_HALO - 2:CONV_HALO - 1],
                       jnp.where(row == 1, uh[CONV_HALO - 1:CONV_HALO], pltpu.roll(u, 2, axis=0)))
        cw = cw_ref[...]
        return cb_ref[...] + cw[0:1] * u2 + cw[1:2] * u1 + cw[2:3] * u

    cg = conv_half(wg_ref, cwg_ref, cbg_ref)
    cv = conv_half(wv_ref, cwv_ref, cbv_ref)
    o_ref[...] = (cg * jax.nn.sigmoid(cg) * cv).astype(o_ref.dtype)


def _ffn_up(hn, w_up, conv_w, conv_b, w_down, layer, seq, tm=FFN_ROWS, tn=256):
    t, d = hn.shape
    d_ff = w_up.shape[2] // 2
    n_tiles = d_ff // tn
    n_steps = (t // tm) * n_tiles
    slab = d_ff // n_steps
    assert d_ff % n_steps == 0 and slab % (2 * SUBLANES) == 0
    blocks_per_seq = seq // tm
    halo_blocks = tm // CONV_HALO
    cw = conv_w[layer]
    cb = conv_b[layer].reshape(1, 2 * d_ff)
    return pl.pallas_call(
        functools.partial(_ffn_up_kernel, blocks_per_seq=blocks_per_seq),
        grid=(t // tm, n_tiles),
        in_specs=[pl.BlockSpec((tm, d), lambda i, j: (i, 0), pipeline_mode=pl.Buffered(1)),
                  pl.BlockSpec((CONV_HALO, d), lambda i, j: (jnp.maximum(i * halo_blocks - 1, 0), 0)),
                  pl.BlockSpec((None, d, tn), lambda i, j: (layer, 0, j)),
                  pl.BlockSpec((None, d, tn), lambda i, j: (layer, 0, n_tiles + j)),
                  pl.BlockSpec((CONV_WIDTH, tn), lambda i, j: (0, j)),
                  pl.BlockSpec((CONV_WIDTH, tn), lambda i, j: (0, n_tiles + j)),
                  pl.BlockSpec((1, tn), lambda i, j: (0, j)),
                  pl.BlockSpec((1, tn), lambda i, j: (0, n_tiles + j)),
                  pl.BlockSpec((None, slab, d), lambda i, j: (layer, i * n_tiles + j, 0))],
        out_specs=[pl.BlockSpec((tm, tn), lambda i, j: (i, j)),
                   pl.BlockSpec((slab, d), lambda i, j: (i * n_tiles + j, 0))],
        out_shape=[jax.ShapeDtypeStruct((t, d_ff), BF16),
                   jax.ShapeDtypeStruct((d_ff, d), BF16)],
        compiler_params=_params("parallel", "arbitrary"),
        name="ffn_up",
    )(hn, hn, w_up, w_up, cw, cw, cb, cb, w_down)


def _dilated_pool_layer(h, hn, w_in, q_gain, k_gain, pool_w, pool_scale, w_out, j, bsz, seq):
    d = h.shape[1]
    a_width = d // 2
    heads = a_width // HEAD_DIM
    tn = 512
    q_blocks = a_width // tn
    k_base = N_BRANCH * q_blocks
    v_cols = 2 * N_BRANCH * a_width
    gains = jnp.concatenate([jnp.tile(q_gain, heads), jnp.tile(k_gain, heads)]).reshape(1, 2 * a_width)
    dils = tuple(dil for _, dil in A_BRANCHES)
    v_list = _mm([hn], w_in, j, lambda c: v_cols // tn + c, a_width, BF16, dils=dils, seq=seq, tn=tn,
                 name="ab_in_v")
    u = _mm([hn], w_in, j, lambda c: (v_cols + a_width) // tn + c, d - a_width, F32, tn=tn, name="ab_in_u")
    slopes = _alibi_slopes(heads)
    outs, lses = [], []
    for g, dil in enumerate(dils):
        def wcol(c, g=g):
            return jnp.where(c < q_blocks, g * q_blocks + c, k_base + g * q_blocks + c - q_blocks)
        qk = _mm([hn], w_in, j, wcol, 2 * a_width, BF16, mode="headnorm", extra=gains, dils=(dil,), seq=seq,
                 tn=tn, name=f"ab_in_qk{g}")
        if dil == 1:
            qk = qk.reshape(bsz, 1, seq, 2 * a_width)
            v_g = v_list[g].reshape(bsz, 1, seq, a_width)
        else:
            v_g = v_list[g]
        o, lse = _band_attention(qk, v_g, bsz, seq, g, slopes[g])
        outs.append(o)
        lses.append(lse)
    a_out = _combine_branches(outs, lses)
    b_out = _pool_mixer(u, pool_w, pool_scale, seq)
    return _mm([a_out, b_out], w_out, j, lambda c: c, d, F32, mode="residual", extra=h, name="ab_out")


def _gla_layer(h, norm_gain, w_in, w_a2, b_a, o_gain, w_out, j, bsz, seq):
    d = h.shape[1]
    kw = w_a2.shape[2]
    vw = w_out.shape[1]
    tn = 512
    w_in_t = jnp.swapaxes(w_in, 1, 2)
    rank = w_a2.shape[1]
    w_r = jnp.pad(w_in_t[j, 2 * kw + 2 * vw:, :], ((0, LANES - rank), (0, 0)))
    w_a2p = jnp.pad(w_a2[j], ((0, LANES - rank), (0, 0)))
    hn, log_a = _gla_norm_log_decay(h, norm_gain, w_r, w_a2p, b_a[j])
    qk = _mm([hn], w_in_t, j, lambda c: c, 2 * kw, F32, w_t=True, tn=tn, name="c_in_qk")
    v = _mm([hn], w_in_t, j, lambda c: 2 * kw // tn + c, vw, BF16, w_t=True, tn=tn, name="c_in_v")
    gate = _mm([hn], w_in_t, j, lambda c: (2 * kw + vw) // tn + c, vw, F32, w_t=True, tn=tn, name="c_in_gate")
    o = _gla(qk, v, gate, log_a, o_gain[j], bsz, seq)
    return _mm([o], w_out, j, lambda c: c, d, F32, mode="residual", extra=h, name="c_out")


def _memory_kv(mem, gain, wkv, k_gain, layer, bsz):
    xw = wkv.shape[2] // 2
    tn = 256
    mem_n = _rmsnorm(mem, gain)
    rows = mem.shape[0]
    gains = jnp.tile(k_gain, xw // HEAD_DIM).reshape(1, xw)
    k = _mm([mem_n], wkv, layer, lambda c: c, xw, BF16, mode="headnorm", extra=gains, tm=rows, tn=tn, name="mem_k")
    v = _mm([mem_n], wkv, layer, lambda c: xw // tn + c, xw, BF16, tm=rows, tn=tn, name="mem_v")
    return k.reshape(bsz, rows // bsz, xw), v.reshape(bsz, rows // bsz, xw)


def kernel(x, mem, mix_norm, ab_w_in, ab_q_norm, ab_k_norm, ab_pool_w, ab_pool_scale, ab_w_out, c_w_in, c_w_a2, c_b_a, c_o_norm, c_w_out, x_norm, x_mem_norm, x_wq, x_wkv, x_q_norm, x_k_norm, x_wo, f_norm, f_w_up, f_conv_w, f_conv_b, f_w_down):
    bsz, seq, d = x.shape
    depth = mix_norm.shape[0]
    h = x.reshape(bsz * seq, d)
    mem2 = mem.reshape(bsz * mem.shape[1], d)
    for layer in range(depth):
        j = layer // 2
        if layer % 2 == 0:
            hn = _rmsnorm(h, mix_norm[layer])
            h = _dilated_pool_layer(h, hn, ab_w_in, ab_q_norm[j], ab_k_norm[j], ab_pool_w[j], ab_pool_scale[j],
                                    ab_w_out, j, bsz, seq)
        else:
            h = _gla_layer(h, mix_norm[layer], c_w_in, c_w_a2, c_b_a, c_o_norm, c_w_out, j, bsz, seq)
        k, v = _memory_kv(mem2, x_mem_norm[layer], x_wkv, x_k_norm[layer], layer, bsz)
        h, hn = _cross_attention(h, x_norm[layer], x_wq[layer].astype(BF16), x_q_norm[layer], k, v,
                                 x_wo[layer].astype(BF16), f_norm[layer], seq)
        act, w_down = _ffn_up(hn, f_w_up, f_conv_w, f_conv_b, f_w_down, layer, seq)
        h = _mm([act], w_down, None, lambda c: c, d, F32, mode="residual", extra=h, tm=512, name="ffn_down")
    return h.reshape(bsz, seq, d)
```

```python
import functools

import numpy as np
import jax
import jax.numpy as jnp
from jax import lax
from jax.experimental import pallas as pl
from jax.experimental.pallas import tpu as pltpu

F32 = jnp.float32
BF16 = jnp.bfloat16

LANES = 128
SUBLANES = 8
VMEM_LIMIT_BYTES = 56 * 2 ** 20

EPS = 1e-6
HEAD_DIM = 128
A_BRANCHES = ((128, 1), (512, 4), (2048, 16))
N_BRANCH = len(A_BRANCHES)
BAND_BLOCK = 128
POOL_WINDOWS = (2, 4, 8, 16)
POOL_HALO = 16
C_HEADS = 8
C_GATE_RANK = 16
C_GATE_TAU = 16.0
C_CHUNK = 64
X_HEADS = 4
CONV_WIDTH = 3
CONV_HALO = SUBLANES
MM_ROWS = 1024
FAST_ROW_STRIDE = 4
ATTN_STAGE_HEADS = {1: 8, 4: 16, 16: 1}
ATTN_PAIRS_PER_STEP = 64
FFN_ROWS = 2048


def _params(*semantics):
    return pltpu.CompilerParams(dimension_semantics=semantics, vmem_limit_bytes=VMEM_LIMIT_BYTES)


def _lane_groups(width):
    return [slice(c * LANES, (c + 1) * LANES) for c in range(width // LANES)]


def _rmsnorm_kernel(x_ref, g_ref, o_ref):
    x = x_ref[...].astype(F32)
    ms = jnp.mean(x * x, axis=-1, keepdims=True)
    o_ref[...] = (x * lax.rsqrt(ms + EPS) * g_ref[...]).astype(o_ref.dtype)


def _rmsnorm(x, gain, tm=512):
    t, d = x.shape
    return pl.pallas_call(
        _rmsnorm_kernel,
        grid=(t // tm,),
        in_specs=[pl.BlockSpec((tm, d), lambda i: (i, 0)),
                  pl.BlockSpec((1, d), lambda i: (0, 0))],
        out_specs=pl.BlockSpec((tm, d), lambda i: (i, 0)),
        out_shape=jax.ShapeDtypeStruct((t, d), BF16),
        compiler_params=_params("parallel"),
        name="rmsnorm",
    )(x, gain.reshape(1, d))


def _mm_kernel(*refs, n_a, mode, dils, w_t):
    a_refs = refs[:n_a]
    w_ref = refs[n_a]
    n_extra = 1 if mode in ("headnorm", "residual") else 0
    extra = refs[n_a + 1:n_a + 1 + n_extra]
    o_refs = refs[n_a + 1 + n_extra:n_a + 1 + n_extra + len(dils)]
    scratch = refs[n_a + 1 + n_extra + len(dils):]
    acc = None
    k0 = 0
    for a_ref in a_refs:
        kk = a_ref.shape[1]
        if w_t:
            part = lax.dot_general(a_ref[...], w_ref[:, k0:k0 + kk].astype(BF16), (((1,), (1,)), ((), ())),
                                   preferred_element_type=F32)
        else:
            part = jnp.dot(a_ref[...], w_ref[k0:k0 + kk, :].astype(BF16), preferred_element_type=F32)
        acc = part if acc is None else acc + part
        k0 += kk
    tm, tn = acc.shape
    for c, sl in enumerate(_lane_groups(tn)):
        blk = acc[:, sl]
        if mode == "headnorm":
            ms = jnp.mean(blk * blk, axis=-1, keepdims=True)
            blk = blk * lax.rsqrt(ms + EPS) * extra[0][:, sl]
        elif mode == "residual":
            blk = extra[0][:, sl] + blk
        if scratch:
            scratch[0][c] = blk
        for o_ref, dil in zip(o_refs, dils):
            if dil == 1:
                o_ref[:, sl] = blk.astype(o_ref.dtype)
    for o_ref, dil in zip(o_refs, dils):
        if 1 < dil <= FAST_ROW_STRIDE:
            for r in range(dil):
                for c, sl in enumerate(_lane_groups(tn)):
                    o_ref[r, :, sl] = scratch[0][c, pl.ds(r, tm // dil, stride=dil), :].astype(o_ref.dtype)
        elif dil > FAST_ROW_STRIDE:
            s1, s2 = FAST_ROW_STRIDE, dil // FAST_ROW_STRIDE
            part = tm // s1
            for q in range(s1):
                for c in range(tn // LANES):
                    scratch[1][c, q * part:(q + 1) * part, :] = scratch[0][c, pl.ds(q, part, stride=s1), :]
            for q in range(s1):
                for p in range(s2):
                    for c, sl in enumerate(_lane_groups(tn)):
                        o_ref[s1 * p + q, :, sl] = scratch[1][c, pl.ds(q * part + p, tm // dil, stride=s2),
                                                              :].astype(o_ref.dtype)


def _mm(a_list, w, layer, wcol, ncols, out_dtype, *, mode="plain", extra=None, dils=(1,), seq=None,
        w_t=False, tm=MM_ROWS, tn=512, name="mm"):
    t = a_list[0].shape[0]
    k_total = sum(a.shape[1] for a in a_list)
    assert w.shape[-1 if w_t else -2] == k_total and ncols % tn == 0 and t % tm == 0
    in_specs = [pl.BlockSpec((tm, a.shape[1]), lambda i, j: (i, 0)) for a in a_list]
    if w_t:
        in_specs.append(pl.BlockSpec((None, tn, k_total), lambda i, j: (layer, wcol(j), 0)))
    elif w.ndim == 3:
        in_specs.append(pl.BlockSpec((None, k_total, tn), lambda i, j: (layer, 0, wcol(j))))
    else:
        in_specs.append(pl.BlockSpec((k_total, tn), lambda i, j: (0, wcol(j))))
    args = list(a_list) + [w]
    if mode == "headnorm":
        in_specs.append(pl.BlockSpec((1, tn), lambda i, j: (0, j)))
        args.append(extra)
    elif mode == "residual":
        in_specs.append(pl.BlockSpec((tm, tn), lambda i, j: (i, j)))
        args.append(extra)
    out_specs, out_shapes = [], []
    for dil in dils:
        if dil == 1:
            out_specs.append(pl.BlockSpec((tm, tn), lambda i, j: (i, j)))
            out_shapes.append(jax.ShapeDtypeStruct((t, ncols), out_dtype))
        else:
            tiles = seq // tm
            assert seq % tm == 0 and tm % dil == 0
            out_specs.append(pl.BlockSpec((None, dil, tm // dil, tn), lambda i, j: (i // tiles, 0, i % tiles, j)))
            out_shapes.append(jax.ShapeDtypeStruct((t // seq, dil, seq // dil, ncols), out_dtype))
    assert all(dil <= FAST_ROW_STRIDE ** 2 and (dil <= FAST_ROW_STRIDE or dil % FAST_ROW_STRIDE == 0) for dil in dils)
    slab = pltpu.VMEM((tn // LANES, tm, LANES), F32)
    n_slabs = 2 if max(dils) > FAST_ROW_STRIDE else 1 if max(dils) > 1 else 0
    scratch = [slab] * n_slabs
    outs = pl.pallas_call(
        functools.partial(_mm_kernel, n_a=len(a_list), mode=mode, dils=tuple(dils), w_t=w_t),
        grid=(t // tm, ncols // tn),
        in_specs=in_specs,
        out_specs=out_specs,
        out_shape=out_shapes,
        scratch_shapes=scratch,
        compiler_params=_params("parallel", "arbitrary"),
        name=name,
    )(*args)
    return outs[0] if len(dils) == 1 else outs


def _alibi_slopes(heads):
    n = N_BRANCH * heads
    s = np.power(np.float32(2.0), -8.0 * np.arange(1, n + 1, dtype=np.float32) / np.float32(n)).astype(np.float32)
    return s.reshape(N_BRANCH, -1)


def _band_attn_kernel(q_ref, kp_ref, kc_ref, vp_ref, vc_ref, o_ref, lse_ref, *scratch, dilation, slopes,
                      stage_heads, sub):
    blk = BAND_BLOCK
    hg = pl.program_id(2)
    has_prev = pl.program_id(1) > 0
    n_heads = len(slopes[0])
    qi = lax.broadcasted_iota(jnp.int32, (blk, 2 * blk), 0)
    ki = lax.broadcasted_iota(jnp.int32, (blk, 2 * blk), 1)
    rel = qi + blk - ki
    in_band = jnp.logical_and(rel >= 0, rel <= blk)
    valid_first = jnp.logical_and(in_band, jnp.logical_or(ki >= blk, has_prev))
    dist = (rel * dilation).astype(F32)
    lane = lax.broadcasted_iota(jnp.int32, (blk, LANES), 1)
    scale = HEAD_DIM ** -0.5
    contract_last = (((1,), (1,)), ((), ()))
    head_cols = [slice(h * HEAD_DIM, (h + 1) * HEAD_DIM) for h in range(n_heads)]
    two_pass = dilation > FAST_ROW_STRIDE
    s1, s2 = FAST_ROW_STRIDE, dilation // FAST_ROW_STRIDE
    part = blk * dilation // s1
    head_slope = []
    for h in range(n_heads):
        slope = slopes[0][h]
        for g in range(1, len(slopes)):
            slope = jnp.where(hg == g, slopes[g][h], slope)
        head_slope.append(slope)
    lse_tiles = []
    for r, m in [(r, m) for r in range(dilation) for m in range(sub)]:
        rows = slice(m * blk, (m + 1) * blk)
        prev_rows = slice((m - 1) * blk, m * blk)
        valid = valid_first if m == 0 else in_band

        def keys(prev_ref, cur_ref, cols):
            prev = prev_ref[r, :, cols] if m == 0 else cur_ref[r, prev_rows, cols]
            return jnp.concatenate([prev, cur_ref[r, rows, cols]], axis=0)

        lse_tile = jnp.zeros((blk, LANES), F32)
        for h0 in range(0, n_heads, stage_heads):
            group = list(range(h0, min(h0 + stage_heads, n_heads)))
            scores = {h: lax.dot_general(q_ref[r, rows, head_cols[h]], keys(kp_ref, kc_ref, head_cols[h]),
                                         contract_last, preferred_element_type=F32) for h in group}
            probs, dens, maxes = {}, {}, {}
            for h in group:
                s = jnp.where(valid, scores[h] * scale - head_slope[h] * dist, -jnp.inf)
                maxes[h] = jnp.max(s, axis=-1, keepdims=True)
                p = jnp.exp(s - maxes[h])
                dens[h] = jnp.sum(p, axis=-1, keepdims=True)
                probs[h] = p.astype(BF16)
            outs = {h: jnp.dot(probs[h], keys(vp_ref, vc_ref, head_cols[h]), preferred_element_type=F32)
                    for h in group}
            for h in group:
                o = outs[h] / dens[h]
                lse_h = maxes[h] + jnp.log(dens[h])
                for g in range(len(slopes)):
                    lse_tile = jnp.where(jnp.logical_and(lane == g * n_heads + h, hg == g), lse_h, lse_tile)
                if dilation == 1:
                    o_ref[rows, head_cols[h]] = o.astype(o_ref.dtype)
                elif two_pass:
                    scratch[0][h, pl.ds((r % s1) * part + r // s1, blk, stride=s2), :] = o
                else:
                    scratch[0][h, pl.ds(r, blk, stride=dilation), :] = o
        if dilation == 1:
            lse_tiles.append(lse_tile)
        elif two_pass:
            scratch[1][pl.ds((r % s1) * part + r // s1, blk, stride=s2), :] = lse_tile
        else:
            scratch[1][pl.ds(r, blk, stride=dilation), :] = lse_tile
    if dilation == 1:
        lse_slab = jnp.concatenate(lse_tiles, axis=0)
    elif two_pass:
        for q in range(s1):
            for h in range(n_heads):
                scratch[2][h, pl.ds(q, part, stride=s1), :] = scratch[0][h, q * part:(q + 1) * part, :]
            scratch[3][pl.ds(q, part, stride=s1), :] = scratch[1][q * part:(q + 1) * part, :]
        for h in range(n_heads):
            o_ref[:, h * HEAD_DIM:(h + 1) * HEAD_DIM] = scratch[2][h].astype(o_ref.dtype)
        lse_slab = scratch[3][...]
    else:
        for h in range(n_heads):
            o_ref[:, h * HEAD_DIM:(h + 1) * HEAD_DIM] = scratch[0][h].astype(o_ref.dtype)
        lse_slab = scratch[1][...]

    @pl.when(hg == 0)
    def _():
        lse_ref[...] = lse_slab

    @pl.when(hg > 0)
    def _():
        lse_ref[...] += lse_slab


def _band_attention(qk, v, bsz, seq, branch, slopes):
    _, dilation = A_BRANCHES[branch]
    width = v.shape[-1]
    n_blk = seq // dilation // BAND_BLOCK
    heads = max(min(width // HEAD_DIM, ATTN_PAIRS_PER_STEP // dilation), 1)
    hw = heads * HEAD_DIM
    n_hg = width // hw
    sub = ATTN_PAIRS_PER_STEP // heads if dilation == 1 else 1
    assert n_blk % sub == 0
    n_steps = n_blk // sub
    slope_tab = tuple(tuple(float(s) for s in slopes[g * heads:(g + 1) * heads]) for g in range(n_hg))
    rows = BAND_BLOCK * dilation * sub
    blk = (None, dilation, sub * BAND_BLOCK, hw)
    prev_blk = (None, dilation, BAND_BLOCK, hw)

    def prev(n):
        return jnp.maximum(n * sub - 1, 0)

    scratch = []
    if dilation > 1:
        scratch = [pltpu.VMEM((heads, rows, HEAD_DIM), F32), pltpu.VMEM((rows, LANES), F32)]
        if dilation > FAST_ROW_STRIDE:
            scratch = scratch * 2
    return pl.pallas_call(
        functools.partial(_band_attn_kernel, dilation=dilation, slopes=slope_tab,
                          stage_heads=ATTN_STAGE_HEADS[dilation], sub=sub),
        grid=(bsz, n_steps, n_hg),
        in_specs=[pl.BlockSpec(blk, lambda b, n, g: (b, 0, n, g)),
                  pl.BlockSpec(prev_blk, lambda b, n, g: (b, 0, prev(n), n_hg + g)),
                  pl.BlockSpec(blk, lambda b, n, g: (b, 0, n, n_hg + g)),
                  pl.BlockSpec(prev_blk, lambda b, n, g: (b, 0, prev(n), g)),
                  pl.BlockSpec(blk, lambda b, n, g: (b, 0, n, g))],
        out_specs=[pl.BlockSpec((rows, hw), lambda b, n, g: (b * n_steps + n, g)),
                   pl.BlockSpec((rows, LANES), lambda b, n, g: (b * n_steps + n, 0))],
        out_shape=[jax.ShapeDtypeStruct((bsz * seq, width), BF16),
                   jax.ShapeDtypeStruct((bsz * seq, LANES), F32)],
        scratch_shapes=scratch,
        compiler_params=_params("parallel", "arbitrary", "arbitrary"),
        name=f"band_attn_d{dilation}",
    )(qk, qk, qk, v, v)


def _combine_kernel(o0_ref, o1_ref, o2_ref, l0_ref, l1_ref, l2_ref, out_ref):
    l0, l1, l2 = l0_ref[...], l1_ref[...], l2_ref[...]
    m = jnp.maximum(jnp.maximum(l0, l1), l2)
    e0, e1, e2 = jnp.exp(l0 - m), jnp.exp(l1 - m), jnp.exp(l2 - m)
    tot = e0 + e1 + e2
    w1, w2 = e1 / tot, e2 / tot
    for h in range(out_ref.shape[1] // HEAD_DIM):
        sl = slice(h * HEAD_DIM, (h + 1) * HEAD_DIM)
        o0 = o0_ref[:, sl].astype(F32)
        acc = o0 + w1[:, h:h + 1] * (o1_ref[:, sl].astype(F32) - o0) + w2[:, h:h + 1] * (o2_ref[:, sl].astype(F32) - o0)
        out_ref[:, sl] = acc.astype(out_ref.dtype)


def _combine_branches(outs, lses, tm=512):
    t, width = outs[0].shape
    o_spec = pl.BlockSpec((tm, width), lambda i: (i, 0))
    l_spec = pl.BlockSpec((tm, LANES), lambda i: (i, 0))
    return pl.pallas_call(
        _combine_kernel,
        grid=(t // tm,),
        in_specs=[o_spec] * 3 + [l_spec] * 3,
        out_specs=o_spec,
        out_shape=jax.ShapeDtypeStruct((t, width), BF16),
        compiler_params=_params("parallel"),
        name="combine_branches",
    )(*outs, *lses)


def _pool_kernel(u_ref, uh_ref, w_ref, sc_ref, o_ref, wb_ref, *, blocks_per_seq):
    i = pl.program_id(0)

    @pl.when(i == 0)
    def _():
        wb_ref[...] = w_ref[...].astype(BF16)

    tm = u_ref.shape[0]
    group = w_ref.shape[1]
    first = (i % blocks_per_seq) == 0
    row = lax.broadcasted_iota(jnp.int32, (tm, 1), 0)
    pos = (i % blocks_per_seq) * tm + row
    for g, win in enumerate(POOL_WINDOWS):
        sl = slice(g * group, (g + 1) * group)
        u = u_ref[:, sl]
        halo = jnp.where(first, 0.0, uh_ref[:, sl])
        s = jnp.concatenate([halo, u], axis=0)
        step = 1
        while step < win:
            s = s + pltpu.roll(s, step, axis=0)
            step *= 2
        count = jnp.minimum(pos + 1, win).astype(F32)
        pooled = s[POOL_HALO:] / count - u
        y = jnp.dot(pooled.astype(BF16), wb_ref[g], preferred_element_type=F32)
        o_ref[:, sl] = (y * sc_ref[:, sl]).astype(o_ref.dtype)


def _pool_mixer(u, pool_w, pool_scale, seq, tm=1024):
    t, width = u.shape
    n_group, group, _ = pool_w.shape
    blocks_per_seq = seq // tm
    halo_blocks = tm // POOL_HALO
    return pl.pallas_call(
        functools.partial(_pool_kernel, blocks_per_seq=blocks_per_seq),
        grid=(t // tm,),
        in_specs=[pl.BlockSpec((tm, width), lambda i: (i, 0)),
                  pl.BlockSpec((POOL_HALO, width), lambda i: (jnp.maximum(i * halo_blocks - 1, 0), 0)),
                  pl.BlockSpec((n_group, group, group), lambda i: (0, 0, 0)),
                  pl.BlockSpec((1, width), lambda i: (0, 0))],
        out_specs=pl.BlockSpec((tm, width), lambda i: (i, 0)),
        out_shape=jax.ShapeDtypeStruct((t, width), BF16),
        scratch_shapes=[pltpu.VMEM((n_group, group, group), BF16)],
        compiler_params=_params("arbitrary"),
        name="pool_mixer",
    )(u, u, pool_w, pool_scale.reshape(1, width))


def _split3(x):
    hi = x.astype(BF16)
    r1 = x - hi.astype(F32)
    mid = r1.astype(BF16)
    lo = (r1 - mid.astype(F32)).astype(BF16)
    return hi, mid, lo


def _gate_kernel(h_ref, ng_ref, wr_ref, wa2_ref, ba_ref, hn_ref, o_ref):
    x = h_ref[...]
    ms = jnp.mean(x * x, axis=-1, keepdims=True)
    hn = (x * lax.rsqrt(ms + EPS) * ng_ref[...]).astype(hn_ref.dtype)
    hn_ref[...] = hn
    r = lax.dot_general(hn, wr_ref[...].astype(BF16), (((1,), (1,)), ((), ())), preferred_element_type=F32)
    r_hi, r_mid, _ = _split3(r)
    w_hi, w_mid, _ = _split3(wa2_ref[...])
    g = (jnp.dot(r_hi, w_hi, preferred_element_type=F32)
         + (jnp.dot(r_hi, w_mid, preferred_element_type=F32) + jnp.dot(r_mid, w_hi, preferred_element_type=F32)))
    g = g + ba_ref[...]
    log_sig = jnp.minimum(g, 0.0) - jnp.log1p(jnp.exp(-jnp.abs(g)))
    o_ref[...] = log_sig / C_GATE_TAU


def _gla_norm_log_decay(h, norm_gain, w_r, w_a2, b_a, tm=512):
    t, d = h.shape
    kw = w_a2.shape[1]
    return pl.pallas_call(
        _gate_kernel,
        grid=(t // tm,),
        in_specs=[pl.BlockSpec((tm, d), lambda i: (i, 0)),
                  pl.BlockSpec((1, d), lambda i: (0, 0)),
                  pl.BlockSpec((LANES, d), lambda i: (0, 0)),
                  pl.BlockSpec((LANES, kw), lambda i: (0, 0)),
                  pl.BlockSpec((1, kw), lambda i: (0, 0))],
        out_specs=[pl.BlockSpec((tm, d), lambda i: (i, 0)),
                   pl.BlockSpec((tm, kw), lambda i: (i, 0))],
        out_shape=[jax.ShapeDtypeStruct((t, d), BF16),
                   jax.ShapeDtypeStruct((t, kw), F32)],
        compiler_params=_params("parallel"),
        name="gla_norm_log_decay",
    )(h, norm_gain.reshape(1, d), w_r, w_a2, b_a.reshape(1, kw))


def _gla_kernel(q_ref, k_ref, v_ref, la_ref, gate_ref, og_ref, o_ref, state_ref, *, chunks, heads):
    @pl.when(pl.program_id(2) == 0)
    def _():
        state_ref[...] = jnp.zeros_like(state_ref)

    c = C_CHUNK
    dk = q_ref.shape[1] // heads
    dv = v_ref.shape[1] // heads
    ri = lax.broadcasted_iota(jnp.int32, (c, c), 0)
    ci = lax.broadcasted_iota(jnp.int32, (c, c), 1)
    causal = ci <= ri
    tri = causal.astype(BF16)
    contract_last = (((1,), (1,)), ((), ()))
    contract_first = (((0,), (0,)), ((), ()))
    pairs = [(n, h) for n in range(chunks) for h in range(heads)]

    def rows(n):
        return slice(n * c, (n + 1) * c)

    def kcols(h):
        return slice(h * dk, (h + 1) * dk)

    def vcols(h):
        return slice(h * dv, (h + 1) * dv)

    bc = {}
    for n, h in pairs:
        la_hi, la_mid, la_lo = _split3(la_ref[rows(n), kcols(h)])
        bc[n, h] = (jnp.dot(tri, la_hi, preferred_element_type=F32)
                    + jnp.dot(tri, la_mid, preferred_element_type=F32)
                    + jnp.dot(tri, la_lo, preferred_element_type=F32))
    q_b, k_in, k_st, decay = {}, {}, {}, {}
    for n, h in pairs:
        b = bc[n, h]
        b_last = b[c - 1:c, :]
        q_b[n, h] = (q_ref[rows(n), kcols(h)] * (dk ** -0.5) * jnp.exp(b)).astype(BF16)
        k = k_ref[rows(n), kcols(h)]
        k_in[n, h] = (k * jnp.exp(-b)).astype(BF16)
        k_st[n, h] = (k * jnp.exp(b_last - b)).astype(BF16)
        decay[n, h] = jnp.exp(b_last)
    att = {}
    for p in pairs:
        a = lax.dot_general(q_b[p], k_in[p], contract_last, preferred_element_type=F32)
        att[p] = jnp.where(causal, a, 0.0).astype(BF16)
    o_intra, upd = {}, {}
    for n, h in pairs:
        v = v_ref[rows(n), vcols(h)]
        o_intra[n, h] = jnp.dot(att[n, h], v, preferred_element_type=F32)
        upd[n, h] = lax.dot_general(v, k_st[n, h], contract_first, preferred_element_type=F32)
    for h in range(heads):
        state = state_ref[h]
        for n in range(chunks):
            o = o_intra[n, h] + lax.dot_general(q_b[n, h], state.astype(BF16), contract_last,
                                                preferred_element_type=F32)
            state = state * decay[n, h] + upd[n, h]
            ms = jnp.mean(o * o, axis=-1, keepdims=True)
            gate = gate_ref[rows(n), vcols(h)]
            y = o * lax.rsqrt(ms + EPS) * og_ref[...] * (gate * jax.nn.sigmoid(gate))
            o_ref[rows(n), vcols(h)] = y.astype(o_ref.dtype)
        state_ref[h] = state


def _gla(qk, v, gate, log_a, o_gain, bsz, seq, rows=512, heads=2):
    t = bsz * seq
    dk = log_a.shape[1] // C_HEADS
    dv = v.shape[1] // C_HEADS
    steps = seq // rows
    groups = C_HEADS // heads

    def row_block(b, n):
        return b * steps + n

    return pl.pallas_call(
        functools.partial(_gla_kernel, chunks=rows // C_CHUNK, heads=heads),
        grid=(bsz, groups, steps),
        in_specs=[pl.BlockSpec((rows, heads * dk), lambda b, g, n: (row_block(b, n), g)),
                  pl.BlockSpec((rows, heads * dk), lambda b, g, n: (row_block(b, n), groups + g)),
                  pl.BlockSpec((rows, heads * dv), lambda b, g, n: (row_block(b, n), g)),
                  pl.BlockSpec((rows, heads * dk), lambda b, g, n: (row_block(b, n), g)),
                  pl.BlockSpec((rows, heads * dv), lambda b, g, n: (row_block(b, n), g)),
                  pl.BlockSpec((1, dv), lambda b, g, n: (0, 0))],
        out_specs=pl.BlockSpec((rows, heads * dv), lambda b, g, n: (row_block(b, n), g)),
        out_shape=jax.ShapeDtypeStruct((t, v.shape[1]), BF16),
        scratch_shapes=[pltpu.VMEM((heads, dv, dk), F32)],
        compiler_params=_params("parallel", "parallel", "arbitrary"),
        name="gla",
    )(qk, qk, v, log_a, gate, o_gain.reshape(1, dv))


def _xattn_kernel(h_ref, g_ref, wq_ref, qg_ref, k_ref, v_ref, wo_ref, fg_ref, o_ref, on_ref):
    x = h_ref[...]
    ms = jnp.mean(x * x, axis=-1, keepdims=True)
    hn = (x * lax.rsqrt(ms + EPS) * g_ref[...]).astype(BF16)
    q = jnp.dot(hn, wq_ref[...], preferred_element_type=F32)
    scale = HEAD_DIM ** -0.5
    contract_last = (((1,), (1,)), ((), ()))
    cols = [slice(hd * HEAD_DIM, (hd + 1) * HEAD_DIM) for hd in range(X_HEADS)]
    q_heads = []
    for sl in cols:
        qh = q[:, sl]
        qms = jnp.mean(qh * qh, axis=-1, keepdims=True)
        q_heads.append((qh * lax.rsqrt(qms + EPS) * qg_ref[...]).astype(BF16))
    scores = [lax.dot_general(qh, k_ref[0, :, sl], contract_last, preferred_element_type=F32) * scale
              for qh, sl in zip(q_heads, cols)]
    probs = []
    for s in scores:
        m = jnp.max(s, axis=-1, keepdims=True)
        p = jnp.exp(s - m)
        probs.append((p / jnp.sum(p, axis=-1, keepdims=True)).astype(BF16))
    heads = [jnp.dot(p, v_ref[0, :, sl], preferred_element_type=F32) for p, sl in zip(probs, cols)]
    o = jnp.concatenate(heads, axis=-1).astype(BF16)
    y = x + jnp.dot(o, wo_ref[...], preferred_element_type=F32)
    o_ref[...] = y
    yms = jnp.mean(y * y, axis=-1, keepdims=True)
    on_ref[...] = (y * lax.rsqrt(yms + EPS) * fg_ref[...]).astype(on_ref.dtype)


def _cross_attention(h, norm_gain, wq, q_gain, kv, wo, next_gain, seq, tm=512):
    t, d = h.shape
    xw = wq.shape[1]
    mlen = kv.shape[1]
    tiles_per_seq = seq // tm
    return pl.pallas_call(
        _xattn_kernel,
        grid=(t // tm,),
        in_specs=[pl.BlockSpec((tm, d), lambda i: (i, 0)),
                  pl.BlockSpec((1, d), lambda i: (0, 0)),
                  pl.BlockSpec((d, xw), lambda i: (0, 0)),
                  pl.BlockSpec((1, HEAD_DIM), lambda i: (0, 0)),
                  pl.BlockSpec((1, mlen, xw), lambda i: (i // tiles_per_seq, 0, 0)),
                  pl.BlockSpec((1, mlen, xw), lambda i: (i // tiles_per_seq, 0, 1)),
                  pl.BlockSpec((xw, d), lambda i: (0, 0)),
                  pl.BlockSpec((1, d), lambda i: (0, 0))],
        out_specs=[pl.BlockSpec((tm, d), lambda i: (i, 0)),
                   pl.BlockSpec((tm, d), lambda i: (i, 0))],
        out_shape=[jax.ShapeDtypeStruct((t, d), F32),
                   jax.ShapeDtypeStruct((t, d), BF16)],
        compiler_params=_params("parallel"),
        name="cross_attention",
    )(h, norm_gain.reshape(1, d), wq, q_gain.reshape(1, HEAD_DIM), kv, kv, wo, next_gain.reshape(1, d))


def _ffn_up_kernel(a_ref, ah_ref, wg_ref, wv_ref, cwg_ref, cwv_ref, cbg_ref, cbv_ref, wd_ref, o_ref, wdo_ref, *,
                   blocks_per_seq):
    wdo_ref[...] = wd_ref[...].astype(wdo_ref.dtype)
    tm = a_ref.shape[0]
    first = (pl.program_id(0) % blocks_per_seq) == 0
    row = lax.broadcasted_iota(jnp.int32, (tm, 1), 0)
    a = a_ref[...]
    ah = ah_ref[...]

    def conv_half(w_ref, cw_ref, cb_ref):
        w = w_ref[...].astype(BF16)
        u = jnp.dot(a, w, preferred_element_type=F32)
        uh = jnp.where(first, 0.0, jnp.dot(ah, w, preferred_element_type=F32))
        u1 = jnp.where(row == 0, uh[CONV_HALO - 1:CONV_HALO], pltpu.roll(u, 1, axis=0))
        u2 = jnp.where(row == 0, uh[CONV_HALO - 2:CONV_HALO - 1],
                       jnp.where(row == 1, uh[CONV_HALO - 1:CONV_HALO], pltpu.roll(u, 2, axis=0)))
        cw = cw_ref[...]
        return cb_ref[...] + cw[0:1] * u2 + cw[1:2] * u1 + cw[2:3] * u

    cg = conv_half(wg_ref, cwg_ref, cbg_ref)
    cv = conv_half(wv_ref, cwv_ref, cbv_ref)
    o_ref[...] = (cg * jax.nn.sigmoid(cg) * cv).astype(o_ref.dtype)


def _ffn_up(hn, w_up, conv_w, conv_b, w_down, layer, seq, tm=FFN_ROWS, tn=256):
    t, d = hn.shape
    d_ff = w_up.shape[2] // 2
    n_tiles = d_ff // tn
    n_steps = (t // tm) * n_tiles
    slab = d_ff // n_steps
    assert d_ff % n_steps == 0 and slab % (2 * SUBLANES) == 0
    blocks_per_seq = seq // tm
    halo_blocks = tm // CONV_HALO
    cw = conv_w[layer]
    cb = conv_b[layer].reshape(1, 2 * d_ff)
    return pl.pallas_call(
        functools.partial(_ffn_up_kernel, blocks_per_seq=blocks_per_seq),
        grid=(t // tm, n_tiles),
        in_specs=[pl.BlockSpec((tm, d), lambda i, j: (i, 0), pipeline_mode=pl.Buffered(1)),
                  pl.BlockSpec((CONV_HALO, d), lambda i, j: (jnp.maximum(i * halo_blocks - 1, 0), 0)),
                  pl.BlockSpec((None, d, tn), lambda i, j: (layer, 0, j)),
                  pl.BlockSpec((None, d, tn), lambda i, j: (layer, 0, n_tiles + j)),
                  pl.BlockSpec((CONV_WIDTH, tn), lambda i, j: (0, j)),
                  pl.BlockSpec((CONV_WIDTH, tn), lambda i, j: (0, n_tiles + j)),
                  pl.BlockSpec((1, tn), lambda i, j: (0, j)),
                  pl.BlockSpec((1, tn), lambda i, j: (0, n_tiles + j)),
                  pl.BlockSpec((None, slab, d), lambda i, j: (layer, i * n_tiles + j, 0))],
        out_specs=[pl.BlockSpec((tm, tn), lambda i, j: (i, j)),
                   pl.BlockSpec((slab, d), lambda i, j: (i * n_tiles + j, 0))],
        out_shape=[jax.ShapeDtypeStruct((t, d_ff), BF16),
                   jax.ShapeDtypeStruct((d_ff, d), BF16)],
        compiler_params=_params("parallel", "arbitrary"),
        name="ffn_up",
    )(hn, hn, w_up, w_up, cw, cw, cb, cb, w_down)


def _dilated_pool_layer(h, hn, w_in, q_gain, k_gain, pool_w, pool_scale, w_out, j, bsz, seq):
    d = h.shape[1]
    a_width = d // 2
    heads = a_width // HEAD_DIM
    tn = 512
    q_blocks = a_width // tn
    k_base = N_BRANCH * q_blocks
    v_cols = 2 * N_BRANCH * a_width
    gains = jnp.concatenate([jnp.tile(q_gain, heads), jnp.tile(k_gain, heads)]).reshape(1, 2 * a_width)
    dils = tuple(dil for _, dil in A_BRANCHES)
    v_list = _mm([hn], w_in, j, lambda c: v_cols // tn + c, a_width, BF16, dils=dils, seq=seq, tn=tn,
                 name="ab_in_v")
    u = _mm([hn], w_in, j, lambda c: (v_cols + a_width) // tn + c, d - a_width, F32, tn=tn, name="ab_in_u")
    slopes = _alibi_slopes(heads)
    outs, lses = [], []
    for g, dil in enumerate(dils):
        def wcol(c, g=g):
            return jnp.where(c < q_blocks, g * q_blocks + c, k_base + g * q_blocks + c - q_blocks)
        qk = _mm([hn], w_in, j, wcol, 2 * a_width, BF16, mode="headnorm", extra=gains, dils=(dil,), seq=seq,
                 tn=tn, name=f"ab_in_qk{g}")
        if dil == 1:
            qk = qk.reshape(bsz, 1, seq, 2 * a_width)
            v_g = v_list[g].reshape(bsz, 1, seq, a_width)
        else:
            v_g = v_list[g]
        o, lse = _band_attention(qk, v_g, bsz, seq, g, slopes[g])
        outs.append(o)
        lses.append(lse)
    a_out = _combine_branches(outs, lses)
    b_out = _pool_mixer(u, pool_w, pool_scale, seq)
    return _mm([a_out, b_out], w_out, j, lambda c: c, d, F32, mode="residual", extra=h, name="ab_out")


def _gla_layer(h, norm_gain, w_in, w_a2, b_a, o_gain, w_out, j, bsz, seq):
    d = h.shape[1]
    kw = w_a2.shape[2]
    vw = w_out.shape[1]
    tn = 512
    w_in_t = jnp.swapaxes(w_in, 1, 2)
    rank = w_a2.shape[1]
    w_r = jnp.pad(w_in_t[j, 2 * kw + 2 * vw:, :], ((0, LANES - rank), (0, 0)))
    w_a2p = jnp.pad(w_a2[j], ((0, LANES - rank), (0, 0)))
    hn, log_a = _gla_norm_log_decay(h, norm_gain, w_r, w_a2p, b_a[j])
    qk = _mm([hn], w_in_t, j, lambda c: c, 2 * kw, F32, w_t=True, tn=tn, name="c_in_qk")
    v = _mm([hn], w_in_t, j, lambda c: 2 * kw // tn + c, vw, BF16, w_t=True, tn=tn, name="c_in_v")
    gate = _mm([hn], w_in_t, j, lambda c: (2 * kw + vw) // tn + c, vw, F32, w_t=True, tn=tn, name="c_in_gate")
    o = _gla(qk, v, gate, log_a, o_gain[j], bsz, seq)
    return _mm([o], w_out, j, lambda c: c, d, F32, mode="residual", extra=h, name="c_out")


def _memory_kv_kernel(m_ref, g_ref, w_ref, kg_ref, o_ref, *, k_tiles):
    x = m_ref[...]
    ms = jnp.mean(x * x, axis=-1, keepdims=True)
    mem_n = (x * lax.rsqrt(ms + EPS) * g_ref[...]).astype(BF16)
    acc = jnp.dot(mem_n, w_ref[...].astype(BF16), preferred_element_type=F32)
    is_key = pl.program_id(0) < k_tiles
    for sl in _lane_groups(acc.shape[1]):
        blk = acc[:, sl]
        hms = jnp.mean(blk * blk, axis=-1, keepdims=True)
        normed = blk * lax.rsqrt(hms + EPS) * kg_ref[...]
        o_ref[:, sl] = jnp.where(is_key, normed, blk).astype(o_ref.dtype)


def _memory_kv(mem, gain, wkv, k_gain, layer, bsz, tn=256):
    rows, d = mem.shape
    xw = wkv.shape[2] // 2
    kv = pl.pallas_call(
        functools.partial(_memory_kv_kernel, k_tiles=xw // tn),
        grid=(2 * xw // tn,),
        in_specs=[pl.BlockSpec((rows, d), lambda j: (0, 0)),
                  pl.BlockSpec((1, d), lambda j: (0, 0)),
                  pl.BlockSpec((None, d, tn), lambda j: (layer, 0, j)),
                  pl.BlockSpec((1, HEAD_DIM), lambda j: (0, 0))],
        out_specs=pl.BlockSpec((rows, tn), lambda j: (0, j)),
        out_shape=jax.ShapeDtypeStruct((rows, 2 * xw), BF16),
        compiler_params=_params("arbitrary"),
        name="memory_kv",
    )(mem, gain.reshape(1, d), wkv, k_gain.reshape(1, HEAD_DIM))
    return kv.reshape(bsz, rows // bsz, 2 * xw)


def kernel(x, mem, mix_norm, ab_w_in, ab_q_norm, ab_k_norm, ab_pool_w, ab_pool_scale, ab_w_out, c_w_in, c_w_a2, c_b_a, c_o_norm, c_w_out, x_norm, x_mem_norm, x_wq, x_wkv, x_q_norm, x_k_norm, x_wo, f_norm, f_w_up, f_conv_w, f_conv_b, f_w_down):
    bsz, seq, d = x.shape
    depth = mix_norm.shape[0]
    h = x.reshape(bsz * seq, d)
    mem2 = mem.reshape(bsz * mem.shape[1], d)
    for layer in range(depth):
        j = layer // 2
        if layer % 2 == 0:
            hn = _rmsnorm(h, mix_norm[layer])
            h = _dilated_pool_layer(h, hn, ab_w_in, ab_q_norm[j], ab_k_norm[j], ab_pool_w[j], ab_pool_scale[j],
                                    ab_w_out, j, bsz, seq)
        else:
            h = _gla_layer(h, mix_norm[layer], c_w_in, c_w_a2, c_b_a, c_o_norm, c_w_out, j, bsz, seq)
        kv = _memory_kv(mem2, x_mem_norm[layer], x_wkv, x_k_norm[layer], layer, bsz)
        h, hn = _cross_attention(h, x_norm[layer], x_wq[layer].astype(BF16), x_q_norm[layer], kv,
                                 x_wo[layer].astype(BF16), f_norm[layer], seq)
        act, w_down = _ffn_up(hn, f_w_up, f_conv_w, f_conv_b, f_w_down, layer, seq)
        h = _mm([act], w_down, None, lambda c: c, d, F32, mode="residual", extra=h, tm=512, name="ffn_down")
    return h.reshape(bsz, seq, d)
```
